```python
import jax
import jax.numpy as jnp
from jax import lax
import numpy as np

D_MODEL = 4096
BATCH = 2
SEQ = 8192
DEPTH = 4

CTX_LEN = 256
GRID_W = 64
N_MOD = 6
ADA_RANK = 512
RWKV_HEAD = 64
RWKV_WIDTH = D_MODEL // 2
RWKV_HEADS = RWKV_WIDTH // RWKV_HEAD
DECAY_RANK = 64
ICLR_RANK = 64
OGATE_RANK = 128
VRES_RANK = 32
FOURIER_WIDTH = D_MODEL // 4
FOURIER_GROUPS = 8
CONV_WIDTH = D_MODEL // 4
CONV_K = 3
FOUR_OFF = 3 * RWKV_WIDTH
CONV_OFF = FOUR_OFF + FOURIER_WIDTH
IN_COLS = CONV_OFF + 3 * CONV_WIDTH
N_BRANCH = 3
GATE_RANK = 256
D_FF = 2 * D_MODEL
N_EXPERTS = 8
TOP_K = 2
D_EXPERT = D_FF // N_EXPERTS
N_DENSE = (DEPTH + 1) // 2
N_MOE = DEPTH // 2
NORM_EPS = 1e-6
GN_EPS = 64e-5
DECAY_OFFSET = 0.5

kernel_name = 'hybrid_rwkv7_fourier_shortconv_moe_dit'


def rmsnorm(x, w):
    xf = x.astype(jnp.float32)
    y = xf * lax.rsqrt(jnp.mean(xf * xf, axis=-1, keepdims=True) + NORM_EPS)
    return (y * w.astype(jnp.float32)).astype(x.dtype)


def modulate(x, w, shift, scale):
    return rmsnorm(x, w) * (1 + scale) + shift


def ada_modulation(cond, a, b, bias):
    m = (cond @ a) @ b + bias
    return jnp.split(m[:, None, :], m.shape[-1] // D_MODEL, axis=-1)


def seq_shift(t):
    half = t.shape[-1] // 2
    prev = jnp.pad(t[:, :-1, :half], ((0, 0), (1, 0), (0, 0)))
    nxt = jnp.pad(t[:, 1:, half:], ((0, 0), (0, 1), (0, 0)))
    return jnp.concatenate([prev, nxt], axis=-1)


def grid_shift(t, rows):
    b, n, ch = t.shape
    q = ch // 4
    g = t.reshape(b, rows, GRID_W, ch)
    from_left = jnp.pad(g[:, :, :-1, :q], ((0, 0), (0, 0), (1, 0), (0, 0)))
    from_right = jnp.pad(g[:, :, 1:, q:2 * q], ((0, 0), (0, 0), (0, 1), (0, 0)))
    from_up = jnp.pad(g[:, :-1, :, 2 * q:3 * q], ((0, 0), (1, 0), (0, 0), (0, 0)))
    from_down = jnp.pad(g[:, 1:, :, 3 * q:], ((0, 0), (0, 1), (0, 0), (0, 0)))
    return jnp.concatenate([from_left, from_right, from_up, from_down], axis=-1).reshape(b, n, ch)


def split_heads(t):
    return t.reshape(t.shape[:-1] + (RWKV_HEADS, RWKV_HEAD))


def wkv_scan(r, w, k, v, kk, a, s0, reverse):
    def tm(t):
        return jnp.moveaxis(t, 1, 0)

    def update(S, w_t, k_t, v_t, kk_t, a_t):
        sa = jnp.einsum('bhij,bhj->bhi', S, kk_t)
        return (S * w_t[:, :, None, :] - sa[..., None] * (kk_t * a_t)[:, :, None, :]
                + v_t[..., None] * k_t[:, :, None, :])

    if r is None:
        def step_state(S, inp):
            return update(S, *inp), None
        S, _ = lax.scan(step_state, s0, (tm(w), tm(k), tm(v), tm(kk), tm(a)), reverse=reverse)
        return None, S

    def step(S, inp):
        S = update(S, *inp[1:])
        return S, jnp.einsum('bhij,bhj->bhi', S, inp[0])

    S, ys = lax.scan(step, s0, (tm(r), tm(w), tm(k), tm(v), tm(kk), tm(a)), reverse=reverse)
    return jnp.moveaxis(ys, 0, 1), S


def rwkv_time_mix(h, p_r, p_k, p_v, shift, prm, vres, v_first, s0):
    f32 = jnp.float32
    b, t_len, _ = h.shape
    dh = shift(h) - h

    def lerp(p, mu):
        return p + (shift(p) - p) * mu

    mu_rkv, mu_lr = prm['mu_rkv'], prm['mu_lr']
    k = lerp(p_k, mu_rkv[1])
    v = lerp(p_v, mu_rkv[2])
    if vres is not None:
        xv = h + dh * vres['mu']
        v = v + (v_first - v) * jax.nn.sigmoid(vres['v0'] + (xv @ vres['v1']) @ vres['v2'])
    kk = split_heads(k * prm['k_k']).astype(f32)
    kk = kk * lax.rsqrt(jnp.maximum(jnp.sum(kk * kk, axis=-1, keepdims=True), 1e-24))
    vh = split_heads(v).astype(f32)
    r = None if p_r is None else split_heads(lerp(p_r, mu_rkv[0])).astype(f32)
    xw = h + dh * mu_lr[0]
    xa = h + dh * mu_lr[1]
    ys, states, ks = [], [], []
    for d, reverse in enumerate((False, True)):
        wl = (prm['w0'][d] + jnp.tanh(xw @ prm['w1'][d]) @ prm['w2'][d]).astype(f32)
        decay = jnp.exp(-jnp.exp(-jax.nn.softplus(-wl) - DECAY_OFFSET))
        a = jax.nn.sigmoid((prm['a0'][d] + (xa @ prm['a1'][d]) @ prm['a2'][d]).astype(f32))
        kd = k.astype(f32) * (1.0 + (a - 1.0) * prm['k_a'].astype(f32))
        init = jnp.zeros((b, RWKV_HEADS, RWKV_HEAD, RWKV_HEAD), f32) if s0 is None else s0[d]
        y_d, s_d = wkv_scan(r, split_heads(decay), split_heads(kd), vh, kk, split_heads(a), init, reverse)
        ys.append(y_d)
        states.append(s_d)
        ks.append(kd)
    if r is None:
        return None, (states[0], states[1]), v
    y = ys[0] + ys[1]
    mean = jnp.mean(y, axis=-1, keepdims=True)
    var = jnp.mean(jnp.square(y - mean), axis=-1, keepdims=True)
    y = ((y - mean) * lax.rsqrt(var + GN_EPS)).reshape(b, t_len, RWKV_WIDTH)
    y = y * prm['gn_w'].astype(f32) + prm['gn_b'].astype(f32)
    bonus = jnp.sum(r * split_heads(ks[0] + ks[1]) * prm['r_k'].astype(f32), axis=-1, keepdims=True) * vh
    y = y + bonus.reshape(b, t_len, RWKV_WIDTH)
    xg = h + dh * mu_lr[2]
    g = jax.nn.sigmoid(xg @ prm['g1']) @ prm['g2']
    return y.astype(h.dtype) * g, (states[0], states[1]), v


def fourier_mix(u):
    b, n, _ = u.shape
    g = u.astype(jnp.float32).reshape(b, n, FOURIER_GROUPS, FOURIER_WIDTH // FOURIER_GROUPS)
    f = jnp.fft.fftn(g, axes=(1, 3), norm='ortho').real
    return f.reshape(b, n, FOURIER_WIDTH).astype(u.dtype)


def depthwise_conv(u, w):
    pad = CONV_K // 2
    n = u.shape[1]
    up = jnp.pad(u, ((0, 0), (pad, pad), (0, 0)))
    out = up[:, 0:n] * w[0]
    for i in range(1, CONV_K):
        out = out + up[:, i:i + n] * w[i]
    return out


def fourier_conv_branches(p, conv_w):
    y_four = fourier_mix(p[..., FOUR_OFF:CONV_OFF])
    gate_b = p[..., CONV_OFF:CONV_OFF + CONV_WIDTH]
    gate_c = p[..., CONV_OFF + CONV_WIDTH:CONV_OFF + 2 * CONV_WIDTH]
    u = p[..., CONV_OFF + 2 * CONV_WIDTH:]
    y_conv = gate_b * depthwise_conv(gate_c * u, conv_w)
    return y_four, y_conv


def gated_merge(h, y_rwkv, y_four, y_conv, g_w1, g_w2, g_b, p_rwkv, p_four, p_conv, w_o):
    gates = jax.nn.sigmoid((h @ g_w1) @ g_w2 + g_b)
    g_r, g_f, g_c = jnp.split(gates, N_BRANCH, axis=-1)
    merged = g_r * (y_rwkv @ p_rwkv) + g_f * (y_four @ p_four) + g_c * (y_conv @ p_conv)
    return merged @ w_o


def swiglu(h, w1, w3, w2):
    return (jax.nn.silu(h @ w1) * (h @ w3)) @ w2


def moe_swiglu(h, rw, rb, w1, w3, w2):
    logits = (h @ rw + rb).astype(jnp.float32)
    top_val, top_idx = lax.top_k(logits, TOP_K)
    weights = jax.nn.softmax(top_val, axis=-1)
    gate = jnp.sum(jax.nn.one_hot(top_idx, N_EXPERTS, dtype=jnp.float32) * weights[..., None], axis=-2).astype(h.dtype)
    hid = jax.nn.silu(jnp.einsum('btd,edf->btef', h, w1)) * jnp.einsum('btd,edf->btef', h, w3)
    return jnp.einsum('btef,efd->btd', hid * gate[..., None], w2)


def setup_inputs(seed: int = 0) -> dict:
    key = jax.random.key(seed)
    keys = iter(jax.random.split(key, 64))

    def nrm(shape, scale):
        return jax.random.normal(next(keys), shape, jnp.float32) * scale

    def uni(shape, lo, hi):
        return jax.random.uniform(next(keys), shape, jnp.float32, lo, hi)

    D, L, RW = D_MODEL, DEPTH, RWKV_WIDTH
    return {
        'x': nrm((BATCH, SEQ, D), 1.0),
        'c': nrm((BATCH, D), 1.0),
        'ctx': nrm((BATCH, CTX_LEN, D), 1.0),
        'c_ctx': nrm((D,), 1.0),
        'ada_a': nrm((L, D, ADA_RANK), D ** -0.5),
        'ada_b': nrm((L, ADA_RANK, N_MOD * D), 0.2 * ADA_RANK ** -0.5),
        'ada_bias': nrm((L, N_MOD * D), 0.02),
        'norm1_w': 1.0 + nrm((L, D), 0.02),
        'norm2_w': 1.0 + nrm((L, D), 0.02),
        'w_in': nrm((L, D, IN_COLS), D ** -0.5),
        'mu_rkv': uni((L, 3, RW), 0.0, 1.0),
        'mu_lr': uni((L, 3, D), 0.0, 1.0),
        'decay_w0': uni((L, 2, RW), -6.0, -1.0),
        'decay_w1': nrm((L, 2, D, DECAY_RANK), D ** -0.5),
        'decay_w2': nrm((L, 2, DECAY_RANK, RW), 0.1 * DECAY_RANK ** -0.5),
        'iclr_a0': nrm((L, 2, RW), 0.1),
        'iclr_a1': nrm((L, 2, D, ICLR_RANK), D ** -0.5),
        'iclr_a2': nrm((L, 2, ICLR_RANK, RW), 0.5 * ICLR_RANK ** -0.5),
        'ogate_g1': nrm((L, D, OGATE_RANK), D ** -0.5),
        'ogate_g2': nrm((L, OGATE_RANK, RW), OGATE_RANK ** -0.5),
        'k_k': 0.85 + nrm((L, RW), 0.02),
        'k_a': 1.0 + nrm((L, RW), 0.02),
        'r_k': nrm((L, RWKV_HEADS, RWKV_HEAD), 0.1),
        'gn_w': 1.0 + nrm((L, RW), 0.02),
        'gn_b': nrm((L, RW), 0.02),
        'vres_mu': uni((L - 1, D), 0.0, 1.0),
        'vres_v0': nrm((L - 1, RW), 0.1),
        'vres_v1': nrm((L - 1, D, VRES_RANK), D ** -0.5),
        'vres_v2': nrm((L - 1, VRES_RANK, RW), 0.5 * VRES_RANK ** -0.5),
        'conv_w': nrm((L, CONV_K, CONV_WIDTH), CONV_K ** -0.5),
        'gate_w1': nrm((L, D, GATE_RANK), D ** -0.5),
        'gate_w2': nrm((L, GATE_RANK, N_BRANCH * D), GATE_RANK ** -0.5),
        'gate_b': nrm((L, N_BRANCH * D), 0.1),
        'proj_rwkv': nrm((L, RW, D), RW ** -0.5),
        'proj_fourier': nrm((L, FOURIER_WIDTH, D), FOURIER_WIDTH ** -0.5),
        'proj_conv': nrm((L, CONV_WIDTH, D), CONV_WIDTH ** -0.5),
        'w_out': nrm((L, D, D), D ** -0.5),
        'ffn_w1': nrm((N_DENSE, D, D_FF), D ** -0.5),
        'ffn_w3': nrm((N_DENSE, D, D_FF), D ** -0.5),
        'ffn_w2': nrm((N_DENSE, D_FF, D), D_FF ** -0.5),
        'router_w': nrm((N_MOE, D, N_EXPERTS), D ** -0.5),
        'router_b': nrm((N_MOE, N_EXPERTS), 0.01),
        'moe_w1': nrm((N_MOE, N_EXPERTS, D, D_EXPERT), D ** -0.5),
        'moe_w3': nrm((N_MOE, N_EXPERTS, D, D_EXPERT), D ** -0.5),
        'moe_w2': nrm((N_MOE, N_EXPERTS, D_EXPERT, D), D_EXPERT ** -0.5),
        'final_norm_w': 1.0 + nrm((D,), 0.02),
    }


def reference(x, c, ctx, c_ctx, ada_a, ada_b, ada_bias, norm1_w, norm2_w, w_in,
              mu_rkv, mu_lr, decay_w0, decay_w1, decay_w2, iclr_a0, iclr_a1, iclr_a2,
              ogate_g1, ogate_g2, k_k, k_a, r_k, gn_w, gn_b,
              vres_mu, vres_v0, vres_v1, vres_v2, conv_w,
              gate_w1, gate_w2, gate_b, proj_rwkv, proj_fourier, proj_conv, w_out,
              ffn_w1, ffn_w3, ffn_w2, router_w, router_b, moe_w1, moe_w3, moe_w2,
              final_norm_w):
    rows = x.shape[1] // GRID_W
    shift_lat = lambda t: grid_shift(t, rows)
    cond_lat = jax.nn.silu(c)
    cond_ctx = jax.nn.silu(c_ctx)[None, :]
    rw_ = RWKV_WIDTH
    xl, xc = x, ctx
    v_first_l = None
    v_first_c = None
    for l in range(DEPTH):
        last = l == DEPTH - 1
        prm = dict(mu_rkv=mu_rkv[l], mu_lr=mu_lr[l], w0=decay_w0[l], w1=decay_w1[l], w2=decay_w2[l],
                   a0=iclr_a0[l], a1=iclr_a1[l], a2=iclr_a2[l], g1=ogate_g1[l], g2=ogate_g2[l],
                   k_k=k_k[l], k_a=k_a[l], r_k=r_k[l], gn_w=gn_w[l], gn_b=gn_b[l])
        vres = None if l == 0 else dict(mu=vres_mu[l - 1], v0=vres_v0[l - 1], v1=vres_v1[l - 1], v2=vres_v2[l - 1])
        j = l // 2
        if l % 2 == 0:
            ffn = lambda h, j=j: swiglu(h, ffn_w1[j], ffn_w3[j], ffn_w2[j])
        else:
            ffn = lambda h, j=j: moe_swiglu(h, router_w[j], router_b[j], moe_w1[j], moe_w3[j], moe_w2[j])
        mod_l = ada_modulation(cond_lat, ada_a[l], ada_b[l], ada_bias[l])
        n_ctx_cols = (2 if last else N_MOD) * D_MODEL
        mod_c = ada_modulation(cond_ctx, ada_a[l], ada_b[l][:, :n_ctx_cols], ada_bias[l][:n_ctx_cols])

        hc = modulate(xc, norm1_w[l], mod_c[0], mod_c[1])
        if last:
            pkv = hc @ w_in[l][:, rw_:3 * rw_]
            pc = None
            pc_r, pc_k, pc_v = None, pkv[..., :rw_], pkv[..., rw_:]
        else:
            pc = hc @ w_in[l]
            pc_r, pc_k, pc_v = pc[..., :rw_], pc[..., rw_:2 * rw_], pc[..., 2 * rw_:3 * rw_]
        yc_rwkv, ctx_states, vc = rwkv_time_mix(hc, pc_r, pc_k, pc_v, seq_shift, prm, vres, v_first_c, None)

        hl = modulate(xl, norm1_w[l], mod_l[0], mod_l[1])
        pl = hl @ w_in[l]
        yl_rwkv, _, vl = rwkv_time_mix(hl, pl[..., :rw_], pl[..., rw_:2 * rw_], pl[..., 2 * rw_:3 * rw_],
                                       shift_lat, prm, vres, v_first_l, ctx_states)
        if l == 0:
            v_first_c, v_first_l = vc, vl
        yl_four, yl_conv = fourier_conv_branches(pl, conv_w[l])
        mix_l = gated_merge(hl, yl_rwkv, yl_four, yl_conv, gate_w1[l], gate_w2[l], gate_b[l],
                            proj_rwkv[l], proj_fourier[l], proj_conv[l], w_out[l])
        xl = xl + mod_l[2] * mix_l
        xl = xl + mod_l[5] * ffn(modulate(xl, norm2_w[l], mod_l[3], mod_l[4]))

        if not last:
            yc_four, yc_conv = fourier_conv_branches(pc, conv_w[l])
            mix_c = gated_merge(hc, yc_rwkv, yc_four, yc_conv, gate_w1[l], gate_w2[l], gate_b[l],
                                proj_rwkv[l], proj_fourier[l], proj_conv[l], w_out[l])
            xc = xc + mod_c[2] * mix_c
            xc = xc + mod_c[5] * ffn(modulate(xc, norm2_w[l], mod_c[3], mod_c[4]))
    return rmsnorm(xl, final_norm_w)
```

```python
import functools
import math

import jax
import jax.numpy as jnp
from jax import lax
from jax.experimental import pallas as pl
from jax.experimental.pallas import tpu as pltpu

F32 = jnp.float32
BF16 = jnp.bfloat16

GRID_W = 64
FOURIER_GROUPS = 8
TOP_K = 2
NORM_EPS = 1e-6
GN_EPS = 64e-5
DECAY_OFFSET = 0.5
WKV_CHUNK = 64
LANE = 128
VMEM_LIMIT = 56 * 1024 * 1024

HI = lax.Precision.HIGHEST


def _pick(dim, target, align):
    if dim <= target:
        return dim
    t = (target // align) * align
    while t >= align:
        if dim % t == 0:
            return t
        t -= align
    return dim


def _mm_kernel(a_ref, b_ref, o_ref, acc_ref, *, nk):
    k = pl.program_id(2)

    @pl.when(k == 0)
    def _():
        acc_ref[...] = jnp.zeros_like(acc_ref)

    acc_ref[...] += jnp.dot(a_ref[...], b_ref[...], preferred_element_type=F32)

    @pl.when(k == nk - 1)
    def _():
        o_ref[...] = acc_ref[...].astype(o_ref.dtype)


def matmul(a, b, out_dtype=F32, tm=1024, tn=1024, tk=2048):
    m = a.shape[0]
    k, n = b.shape
    tm = _pick(m, tm, 16)
    tn = _pick(n, tn, LANE)
    tk = _pick(k, tk, LANE)
    nk = k // tk
    return pl.pallas_call(
        functools.partial(_mm_kernel, nk=nk),
        grid=(m // tm, n // tn, nk),
        in_specs=[pl.BlockSpec((tm, tk), lambda i, j, kk: (i, kk)),
                  pl.BlockSpec((tk, tn), lambda i, j, kk: (kk, j))],
        out_specs=pl.BlockSpec((tm, tn), lambda i, j, kk: (i, j)),
        out_shape=jax.ShapeDtypeStruct((m, n), out_dtype),
        scratch_shapes=[pltpu.VMEM((tm, tn), F32)],
        compiler_params=pltpu.CompilerParams(
            dimension_semantics=("parallel", "parallel", "arbitrary"),
            vmem_limit_bytes=VMEM_LIMIT),
        name="matmul",
    )(a, b)


def mm(a, b, out_dtype=F32, **kw):
    lead = a.shape[:-1]
    out = matmul(a.reshape(-1, a.shape[-1]).astype(BF16), b.astype(BF16), out_dtype, **kw)
    return out.reshape(lead + (b.shape[-1],))


def _dot(a, b):
    return jnp.dot(a, b, precision=HI, preferred_element_type=F32)


def _dot_nt(a, b):
    return lax.dot_general(a, b, (((1,), (1,)), ((), ())), precision=HI, preferred_element_type=F32)


def _dot_tn(a, b):
    return lax.dot_general(a, b, (((0,), (0,)), ((), ())), precision=HI, preferred_element_type=F32)


def _unit_tri_inverse(a_tri, row, col, size):
    def same_block(shift):
        return (row >> shift) == (col >> shift)

    eye = (row == col).astype(F32)
    a8 = jnp.where(same_block(3), a_tri, 0.0)
    a8_2 = _dot(a8, a8)
    a8_4 = _dot(a8_2, a8_2)
    x = eye - a8
    x = x + _dot(x, a8_2)
    x = x + _dot(x, a8_4)
    shift = 3
    while (1 << shift) < size:
        e = jnp.where(same_block(shift + 1) & jnp.logical_not(same_block(shift)), a_tri, 0.0)
        x = x - _dot(x, _dot(e, x))
        shift += 1
    return x


def _wkv_kernel(r_ref, lw_ref, k_ref, v_ref, kk_ref, a_ref, s0_ref, y_ref, sf_ref, st_ref,
                *, chunk, hd, heads, reverse, nc):
    c = pl.program_id(2)

    @pl.when(c == 0)
    def _():
        st_ref[...] = s0_ref[0]

    row = lax.broadcasted_iota(jnp.int32, (chunk, chunk), 0)
    col = lax.broadcasted_iota(jnp.int32, (chunk, chunk), 1)
    if reverse:
        strict, incl = col > row, col >= row
    else:
        strict, incl = col < row, col <= row
    tri = incl.astype(F32)
    last = 0 if reverse else chunk - 1

    for h in range(heads):
        sl = slice(h * hd, (h + 1) * hd)
        r = r_ref[0, :, sl]
        lw = lw_ref[0, :, sl]
        k = k_ref[0, :, sl]
        v = v_ref[0, :, sl]
        kk = kk_ref[0, :, sl]
        a = a_ref[0, :, sl]
        st0 = st_ref[h]

        cum = _dot(tri, lw)
        p_in = jnp.exp(cum)
        p_inv = jnp.exp(-cum)
        p_ex = jnp.exp(cum - lw)
        rt = r * p_in
        kt = k * p_inv
        beta = a * kk * p_inv
        alpha = kk * p_ex

        a_ab = jnp.where(strict, _dot_nt(alpha, beta), 0.0)
        a_ak = jnp.where(strict, _dot_nt(alpha, kt), 0.0)
        a_rk = jnp.where(incl, _dot_nt(rt, kt), 0.0)
        a_rb = jnp.where(incl, _dot_nt(rt, beta), 0.0)
        t_inv = _unit_tri_inverse(a_ab, row, col, chunk)

        u = _dot(t_inv, _dot(alpha, st0) + _dot(a_ak, v))
        y = _dot(rt, st0) + _dot(a_rk, v) - _dot(a_rb, u)
        y_ref[0, :, sl] = y

        p_last = p_in[last:last + 1, :]
        st_new = st0 + _dot_tn(kt, v) - _dot_tn(beta, u)
        st_ref[h] = st_new * jnp.transpose(p_last)

    @pl.when(c == nc - 1)
    def _():
        sf_ref[0] = st_ref[...]


def wkv(r, lw, k, v, kk, a, s0t, reverse, hd):
    b, t, width = r.shape
    nh = width // hd
    heads = 2 if nh % 2 == 0 else 1
    chunk = WKV_CHUNK
    nc = t // chunk
    blk = heads * hd

    def tok_map(bi, hi, ci):
        return (bi, nc - 1 - ci if reverse else ci, hi)

    tok_spec = pl.BlockSpec((1, chunk, blk), tok_map)
    st_spec = pl.BlockSpec((1, heads, hd, hd), lambda bi, hi, ci: (bi, hi, 0, 0))
    return pl.pallas_call(
        functools.partial(_wkv_kernel, chunk=chunk, hd=hd, heads=heads, reverse=reverse, nc=nc),
        grid=(b, nh // heads, nc),
        in_specs=[tok_spec] * 6 + [st_spec],
        out_specs=[tok_spec, st_spec],
        out_shape=[jax.ShapeDtypeStruct((b, t, width), F32),
                   jax.ShapeDtypeStruct((b, nh, hd, hd), F32)],
        scratch_shapes=[pltpu.VMEM((heads, hd, hd), F32)],
        compiler_params=pltpu.CompilerParams(
            dimension_semantics=("parallel", "parallel", "arbitrary"),
            vmem_limit_bytes=VMEM_LIMIT),
        name="wkv_rev" if reverse else "wkv_fwd",
    )(r, lw, k, v, kk, a, s0t)


def rmsnorm(x, w):
    y = x * lax.rsqrt(jnp.mean(x * x, axis=-1, keepdims=True) + NORM_EPS)
    return y * w


def modulate(x, w, shift, scale):
    return rmsnorm(x, w) * (1 + scale) + shift


def seq_shift(t):
    half = t.shape[-1] // 2
    prev = jnp.pad(t[:, :-1, :half], ((0, 0), (1, 0), (0, 0)))
    nxt = jnp.pad(t[:, 1:, half:], ((0, 0), (0, 1), (0, 0)))
    return jnp.concatenate([prev, nxt], axis=-1)


def grid_shift(t):
    b, n, ch = t.shape
    q = ch // 4
    g = t.reshape(b, n // GRID_W, GRID_W, ch)
    from_left = jnp.pad(g[:, :, :-1, :q], ((0, 0), (0, 0), (1, 0), (0, 0)))
    from_right = jnp.pad(g[:, :, 1:, q:2 * q], ((0, 0), (0, 0), (0, 1), (0, 0)))
    from_up = jnp.pad(g[:, :-1, :, 2 * q:3 * q], ((0, 0), (1, 0), (0, 0), (0, 0)))
    from_down = jnp.pad(g[:, 1:, :, 3 * q:], ((0, 0), (0, 1), (0, 0), (0, 0)))
    return jnp.concatenate([from_left, from_right, from_up, from_down], axis=-1).reshape(b, n, ch)


def dft_tables(n, scale):
    j = lax.broadcasted_iota(jnp.int32, (n, n), 0)
    k = lax.broadcasted_iota(jnp.int32, (n, n), 1)
    ang = ((j * k) % n).astype(F32) * (2.0 * math.pi / n)
    return jnp.cos(ang) * scale, jnp.sin(ang) * scale


def fourier_mix(u, pos_table, chan_table):
    b, n, fw = u.shape
    gcs = mm(u, chan_table, BF16)
    gcs = gcs.reshape(b, n, 2, fw).transpose(0, 2, 1, 3).reshape(b, 2 * n, fw)
    out = [matmul(pos_table, gcs[i]) for i in range(b)]
    return jnp.stack(out, axis=0)


def depthwise_conv(u, w):
    n = u.shape[1]
    kk = w.shape[0]
    pad = kk // 2
    up = jnp.pad(u, ((0, 0), (pad, pad), (0, 0)))
    out = up[:, 0:n] * w[0]
    for i in range(1, kk):
        out = out + up[:, i:i + n] * w[i]
    return out


def kernel(x, c, ctx, c_ctx, ada_a, ada_b, ada_bias, norm1_w, norm2_w, w_in, mu_rkv, mu_lr, decay_w0, decay_w1, decay_w2, iclr_a0, iclr_a1, iclr_a2, ogate_g1, ogate_g2, k_k, k_a, r_k, gn_w, gn_b, vres_mu, vres_v0, vres_v1, vres_v2, conv_w, gate_w1, gate_w2, gate_b, proj_rwkv, proj_fourier, proj_conv, w_out, ffn_w1, ffn_w3, ffn_w2, router_w, router_b, moe_w1, moe_w3, moe_w2, final_norm_w):
    depth = w_in.shape[0]
    bsz, seq, d = x.shape
    ctx_len = ctx.shape[1]
    rw = mu_rkv.shape[-1]
    nh, hd = r_k.shape[1], r_k.shape[2]
    fw = proj_fourier.shape[1]
    cw = proj_conv.shape[1]
    four_off = 3 * rw
    conv_off = four_off + fw
    n_exp = router_w.shape[-1]
    d_exp = moe_w1.shape[-1]
    n_mod = ada_b.shape[-1] // d
    r_dec, r_icl, r_og, r_vr, r_gate = (decay_w1.shape[-1], iclr_a1.shape[-1], ogate_g1.shape[-1],
                                        vres_v1.shape[-1], gate_w1.shape[-1])

    gc = fw // FOURIER_GROUPS
    cc, cs = dft_tables(gc, gc ** -0.5)
    eye_g = jnp.eye(FOURIER_GROUPS, dtype=F32)
    chan_table = jnp.concatenate([jnp.kron(eye_g, cc), jnp.kron(eye_g, cs)], axis=1).astype(BF16)

    def pos_table(n):
        pc, ps = dft_tables(n, n ** -0.5)
        return jnp.concatenate([pc, -ps], axis=1).astype(BF16)

    pos_lat, pos_ctx = pos_table(seq), pos_table(ctx_len)

    cond_lat = jax.nn.silu(c)
    cond_ctx = jax.nn.silu(c_ctx)[None, :]
    cond = jnp.concatenate([cond_lat, cond_ctx], axis=0)
    cond = jnp.pad(cond, ((0, 16 - cond.shape[0] % 16), (0, 0)))

    def tokens(xs, mod, l, shift, pos_tab, s0, v_first, last_ctx):
        b, t, _ = xs.shape
        has_vres = l > 0
        h = modulate(xs, norm1_w[l], mod[0], mod[1])
        dh = shift(h) - h
        hd_cat = jnp.concatenate([h, dh], axis=-1).astype(BF16).reshape(b * t, 2 * d)

        w_h = [decay_w1[l, 0], decay_w1[l, 1], iclr_a1[l, 0], iclr_a1[l, 1], ogate_g1[l], gate_w1[l]]
        mus = [mu_lr[l, 0], mu_lr[l, 0], mu_lr[l, 1], mu_lr[l, 1], mu_lr[l, 2], None]
        if has_vres:
            w_h.append(vres_v1[l - 1])
            mus.append(vres_mu[l - 1])
        w_dh = [jnp.zeros_like(w) if m is None else w * m[:, None] for w, m in zip(w_h, mus)]
        w_low = jnp.concatenate([jnp.concatenate(w_h, axis=1), jnp.concatenate(w_dh, axis=1)], axis=0)
        n_low = w_low.shape[1]
        w_low = jnp.pad(w_low, ((0, 0), (0, -n_low % LANE)))
        low = matmul(hd_cat, w_low.astype(BF16)).reshape(b, t, -1)
        offs = [0]
        for w in w_h:
            offs.append(offs[-1] + w.shape[1])
        low_w = [low[..., offs[0]:offs[1]], low[..., offs[1]:offs[2]]]
        low_a = [low[..., offs[2]:offs[3]], low[..., offs[3]:offs[4]]]
        low_g = low[..., offs[4]:offs[5]]
        low_gate = low[..., offs[5]:offs[6]]

        p = matmul(hd_cat, w_in[l].astype(BF16)).reshape(b, t, -1)

        def lerp(q, mu):
            return q + (shift(q) - q) * mu

        p_r, p_k, p_v = p[..., :rw], p[..., rw:2 * rw], p[..., 2 * rw:3 * rw]
        k = lerp(p_k, mu_rkv[l, 1])
        v = lerp(p_v, mu_rkv[l, 2])
        if has_vres:
            low_v = low[..., offs[6]:offs[7]]
            v = v + (v_first - v) * jax.nn.sigmoid(vres_v0[l - 1] + mm(low_v, vres_v2[l - 1]))
        kk = (k * k_k[l]).reshape(b, t, nh, hd)
        kk = kk * lax.rsqrt(jnp.maximum(jnp.sum(kk * kk, axis=-1, keepdims=True), 1e-24))
        kk = kk.reshape(b, t, rw)
        r = lerp(p_r, mu_rkv[l, 0])

        ys, states, ks = [], [], []
        for di, reverse in enumerate((False, True)):
            wl = decay_w0[l, di] + mm(jnp.tanh(low_w[di]), decay_w2[l, di])
            lw = -jnp.exp(-jax.nn.softplus(-wl) - DECAY_OFFSET)
            a = jax.nn.sigmoid(iclr_a0[l, di] + mm(low_a[di], iclr_a2[l, di]))
            kd = k * (1.0 + (a - 1.0) * k_a[l])
            init = jnp.zeros((b, nh, hd, hd), F32) if s0 is None else s0[di]
            y_d, s_d = wkv(r, lw, kd, v, kk, a, init, reverse, hd)
            ys.append(y_d)
            states.append(s_d)
            ks.append(kd)
        if last_ctx:
            return None, states, v

        y = (ys[0] + ys[1]).reshape(b, t, nh, hd)
        mean = jnp.mean(y, axis=-1, keepdims=True)
        var = jnp.mean(jnp.square(y - mean), axis=-1, keepdims=True)
        y = ((y - mean) * lax.rsqrt(var + GN_EPS)).reshape(b, t, rw)
        y = y * gn_w[l] + gn_b[l]
        rk = (r * (ks[0] + ks[1])).reshape(b, t, nh, hd) * r_k[l]
        bonus = jnp.sum(rk, axis=-1, keepdims=True) * v.reshape(b, t, nh, hd)
        y = y + bonus.reshape(b, t, rw)
        g = mm(jax.nn.sigmoid(low_g), ogate_g2[l])
        y_rwkv = y * g

        y_four = fourier_mix(p[..., four_off:conv_off], pos_tab, chan_table)
        gate_b_ = p[..., conv_off:conv_off + cw]
        gate_c_ = p[..., conv_off + cw:conv_off + 2 * cw]
        uu = p[..., conv_off + 2 * cw:]
        y_conv = gate_b_ * depthwise_conv(gate_c_ * uu, conv_w[l])

        gates = jax.nn.sigmoid(mm(low_gate, gate_w2[l]) + gate_b[l])
        g_r, g_f, g_c = jnp.split(gates, 3, axis=-1)
        merged = (g_r * mm(y_rwkv, proj_rwkv[l]) + g_f * mm(y_four, proj_fourier[l])
                  + g_c * mm(y_conv, proj_conv[l]))
        mix = mm(merged, w_out[l])
        xs = xs + mod[2] * mix

        h2 = modulate(xs, norm2_w[l], mod[3], mod[4])
        j = l // 2
        if l % 2 == 0:
            hid = jax.nn.silu(mm(h2, ffn_w1[j])) * mm(h2, ffn_w3[j])
            f = mm(hid, ffn_w2[j])
        else:
            logits = jnp.dot(h2, router_w[j], precision=HI) + router_b[j]
            top_val, top_idx = lax.top_k(logits, TOP_K)
            weights = jax.nn.softmax(top_val, axis=-1)
            gate = jnp.sum(jax.nn.one_hot(top_idx, n_exp, dtype=F32) * weights[..., None], axis=-2)
            w1 = moe_w1[j].transpose(1, 0, 2).reshape(d, n_exp * d_exp)
            w3 = moe_w3[j].transpose(1, 0, 2).reshape(d, n_exp * d_exp)
            w2 = moe_w2[j].reshape(n_exp * d_exp, d)
            hid = jax.nn.silu(mm(h2, w1)) * mm(h2, w3)
            hid = hid * jnp.repeat(gate, d_exp, axis=-1)
            f = mm(hid, w2)
        xs = xs + mod[5] * f
        return xs, states, v

    xl, xc = x, ctx
    v_first_l = v_first_c = None
    for l in range(depth):
        last = l == depth - 1
        m = mm(mm(cond, ada_a[l]), ada_b[l]) + ada_bias[l]
        mod_l = [m[:bsz, None, i * d:(i + 1) * d] for i in range(n_mod)]
        mod_c = [m[bsz:bsz + 1, None, i * d:(i + 1) * d] for i in range(n_mod)]

        xc_new, ctx_states, vc = tokens(xc, mod_c, l, seq_shift, pos_ctx, None, v_first_c, last)
        xl, _, vl = tokens(xl, mod_l, l, grid_shift, pos_lat, ctx_states, v_first_l, False)
        if l == 0:
            v_first_c, v_first_l = vc, vl
        if not last:
            xc = xc_new
    return rmsnorm(xl, final_norm_w)
```

```python
import functools
import math

import jax
import jax.numpy as jnp
from jax import lax
from jax.experimental import pallas as pl
from jax.experimental.pallas import tpu as pltpu

F32 = jnp.float32
BF16 = jnp.bfloat16

GRID_W = 64
FOURIER_GROUPS = 8
TOP_K = 2
NORM_EPS = 1e-6
GN_EPS = 64e-5
DECAY_OFFSET = 0.5
WKV_CHUNK = 64
WKV_HEADS = 8
INV_PASSES = 3
LANE = 128
VMEM_LIMIT = 56 * 1024 * 1024

HI = lax.Precision.HIGHEST


def _pick(dim, target, align):
    if dim <= target:
        return dim
    t = (target // align) * align
    while t >= align:
        if dim % t == 0:
            return t
        t -= align
    return dim


def _mm_kernel(a_ref, b_ref, o_ref, acc_ref, *, nk):
    k = pl.program_id(2)

    @pl.when(k == 0)
    def _():
        acc_ref[...] = jnp.zeros_like(acc_ref)

    acc_ref[...] += jnp.dot(a_ref[...], b_ref[...], preferred_element_type=F32)

    @pl.when(k == nk - 1)
    def _():
        o_ref[...] = acc_ref[...].astype(o_ref.dtype)


def matmul(a, b, out_dtype=F32, tm=1024, tn=1024, tk=2048):
    m = a.shape[0]
    k, n = b.shape
    tm = _pick(m, tm, 16)
    tn = _pick(n, tn, LANE)
    tk = _pick(k, tk, LANE)
    nk = k // tk
    return pl.pallas_call(
        functools.partial(_mm_kernel, nk=nk),
        grid=(m // tm, n // tn, nk),
        in_specs=[pl.BlockSpec((tm, tk), lambda i, j, kk: (i, kk)),
                  pl.BlockSpec((tk, tn), lambda i, j, kk: (kk, j))],
        out_specs=pl.BlockSpec((tm, tn), lambda i, j, kk: (i, j)),
        out_shape=jax.ShapeDtypeStruct((m, n), out_dtype),
        scratch_shapes=[pltpu.VMEM((tm, tn), F32)],
        compiler_params=pltpu.CompilerParams(
            dimension_semantics=("parallel", "parallel", "arbitrary"),
            vmem_limit_bytes=VMEM_LIMIT),
        name="matmul",
    )(a, b)


def mm(a, b, out_dtype=F32, **kw):
    lead = a.shape[:-1]
    out = matmul(a.reshape(-1, a.shape[-1]).astype(BF16), b.astype(BF16), out_dtype, **kw)
    return out.reshape(lead + (b.shape[-1],))


def _params(sem):
    return pltpu.CompilerParams(dimension_semantics=sem, vmem_limit_bytes=VMEM_LIMIT)


def _swiglu_up_kernel(a_ref, w1_ref, w3_ref, *rest, nk, gated):
    if gated:
        g_ref, o_ref, acc1_ref, acc3_ref = rest
    else:
        o_ref, acc1_ref, acc3_ref = rest
    k = pl.program_id(2)

    @pl.when(k == 0)
    def _():
        acc1_ref[...] = jnp.zeros_like(acc1_ref)
        acc3_ref[...] = jnp.zeros_like(acc3_ref)

    a = a_ref[...]
    acc1_ref[...] += jnp.dot(a, w1_ref[...], preferred_element_type=F32)
    acc3_ref[...] += jnp.dot(a, w3_ref[...], preferred_element_type=F32)

    @pl.when(k == nk - 1)
    def _():
        h1 = acc1_ref[...]
        hid = h1 * jax.nn.sigmoid(h1) * acc3_ref[...]
        if gated:
            hid = hid * jnp.tile(g_ref[...], (1, hid.shape[1] // LANE))
        o_ref[...] = hid.astype(o_ref.dtype)


def swiglu_up(a, w1, w3, gate_rep=None):
    m, k = a.shape
    n_e, _, f = w1.shape
    n = n_e * f
    tm = _pick(m, 1024, 16)
    tk = _pick(k, 2048, LANE)
    tn = _pick(f, 1024, LANE)
    gated = gate_rep is not None
    nk = k // tk
    per = f // tn
    w_spec = pl.BlockSpec((None, tk, tn), lambda i, j, kk: (j // per, kk, j % per))
    in_specs = [pl.BlockSpec((tm, tk), lambda i, j, kk: (i, kk)), w_spec, w_spec]
    args = [a, w1, w3]
    if gated:
        in_specs.append(pl.BlockSpec((tm, LANE), lambda i, j, kk: (i, j // per)))
        args.append(gate_rep)
    return pl.pallas_call(
        functools.partial(_swiglu_up_kernel, nk=nk, gated=gated),
        grid=(m // tm, n // tn, nk),
        in_specs=in_specs,
        out_specs=pl.BlockSpec((tm, tn), lambda i, j, kk: (i, j)),
        out_shape=jax.ShapeDtypeStruct((m, n), BF16),
        scratch_shapes=[pltpu.VMEM((tm, tn), F32), pltpu.VMEM((tm, tn), F32)],
        compiler_params=_params(("parallel", "parallel", "arbitrary")),
        name="swiglu_up",
    )(*args)


def _resid_kernel(a_ref, w_ref, x_ref, g_ref, o_ref, acc_ref, *, nk):
    k = pl.program_id(2)

    @pl.when(k == 0)
    def _():
        acc_ref[...] = jnp.zeros_like(acc_ref)

    acc_ref[...] += jnp.dot(a_ref[...], w_ref[...], preferred_element_type=F32)

    @pl.when(k == nk - 1)
    def _():
        o_ref[...] = x_ref[...] + g_ref[0] * acc_ref[...]


def resid_matmul(a, w, x, g, rows_per_batch):
    m, k = a.shape
    n = w.shape[1]
    tm = _pick(rows_per_batch, 1024, 16)
    tn = _pick(n, 1024, LANE)
    tk = _pick(k, 2048, LANE)
    nk = k // tk
    per = rows_per_batch // tm
    return pl.pallas_call(
        functools.partial(_resid_kernel, nk=nk),
        grid=(m // tm, n // tn, nk),
        in_specs=[pl.BlockSpec((tm, tk), lambda i, j, kk: (i, kk)),
                  pl.BlockSpec((tk, tn), lambda i, j, kk: (kk, j)),
                  pl.BlockSpec((tm, tn), lambda i, j, kk: (i, j)),
                  pl.BlockSpec((1, 1, tn), lambda i, j, kk: (i // per, 0, j))],
        out_specs=pl.BlockSpec((tm, tn), lambda i, j, kk: (i, j)),
        out_shape=jax.ShapeDtypeStruct((m, n), F32),
        scratch_shapes=[pltpu.VMEM((tm, tn), F32)],
        compiler_params=_params(("parallel", "parallel", "arbitrary")),
        name="resid_matmul",
    )(a, w, x, g)


def _merge_kernel(yr_ref, yf_ref, yc_ref, lg_ref, pr_ref, pf_ref, pc_ref,
                  gwr_ref, gwf_ref, gwc_ref, gbr_ref, gbf_ref, gbc_ref, o_ref):
    lg = lg_ref[...]

    def branch(y_ref, p_ref, gw_ref, gb_ref):
        gate = jax.nn.sigmoid(jnp.dot(lg, gw_ref[...], preferred_element_type=F32) + gb_ref[...])
        return gate * jnp.dot(y_ref[...], p_ref[...], preferred_element_type=F32)

    out = (branch(yr_ref, pr_ref, gwr_ref, gbr_ref) + branch(yf_ref, pf_ref, gwf_ref, gbf_ref)
           + branch(yc_ref, pc_ref, gwc_ref, gbc_ref))
    o_ref[...] = out.astype(o_ref.dtype)


def gated_merge(yr, yf, yc, lg, pr, pf, pc, gw2, gb):
    m = yr.shape[0]
    d = pr.shape[1]
    tm = _pick(m, 1024, 16)
    tn = _pick(d, 512, LANE)
    nj = d // tn

    def rows(arr):
        return pl.BlockSpec((tm, arr.shape[1]), lambda i, j: (i, 0))

    def cols(arr, off):
        return pl.BlockSpec((arr.shape[0], tn), lambda i, j: (0, off * nj + j))

    return pl.pallas_call(
        _merge_kernel,
        grid=(m // tm, nj),
        in_specs=[rows(yr), rows(yf), rows(yc), rows(lg), cols(pr, 0), cols(pf, 0), cols(pc, 0),
                  cols(gw2, 0), cols(gw2, 1), cols(gw2, 2), cols(gb, 0), cols(gb, 1), cols(gb, 2)],
        out_specs=pl.BlockSpec((tm, tn), lambda i, j: (i, j)),
        out_shape=jax.ShapeDtypeStruct((m, d), BF16),
        compiler_params=_params(("parallel", "parallel")),
        name="gated_merge",
    )(yr, yf, yc, lg, pr, pf, pc, gw2, gw2, gw2, gb, gb, gb)


def _bf(x):
    return x.astype(BF16)


def _dot(a, b, passes=1):
    if passes == 1:
        return jnp.dot(_bf(a), _bf(b), preferred_element_type=F32)
    a_hi, b_hi = _bf(a), _bf(b)
    a_lo, b_lo = _bf(a - a_hi.astype(F32)), _bf(b - b_hi.astype(F32))
    return (jnp.dot(a_hi, b_hi, preferred_element_type=F32)
            + jnp.dot(a_hi, b_lo, preferred_element_type=F32)
            + jnp.dot(a_lo, b_hi, preferred_element_type=F32))


def _dot_nt(a, b):
    return lax.dot_general(_bf(a), _bf(b), (((1,), (1,)), ((), ())), preferred_element_type=F32)


def _dot_tn(a, b):
    return lax.dot_general(_bf(a), _bf(b), (((0,), (0,)), ((), ())), preferred_element_type=F32)


def _each(f, *lists):
    return [f(*xs) for xs in zip(*lists)]


def _unit_tri_inverse(a_tri, row, col, size):
    def same_block(shift):
        return (row >> shift) == (col >> shift)

    def idot(a, b):
        return _dot(a, b, INV_PASSES)

    eye = (row == col).astype(F32)
    a8 = _each(lambda a: jnp.where(same_block(3), a, 0.0), a_tri)
    a8_2 = _each(idot, a8, a8)
    a8_4 = _each(idot, a8_2, a8_2)
    x = _each(lambda a: eye - a, a8)
    x = _each(lambda xx, d: xx + d, x, _each(idot, x, a8_2))
    x = _each(lambda xx, d: xx + d, x, _each(idot, x, a8_4))
    shift = 3
    while (1 << shift) < size:
        off = same_block(shift + 1) & jnp.logical_not(same_block(shift))
        e = _each(lambda a: jnp.where(off, a, 0.0), a_tri)
        ex = _each(idot, e, x)
        x = _each(lambda xx, d: xx - d, x, _each(idot, x, ex))
        shift += 1
    return x


def _wkv_kernel(r_ref, lw_ref, k_ref, v_ref, kk_ref, a_ref, s0_ref, y_ref, sf_ref,
                st_ref, al_ref, be_ref, rt_ref, kt_ref, vb_ref, pl_ref,
                *, chunk, hd, heads, reverse, nc):
    c = pl.program_id(2)

    @pl.when(c == 0)
    def _():
        st_ref[...] = s0_ref[0]

    row = lax.broadcasted_iota(jnp.int32, (chunk, chunk), 0)
    col = lax.broadcasted_iota(jnp.int32, (chunk, chunk), 1)
    if reverse:
        strict, incl = col > row, col >= row
    else:
        strict, incl = col < row, col <= row
    tri = jnp.where(incl, 1.0, 0.0).astype(BF16)
    last = 0 if reverse else chunk - 1

    lw = lw_ref[0]
    lw_hi = _bf(lw)
    rem = lw - lw_hi.astype(F32)
    lw_mid = _bf(rem)
    lw_lo = _bf(rem - lw_mid.astype(F32))
    cum = (jnp.dot(tri, lw_hi, preferred_element_type=F32)
           + jnp.dot(tri, lw_mid, preferred_element_type=F32)
           + jnp.dot(tri, lw_lo, preferred_element_type=F32))
    p_in = jnp.exp(cum)
    p_inv = jnp.exp(-cum)
    kk = kk_ref[0]
    rt_ref[...] = _bf(r_ref[0] * p_in)
    kt_ref[...] = _bf(k_ref[0] * p_inv)
    be_ref[...] = _bf(a_ref[0] * kk * p_inv)
    al_ref[...] = _bf(kk * jnp.exp(cum - lw))
    vb_ref[...] = _bf(v_ref[0])
    pl_ref[...] = jnp.broadcast_to(p_in[last:last + 1, :], pl_ref.shape)

    sls = [slice(h * hd, (h + 1) * hd) for h in range(heads)]
    alpha = [al_ref[:, sl] for sl in sls]
    beta = [be_ref[:, sl] for sl in sls]
    rt = [rt_ref[:, sl] for sl in sls]
    kt = [kt_ref[:, sl] for sl in sls]
    v = [vb_ref[:, sl] for sl in sls]
    st0 = [st_ref[h] for h in range(heads)]
    st0_b = _each(_bf, st0)

    ar = _each(lambda x, y: jnp.concatenate([x, y], axis=0), alpha, rt)
    x_b = _each(_dot_nt, ar, beta)
    x_k = _each(_dot_nt, ar, kt)
    a_ab = _each(lambda x: jnp.where(strict, x[:chunk], 0.0), x_b)
    a_rb = _each(lambda x: jnp.where(incl, x[chunk:], 0.0), x_b)
    a_ak = _each(lambda x: jnp.where(strict, x[:chunk], 0.0), x_k)
    a_rk = _each(lambda x: jnp.where(incl, x[chunk:], 0.0), x_k)
    t_inv = _unit_tri_inverse(a_ab, row, col, chunk)

    w_t = _each(_dot, t_inv, alpha)
    u0 = _each(_dot, t_inv, _each(_dot, a_ak, v))
    y0 = _each(_dot, a_rk, v)
    ktv = _each(_dot_tn, kt, v)
    u = _each(lambda x, y: x + y, _each(_dot, w_t, st0_b), u0)
    y1 = _each(_dot, rt, st0_b)
    y2 = _each(_dot, a_rb, u)
    btu = _each(_dot_tn, beta, u)
    p_col = [jnp.transpose(pl_ref[:, sl])[:, :1] for sl in sls]
    for h in range(heads):
        y_ref[0, :, sls[h]] = y0[h] + y1[h] - y2[h]
        st_ref[h] = (st0[h] + ktv[h] - btu[h]) * p_col[h]

    @pl.when(c == nc - 1)
    def _():
        sf_ref[0] = st_ref[...]


def wkv(r, lw, k, v, kk, a, s0t, reverse, hd):
    b, t, width = r.shape
    nh = width // hd
    heads = max(hh for hh in (1, 2, 4, WKV_HEADS) if nh % hh == 0 and hh <= WKV_HEADS)
    chunk = WKV_CHUNK
    nc = t // chunk
    blk = heads * hd

    def tok_map(bi, hi, ci):
        return (bi, nc - 1 - ci if reverse else ci, hi)

    tok_spec = pl.BlockSpec((1, chunk, blk), tok_map)
    st_spec = pl.BlockSpec((1, heads, hd, hd), lambda bi, hi, ci: (bi, hi, 0, 0))
    return pl.pallas_call(
        functools.partial(_wkv_kernel, chunk=chunk, hd=hd, heads=heads, reverse=reverse, nc=nc),
        grid=(b, nh // heads, nc),
        in_specs=[tok_spec] * 6 + [st_spec],
        out_specs=[tok_spec, st_spec],
        out_shape=[jax.ShapeDtypeStruct((b, t, width), F32),
                   jax.ShapeDtypeStruct((b, nh, hd, hd), F32)],
        scratch_shapes=[pltpu.VMEM((heads, hd, hd), F32)]
        + [pltpu.VMEM((chunk, blk), BF16)] * 5 + [pltpu.VMEM((8, blk), F32)],
        compiler_params=pltpu.CompilerParams(
            dimension_semantics=("parallel", "parallel", "arbitrary"),
            vmem_limit_bytes=VMEM_LIMIT),
        name="wkv_rev" if reverse else "wkv_fwd",
    )(r, lw, k, v, kk, a, s0t)


def rmsnorm(x, w):
    y = x * lax.rsqrt(jnp.mean(x * x, axis=-1, keepdims=True) + NORM_EPS)
    return y * w


def modulate(x, w, shift, scale):
    return rmsnorm(x, w) * (1 + scale) + shift


def seq_shift(t):
    half = t.shape[-1] // 2
    prev = jnp.pad(t[:, :-1, :half], ((0, 0), (1, 0), (0, 0)))
    nxt = jnp.pad(t[:, 1:, half:], ((0, 0), (0, 1), (0, 0)))
    return jnp.concatenate([prev, nxt], axis=-1)


def grid_shift(t):
    b, n, ch = t.shape
    q = ch // 4
    g = t.reshape(b, n // GRID_W, GRID_W, ch)
    from_left = jnp.pad(g[:, :, :-1, :q], ((0, 0), (0, 0), (1, 0), (0, 0)))
    from_right = jnp.pad(g[:, :, 1:, q:2 * q], ((0, 0), (0, 0), (0, 1), (0, 0)))
    from_up = jnp.pad(g[:, :-1, :, 2 * q:3 * q], ((0, 0), (1, 0), (0, 0), (0, 0)))
    from_down = jnp.pad(g[:, 1:, :, 3 * q:], ((0, 0), (0, 1), (0, 0), (0, 0)))
    return jnp.concatenate([from_left, from_right, from_up, from_down], axis=-1).reshape(b, n, ch)


def dft_tables(n, scale):
    j = lax.broadcasted_iota(jnp.int32, (n, n), 0)
    k = lax.broadcasted_iota(jnp.int32, (n, n), 1)
    ang = ((j * k) % n).astype(F32) * (2.0 * math.pi / n)
    return jnp.cos(ang) * scale, jnp.sin(ang) * scale


def fourier_mix(u, pos_table, chan_table):
    b, n, fw = u.shape
    gcs = mm(u, chan_table, BF16)
    gcs = gcs.reshape(b, n, 2, fw).transpose(0, 2, 1, 3).reshape(b, 2 * n, fw)
    out = [matmul(pos_table, gcs[i], BF16) for i in range(b)]
    return jnp.stack(out, axis=0)


def depthwise_conv(u, w):
    n = u.shape[1]
    kk = w.shape[0]
    pad = kk // 2
    up = jnp.pad(u, ((0, 0), (pad, pad), (0, 0)))
    out = up[:, 0:n] * w[0]
    for i in range(1, kk):
        out = out + up[:, i:i + n] * w[i]
    return out


def kernel(x, c, ctx, c_ctx, ada_a, ada_b, ada_bias, norm1_w, norm2_w, w_in, mu_rkv, mu_lr, decay_w0, decay_w1, decay_w2, iclr_a0, iclr_a1, iclr_a2, ogate_g1, ogate_g2, k_k, k_a, r_k, gn_w, gn_b, vres_mu, vres_v0, vres_v1, vres_v2, conv_w, gate_w1, gate_w2, gate_b, proj_rwkv, proj_fourier, proj_conv, w_out, ffn_w1, ffn_w3, ffn_w2, router_w, router_b, moe_w1, moe_w3, moe_w2, final_norm_w):
    depth = w_in.shape[0]
    bsz, seq, d = x.shape
    ctx_len = ctx.shape[1]
    rw = mu_rkv.shape[-1]
    nh, hd = r_k.shape[1], r_k.shape[2]
    fw = proj_fourier.shape[1]
    cw = proj_conv.shape[1]
    four_off = 3 * rw
    conv_off = four_off + fw
    n_exp = router_w.shape[-1]
    d_exp = moe_w1.shape[-1]
    n_mod = ada_b.shape[-1] // d
    r_dec, r_icl, r_og, r_vr, r_gate = (decay_w1.shape[-1], iclr_a1.shape[-1], ogate_g1.shape[-1],
                                        vres_v1.shape[-1], gate_w1.shape[-1])

    gc = fw // FOURIER_GROUPS
    cc, cs = dft_tables(gc, gc ** -0.5)
    eye_g = jnp.eye(FOURIER_GROUPS, dtype=F32)
    chan_table = jnp.concatenate([jnp.kron(eye_g, cc), jnp.kron(eye_g, cs)], axis=1).astype(BF16)

    def pos_table(n):
        pc, ps = dft_tables(n, n ** -0.5)
        return jnp.concatenate([pc, -ps], axis=1).astype(BF16)

    pos_lat, pos_ctx = pos_table(seq), pos_table(ctx_len)

    cond_lat = jax.nn.silu(c)
    cond_ctx = jax.nn.silu(c_ctx)[None, :]
    cond = jnp.concatenate([cond_lat, cond_ctx], axis=0)
    cond = jnp.pad(cond, ((0, 16 - cond.shape[0] % 16), (0, 0)))

    def tokens(xs, mod, l, shift, pos_tab, s0, v_first, last_ctx):
        b, t, _ = xs.shape
        has_vres = l > 0
        h = modulate(xs, norm1_w[l], mod[0], mod[1])
        dh = shift(h) - h
        hd_cat = jnp.concatenate([h, dh], axis=-1).astype(BF16).reshape(b * t, 2 * d)

        w_h = [decay_w1[l, 0], decay_w1[l, 1], iclr_a1[l, 0], iclr_a1[l, 1], ogate_g1[l], gate_w1[l]]
        mus = [mu_lr[l, 0], mu_lr[l, 0], mu_lr[l, 1], mu_lr[l, 1], mu_lr[l, 2], None]
        if has_vres:
            w_h.append(vres_v1[l - 1])
            mus.append(vres_mu[l - 1])
        w_dh = [jnp.zeros_like(w) if m is None else w * m[:, None] for w, m in zip(w_h, mus)]
        w_low = jnp.concatenate([jnp.concatenate(w_h, axis=1), jnp.concatenate(w_dh, axis=1)], axis=0)
        n_low = w_low.shape[1]
        w_low = jnp.pad(w_low, ((0, 0), (0, -n_low % LANE)))
        low = matmul(hd_cat, w_low.astype(BF16)).reshape(b, t, -1)
        offs = [0]
        for w in w_h:
            offs.append(offs[-1] + w.shape[1])
        low_w = [low[..., offs[0]:offs[1]], low[..., offs[1]:offs[2]]]
        low_a = [low[..., offs[2]:offs[3]], low[..., offs[3]:offs[4]]]
        low_g = low[..., offs[4]:offs[5]]
        low_gate = low[..., offs[5]:offs[6]]

        w_in_b = w_in[l].astype(BF16)
        p = matmul(hd_cat, w_in_b[:, :four_off]).reshape(b, t, -1)
        p_fc = matmul(hd_cat, w_in_b[:, four_off:], BF16).reshape(b, t, -1)

        def lerp(q, mu):
            return q + (shift(q) - q) * mu

        p_r, p_k, p_v = p[..., :rw], p[..., rw:2 * rw], p[..., 2 * rw:3 * rw]
        k = lerp(p_k, mu_rkv[l, 1])
        v = lerp(p_v, mu_rkv[l, 2])
        if has_vres:
            low_v = low[..., offs[6]:offs[7]]
            v = v + (v_first - v) * jax.nn.sigmoid(vres_v0[l - 1] + mm(low_v, vres_v2[l - 1]))
        kk = (k * k_k[l]).reshape(b, t, nh, hd)
        kk = kk * lax.rsqrt(jnp.maximum(jnp.sum(kk * kk, axis=-1, keepdims=True), 1e-24))
        kk = kk.reshape(b, t, rw)
        r = lerp(p_r, mu_rkv[l, 0])

        ys, states, ks = [], [], []
        for di, reverse in enumerate((False, True)):
            wl = decay_w0[l, di] + mm(jnp.tanh(low_w[di]), decay_w2[l, di])
            lw = -jnp.exp(-jax.nn.softplus(-wl) - DECAY_OFFSET)
            a = jax.nn.sigmoid(iclr_a0[l, di] + mm(low_a[di], iclr_a2[l, di]))
            kd = k * (1.0 + (a - 1.0) * k_a[l])
            init = jnp.zeros((b, nh, hd, hd), F32) if s0 is None else s0[di]
            y_d, s_d = wkv(r, lw, kd, v, kk, a, init, reverse, hd)
            ys.append(y_d)
            states.append(s_d)
            ks.append(kd)
        if last_ctx:
            return None, states, v

        y = (ys[0] + ys[1]).reshape(b, t, nh, hd)
        mean = jnp.mean(y, axis=-1, keepdims=True)
        var = jnp.mean(jnp.square(y - mean), axis=-1, keepdims=True)
        y = ((y - mean) * lax.rsqrt(var + GN_EPS)).reshape(b, t, rw)
        y = y * gn_w[l] + gn_b[l]
        rk = (r * (ks[0] + ks[1])).reshape(b, t, nh, hd) * r_k[l]
        bonus = jnp.sum(rk, axis=-1, keepdims=True) * v.reshape(b, t, nh, hd)
        y = y + bonus.reshape(b, t, rw)
        g = mm(jax.nn.sigmoid(low_g), ogate_g2[l])
        y_rwkv = y * g

        y_four = fourier_mix(p_fc[..., :fw], pos_tab, chan_table)
        gate_b_ = p_fc[..., fw:fw + cw].astype(F32)
        gate_c_ = p_fc[..., fw + cw:fw + 2 * cw].astype(F32)
        uu = p_fc[..., fw + 2 * cw:].astype(F32)
        y_conv = gate_b_ * depthwise_conv(gate_c_ * uu, conv_w[l])

        rows = b * t

        def flat(z):
            return z.reshape(rows, -1).astype(BF16)

        merged = gated_merge(flat(y_rwkv), flat(y_four), flat(y_conv), flat(low_gate),
                             proj_rwkv[l].astype(BF16), proj_fourier[l].astype(BF16),
                             proj_conv[l].astype(BF16), gate_w2[l].astype(BF16), gate_b[l][None, :])
        xs = resid_matmul(merged, w_out[l].astype(BF16), xs.reshape(rows, d),
                          jnp.broadcast_to(mod[2], (b, 1, d)), t)

        h2 = modulate(xs.reshape(b, t, d), norm2_w[l], mod[3], mod[4])
        j = l // 2
        if l % 2 == 0:
            hid = swiglu_up(flat(h2), ffn_w1[j][None].astype(BF16), ffn_w3[j][None].astype(BF16))
            w2 = ffn_w2[j]
        else:
            logits = jnp.dot(h2, router_w[j], precision=HI) + router_b[j]
            top_val, top_idx = lax.top_k(logits, TOP_K)
            weights = jax.nn.softmax(top_val, axis=-1)
            gate = jnp.sum(jax.nn.one_hot(top_idx, n_exp, dtype=F32) * weights[..., None], axis=-2)
            gate_rep = jnp.repeat(gate.reshape(rows, n_exp), LANE, axis=-1)
            hid = swiglu_up(flat(h2), moe_w1[j].astype(BF16), moe_w3[j].astype(BF16), gate_rep)
            w2 = moe_w2[j].reshape(n_exp * d_exp, d)
        xs = resid_matmul(hid, w2.astype(BF16), xs, jnp.broadcast_to(mod[5], (b, 1, d)), t)
        return xs.reshape(b, t, d), states, v

    xl, xc = x, ctx
    v_first_l = v_first_c = None
    for l in range(depth):
        last = l == depth - 1
        m = mm(mm(cond, ada_a[l]), ada_b[l]) + ada_bias[l]
        mod_l = [m[:bsz, None, i * d:(i + 1) * d] for i in range(n_mod)]
        mod_c = [m[bsz:bsz + 1, None, i * d:(i + 1) * d] for i in range(n_mod)]

        xc_new, ctx_states, vc = tokens(xc, mod_c, l, seq_shift, pos_ctx, None, v_first_c, last)
        xl, _, vl = tokens(xl, mod_l, l, grid_shift, pos_lat, ctx_states, v_first_l, False)
        if l == 0:
            v_first_c, v_first_l = vc, vl
        if not last:
            xc = xc_new
    return rmsnorm(xl, final_norm_w)
```

```python
import functools
import math

import jax
import jax.numpy as jnp
from jax import lax
from jax.experimental import pallas as pl
from jax.experimental.pallas import tpu as pltpu

F32 = jnp.float32
BF16 = jnp.bfloat16

GRID_W = 64
FOURIER_GROUPS = 8
TOP_K = 2
NORM_EPS = 1e-6
GN_EPS = 64e-5
DECAY_OFFSET = 0.5
WKV_CHUNK = 64
WKV_HEADS = 32
INV_PASSES = 1
PREP_ROWS = 256
POST_ROWS = 256
LANE = 128
VMEM_LIMIT = 56 * 1024 * 1024

HI = lax.Precision.HIGHEST


def _pick(dim, target, align):
    if dim <= target:
        return dim
    t = (target // align) * align
    while t >= align:
        if dim % t == 0:
            return t
        t -= align
    return dim


def _mm_kernel(a_ref, b_ref, o_ref, acc_ref, *, nk):
    k = pl.program_id(2)

    @pl.when(k == 0)
    def _():
        acc_ref[...] = jnp.zeros_like(acc_ref)

    acc_ref[...] += jnp.dot(a_ref[...], b_ref[...], preferred_element_type=F32)

    @pl.when(k == nk - 1)
    def _():
        o_ref[...] = acc_ref[...].astype(o_ref.dtype)


def matmul(a, b, out_dtype=F32, tm=1024, tn=1024, tk=2048, n_cols=None, col_off=0):
    m = a.shape[0]
    k = b.shape[0]
    n = b.shape[1] if n_cols is None else n_cols
    tm = _pick(m, tm, 16)
    tn = _pick(math.gcd(n, col_off) if col_off else n, tn, LANE)
    tk = _pick(k, tk, LANE)
    nk = k // tk
    joff = col_off // tn
    return pl.pallas_call(
        functools.partial(_mm_kernel, nk=nk),
        grid=(m // tm, n // tn, nk),
        in_specs=[pl.BlockSpec((tm, tk), lambda i, j, kk: (i, kk)),
                  pl.BlockSpec((tk, tn), lambda i, j, kk: (kk, j + joff))],
        out_specs=pl.BlockSpec((tm, tn), lambda i, j, kk: (i, j)),
        out_shape=jax.ShapeDtypeStruct((m, n), out_dtype),
        scratch_shapes=[pltpu.VMEM((tm, tn), F32)],
        compiler_params=pltpu.CompilerParams(
            dimension_semantics=("parallel", "parallel", "arbitrary"),
            vmem_limit_bytes=VMEM_LIMIT),
        name="matmul",
    )(a, b)


def mm(a, b, out_dtype=F32, **kw):
    lead = a.shape[:-1]
    out = matmul(a.reshape(-1, a.shape[-1]).astype(BF16), b.astype(BF16), out_dtype, **kw)
    return out.reshape(lead + (b.shape[-1],))


def _params(sem):
    return pltpu.CompilerParams(dimension_semantics=sem, vmem_limit_bytes=VMEM_LIMIT)


def _swiglu_up_kernel(a_ref, w1_ref, w3_ref, *rest, nk, gated):
    if gated:
        g_ref, o_ref, acc1_ref, acc3_ref = rest
    else:
        o_ref, acc1_ref, acc3_ref = rest
    k = pl.program_id(2)

    @pl.when(k == 0)
    def _():
        acc1_ref[...] = jnp.zeros_like(acc1_ref)
        acc3_ref[...] = jnp.zeros_like(acc3_ref)

    a = a_ref[...]
    acc1_ref[...] += jnp.dot(a, w1_ref[...], preferred_element_type=F32)
    acc3_ref[...] += jnp.dot(a, w3_ref[...], preferred_element_type=F32)

    @pl.when(k == nk - 1)
    def _():
        h1 = acc1_ref[...]
        hid = h1 * jax.nn.sigmoid(h1) * acc3_ref[...]
        if gated:
            hid = hid * jnp.tile(g_ref[...], (1, hid.shape[1] // LANE))
        o_ref[...] = hid.astype(o_ref.dtype)


def swiglu_up(a, w1, w3, gate_rep=None):
    m, k = a.shape
    n_e, _, f = w1.shape
    n = n_e * f
    tm = _pick(m, 1024, 16)
    tk = _pick(k, 2048, LANE)
    tn = _pick(f, 1024, LANE)
    gated = gate_rep is not None
    nk = k // tk
    per = f // tn
    w_spec = pl.BlockSpec((None, tk, tn), lambda i, j, kk: (j // per, kk, j % per))
    in_specs = [pl.BlockSpec((tm, tk), lambda i, j, kk: (i, kk)), w_spec, w_spec]
    args = [a, w1, w3]
    if gated:
        in_specs.append(pl.BlockSpec((tm, LANE), lambda i, j, kk: (i, j // per)))
        args.append(gate_rep)
    return pl.pallas_call(
        functools.partial(_swiglu_up_kernel, nk=nk, gated=gated),
        grid=(m // tm, n // tn, nk),
        in_specs=in_specs,
        out_specs=pl.BlockSpec((tm, tn), lambda i, j, kk: (i, j)),
        out_shape=jax.ShapeDtypeStruct((m, n), BF16),
        scratch_shapes=[pltpu.VMEM((tm, tn), F32), pltpu.VMEM((tm, tn), F32)],
        compiler_params=_params(("parallel", "parallel", "arbitrary")),
        name="swiglu_up",
    )(*args)


def _resid_kernel(a_ref, w_ref, x_ref, g_ref, o_ref, acc_ref, *, nk):
    k = pl.program_id(2)

    @pl.when(k == 0)
    def _():
        acc_ref[...] = jnp.zeros_like(acc_ref)

    acc_ref[...] += jnp.dot(a_ref[...], w_ref[...], preferred_element_type=F32)

    @pl.when(k == nk - 1)
    def _():
        o_ref[...] = x_ref[...] + g_ref[0] * acc_ref[...]


def resid_matmul(a, w, x, g, rows_per_batch):
    m, k = a.shape
    n = w.shape[1]
    tm = _pick(rows_per_batch, 1024, 16)
    tn = _pick(n, 1024, LANE)
    tk = _pick(k, 2048, LANE)
    nk = k // tk
    per = rows_per_batch // tm
    return pl.pallas_call(
        functools.partial(_resid_kernel, nk=nk),
        grid=(m // tm, n // tn, nk),
        in_specs=[pl.BlockSpec((tm, tk), lambda i, j, kk: (i, kk)),
                  pl.BlockSpec((tk, tn), lambda i, j, kk: (kk, j)),
                  pl.BlockSpec((tm, tn), lambda i, j, kk: (i, j)),
                  pl.BlockSpec((1, 1, tn), lambda i, j, kk: (i // per, 0, j))],
        out_specs=pl.BlockSpec((tm, tn), lambda i, j, kk: (i, j)),
        out_shape=jax.ShapeDtypeStruct((m, n), F32),
        scratch_shapes=[pltpu.VMEM((tm, tn), F32)],
        compiler_params=_params(("parallel", "parallel", "arbitrary")),
        name="resid_matmul",
    )(a, w, x, g)


def _merge_kernel(yr_ref, yf_ref, yc_ref, lg_ref, pr_ref, pf_ref, pc_ref,
                  gwr_ref, gwf_ref, gwc_ref, gbr_ref, gbf_ref, gbc_ref, o_ref):
    lg = lg_ref[...]

    def branch(y_ref, p_ref, gw_ref, gb_ref):
        gate = jax.nn.sigmoid(jnp.dot(lg, gw_ref[...], preferred_element_type=F32) + gb_ref[...])
        return gate * jnp.dot(y_ref[...], p_ref[...], preferred_element_type=F32)

    out = (branch(yr_ref, pr_ref, gwr_ref, gbr_ref) + branch(yf_ref, pf_ref, gwf_ref, gbf_ref)
           + branch(yc_ref, pc_ref, gwc_ref, gbc_ref))
    o_ref[...] = out.astype(o_ref.dtype)


def gated_merge(yr, yf, yc, lg, pr, pf, pc, gw2, gb):
    m = yr.shape[0]
    d = pr.shape[1]
    tm = _pick(m, 1024, 16)
    tn = _pick(d, 512, LANE)
    nj = d // tn

    def rows(arr):
        return pl.BlockSpec((tm, arr.shape[1]), lambda i, j: (i, 0))

    def cols(arr, off):
        return pl.BlockSpec((arr.shape[0], tn), lambda i, j: (0, off * nj + j))

    return pl.pallas_call(
        _merge_kernel,
        grid=(m // tm, nj),
        in_specs=[rows(yr), rows(yf), rows(yc), rows(lg), cols(pr, 0), cols(pf, 0), cols(pc, 0),
                  cols(gw2, 0), cols(gw2, 1), cols(gw2, 2), cols(gb, 0), cols(gb, 1), cols(gb, 2)],
        out_specs=pl.BlockSpec((tm, tn), lambda i, j: (i, j)),
        out_shape=jax.ShapeDtypeStruct((m, d), BF16),
        compiler_params=_params(("parallel", "parallel")),
        name="gated_merge",
    )(yr, yf, yc, lg, pr, pf, pc, gw2, gw2, gw2, gb, gb, gb)


def _bf(x):
    return x.astype(BF16)


def _dot(a, b, passes=1):
    if passes == 1:
        return jnp.dot(_bf(a), _bf(b), preferred_element_type=F32)
    a_hi, b_hi = _bf(a), _bf(b)
    a_lo, b_lo = _bf(a - a_hi.astype(F32)), _bf(b - b_hi.astype(F32))
    return (jnp.dot(a_hi, b_hi, preferred_element_type=F32)
            + jnp.dot(a_hi, b_lo, preferred_element_type=F32)
            + jnp.dot(a_lo, b_hi, preferred_element_type=F32))


def _dot_nt(a, b):
    return lax.dot_general(_bf(a), _bf(b), (((1,), (1,)), ((), ())), preferred_element_type=F32)


def _dot_tn(a, b):
    return lax.dot_general(_bf(a), _bf(b), (((0,), (0,)), ((), ())), preferred_element_type=F32)


def _each(f, *lists):
    return [f(*xs) for xs in zip(*lists)]


def _unit_tri_inverse(a_tri, row, col, size):
    def same_block(shift):
        return (row >> shift) == (col >> shift)

    def idot(a, b):
        return _dot(a, b, INV_PASSES)

    eye = (row == col).astype(F32)
    a8 = _each(lambda a: jnp.where(same_block(3), a, 0.0), a_tri)
    a8_2 = _each(idot, a8, a8)
    a8_4 = _each(idot, a8_2, a8_2)
    x = _each(lambda a: eye - a, a8)
    x = _each(lambda xx, d: xx + d, x, _each(idot, x, a8_2))
    x = _each(lambda xx, d: xx + d, x, _each(idot, x, a8_4))
    shift = 3
    while (1 << shift) < size:
        off = same_block(shift + 1) & jnp.logical_not(same_block(shift))
        e = _each(lambda a: jnp.where(off, a, 0.0), a_tri)
        ex = _each(idot, e, x)
        x = _each(lambda xx, d: xx - d, x, _each(idot, x, ex))
        shift += 1
    return x


def _wkv_kernel(r_ref, lw_ref, k_ref, v_ref, kk_ref, a_ref, s0_ref, y_ref, sf_ref,
                st_ref, al_ref, be_ref, rt_ref, kt_ref, vb_ref, pl_ref,
                *, chunk, hd, heads, reverse, nc):
    c = pl.program_id(2)

    @pl.when(c == 0)
    def _():
        st_ref[...] = s0_ref[0]

    row = lax.broadcasted_iota(jnp.int32, (chunk, chunk), 0)
    col = lax.broadcasted_iota(jnp.int32, (chunk, chunk), 1)
    if reverse:
        strict, incl = col > row, col >= row
    else:
        strict, incl = col < row, col <= row
    tri = jnp.where(incl, 1.0, 0.0).astype(BF16)
    last = 0 if reverse else chunk - 1

    lw = lw_ref[0]
    lw_hi = _bf(lw)
    rem = lw - lw_hi.astype(F32)
    lw_mid = _bf(rem)
    lw_lo = _bf(rem - lw_mid.astype(F32))
    cum = (jnp.dot(tri, lw_hi, preferred_element_type=F32)
           + jnp.dot(tri, lw_mid, preferred_element_type=F32)
           + jnp.dot(tri, lw_lo, preferred_element_type=F32))
    p_in = jnp.exp(cum)
    p_inv = jnp.exp(-cum)
    kk = kk_ref[0].astype(F32)
    rt_ref[...] = _bf(r_ref[0].astype(F32) * p_in)
    kt_ref[...] = _bf(k_ref[0].astype(F32) * p_inv)
    be_ref[...] = _bf(a_ref[0].astype(F32) * kk * p_inv)
    al_ref[...] = _bf(kk * jnp.exp(cum - lw))
    vb_ref[...] = _bf(v_ref[0])
    pl_ref[...] = jnp.broadcast_to(p_in[last:last + 1, :], pl_ref.shape)

    sls = [slice(h * hd, (h + 1) * hd) for h in range(heads)]
    alpha = [al_ref[:, sl] for sl in sls]
    beta = [be_ref[:, sl] for sl in sls]
    rt = [rt_ref[:, sl] for sl in sls]
    kt = [kt_ref[:, sl] for sl in sls]
    v = [vb_ref[:, sl] for sl in sls]
    st0 = [st_ref[h] for h in range(heads)]
    st0_b = _each(_bf, st0)

    ar = _each(lambda x, y: jnp.concatenate([x, y], axis=0), alpha, rt)
    x_b = _each(_dot_nt, ar, beta)
    x_k = _each(_dot_nt, ar, kt)
    a_ab = _each(lambda x: jnp.where(strict, x[:chunk], 0.0), x_b)
    a_rb = _each(lambda x: jnp.where(incl, x[chunk:], 0.0), x_b)
    a_ak = _each(lambda x: jnp.where(strict, x[:chunk], 0.0), x_k)
    a_rk = _each(lambda x: jnp.where(incl, x[chunk:], 0.0), x_k)
    t_inv = _unit_tri_inverse(a_ab, row, col, chunk)

    w_t = _each(_dot, t_inv, alpha)
    u0 = _each(_dot, t_inv, _each(_dot, a_ak, v))
    y0 = _each(_dot, a_rk, v)
    ktv = _each(_dot_tn, kt, v)
    u = _each(lambda x, y: x + y, _each(_dot, w_t, st0_b), u0)
    y1 = _each(_dot, rt, st0_b)
    y2 = _each(_dot, a_rb, u)
    btu = _each(_dot_tn, beta, u)
    p_col = [jnp.transpose(pl_ref[:, sl])[:, :1] for sl in sls]
    for h in range(heads):
        y_ref[0, :, sls[h]] = y0[h] + y1[h] - y2[h]
        st_ref[h] = (st0[h] + ktv[h] - btu[h]) * p_col[h]

    @pl.when(c == nc - 1)
    def _():
        sf_ref[0] = st_ref[...]


def wkv(r, lw, k, v, kk, a, s0t, reverse, hd):
    b, t, width = r.shape
    nh = width // hd
    heads = max(hh for hh in (1, 2, 4, 8, 16, 32) if nh % hh == 0 and hh <= WKV_HEADS)
    chunk = WKV_CHUNK
    nc = t // chunk
    blk = heads * hd

    def tok_map(bi, hi, ci):
        return (bi, nc - 1 - ci if reverse else ci, hi)

    tok_spec = pl.BlockSpec((1, chunk, blk), tok_map)
    st_spec = pl.BlockSpec((1, heads, hd, hd), lambda bi, hi, ci: (bi, hi, 0, 0))
    return pl.pallas_call(
        functools.partial(_wkv_kernel, chunk=chunk, hd=hd, heads=heads, reverse=reverse, nc=nc),
        grid=(b, nh // heads, nc),
        in_specs=[tok_spec] * 6 + [st_spec],
        out_specs=[tok_spec, st_spec],
        out_shape=[jax.ShapeDtypeStruct((b, t, width), F32),
                   jax.ShapeDtypeStruct((b, nh, hd, hd), F32)],
        scratch_shapes=[pltpu.VMEM((heads, hd, hd), F32)]
        + [pltpu.VMEM((chunk, blk), BF16)] * 5 + [pltpu.VMEM((8, blk), F32)],
        compiler_params=pltpu.CompilerParams(
            dimension_semantics=("parallel", "parallel", "arbitrary"),
            vmem_limit_bytes=VMEM_LIMIT),
        name="wkv_rev" if reverse else "wkv_fwd",
    )(r, lw, k, v, kk, a, s0t)


def _from_prev(x):
    return pltpu.roll(x, 1, axis=0)


def _from_next(x):
    return pltpu.roll(x, x.shape[0] - 1, axis=0)


def _token_shift(x, up, dn, has_up, has_dn, mode):
    rows, ch = x.shape
    if mode == "seq":
        half = ch // 2
        t = lax.broadcasted_iota(jnp.int32, (rows, half), 0)
        s0 = jnp.where(t == 0, 0.0, _from_prev(x[:, :half]))
        s1 = jnp.where(t == rows - 1, 0.0, _from_next(x[:, half:]))
        return [s0, s1]
    q = ch // 4
    t = lax.broadcasted_iota(jnp.int32, (rows, q), 0) % GRID_W
    s0 = jnp.where(t == 0, 0.0, _from_prev(x[:, :q]))
    s1 = jnp.where(t == GRID_W - 1, 0.0, _from_next(x[:, q:2 * q]))
    up = jnp.where(has_up, up, 0.0)
    dn = jnp.where(has_dn, dn, 0.0)
    if rows == GRID_W:
        s2, s3 = up, dn
    else:
        s2 = jnp.concatenate([up, x[:rows - GRID_W, 2 * q:3 * q]], axis=0)
        s3 = jnp.concatenate([x[GRID_W:, 3 * q:], dn], axis=0)
    return [s0, s1, s2, s3]


def _prep_kernel(*refs, mode, nblk):
    if mode == "grid":
        x_ref, up_ref, dn_ref, w_ref, sh_ref, sc_ref, o_ref = refs
    else:
        x_ref, w_ref, sh_ref, sc_ref, o_ref = refs
    i = pl.program_id(1)
    w, sh, sc = w_ref[...], sh_ref[0], sc_ref[0]

    def modulated(x):
        y = x * lax.rsqrt(jnp.mean(x * x, axis=-1, keepdims=True) + NORM_EPS)
        return (y * w) * (1 + sc) + sh

    h = modulated(x_ref[0])
    d = h.shape[1]
    if mode == "grid":
        q = d // 4
        up = modulated(up_ref[0])[:, 2 * q:3 * q]
        dn = modulated(dn_ref[0])[:, 3 * q:]
        parts = _token_shift(h, up, dn, i > 0, i < nblk - 1, mode)
    else:
        parts = _token_shift(h, None, None, None, None, mode)
    o_ref[:, :d] = h.astype(o_ref.dtype)
    width = d // len(parts)
    for n, s in enumerate(parts):
        lo = n * width
        o_ref[:, d + lo:d + lo + width] = (s - h[:, lo:lo + width]).astype(o_ref.dtype)


def prep(x, w, shift, scale, mode):
    b, t, d = x.shape
    rows = _pick(t, PREP_ROWS, GRID_W) if mode == "grid" else t
    nblk = t // rows
    per = rows // GRID_W
    x_spec = pl.BlockSpec((1, rows, d), lambda bi, i: (bi, i, 0))
    vec_spec = pl.BlockSpec((1, d), lambda bi, i: (0, 0))
    mod_spec = pl.BlockSpec((1, 1, d), lambda bi, i: (bi, 0, 0))
    in_specs, args = [x_spec], [x]
    if mode == "grid":
        last = t // GRID_W - 1
        in_specs += [pl.BlockSpec((1, GRID_W, d), lambda bi, i: (bi, jnp.maximum(i * per - 1, 0), 0)),
                     pl.BlockSpec((1, GRID_W, d), lambda bi, i: (bi, jnp.minimum((i + 1) * per, last), 0))]
        args += [x, x]
    in_specs += [vec_spec, mod_spec, mod_spec]
    args += [w[None, :], shift, scale]
    return pl.pallas_call(
        functools.partial(_prep_kernel, mode=mode, nblk=nblk),
        grid=(b, nblk),
        in_specs=in_specs,
        out_specs=pl.BlockSpec((rows, 2 * d), lambda bi, i: (bi * nblk + i, 0)),
        out_shape=jax.ShapeDtypeStruct((b * t, 2 * d), BF16),
        compiler_params=_params(("parallel", "parallel")),
        name="prep_" + mode,
    )(*args)


def _head_sums(x, hd):
    rows, width = x.shape
    nt = width // LANE
    stacked = jnp.concatenate([x[:, i * LANE:(i + 1) * LANE] for i in range(nt)], axis=0)
    li = lax.broadcasted_iota(jnp.int32, (LANE, LANE), 0) // hd
    lj = lax.broadcasted_iota(jnp.int32, (LANE, LANE), 1) // hd
    ones = jnp.where(li == lj, 1.0, 0.0).astype(BF16)
    hi = _bf(stacked)
    rem = stacked - hi.astype(F32)
    mid = _bf(rem)
    lo = _bf(rem - mid.astype(F32))
    s = (jnp.dot(hi, ones, preferred_element_type=F32) + jnp.dot(mid, ones, preferred_element_type=F32)
         + jnp.dot(lo, ones, preferred_element_type=F32))
    return jnp.concatenate([s[i * rows:(i + 1) * rows] for i in range(nt)], axis=1)


def _wkv_prep_kernel(*refs, mode, nblk, hd, offs, has_vres):
    refs = list(refs)
    main = [refs.pop(0) for _ in range(3)]
    halo = [(refs.pop(0), refs.pop(0)) for _ in range(3)] if mode == "grid" else [(None, None)] * 3
    low_ref = refs.pop(0)
    vf_ref = refs.pop(0) if has_vres else None
    mu_ref, w0_ref, a0_ref, kk_ref, ka_ref, w2_ref, a2_ref = [refs.pop(0) for _ in range(7)]
    if has_vres:
        v0_ref, v2_ref = refs.pop(0), refs.pop(0)
    r_o, v_o, kk_o, kdf_o, kdr_o, af_o, ar_o, lwf_o, lwr_o = refs
    c = pl.program_id(1)
    low = low_ref[0]

    def lerp(n):
        x = main[n][0]
        up, dn = (halo[n][0][0], halo[n][1][0]) if mode == "grid" else (None, None)
        sh = jnp.concatenate(_token_shift(x, up, dn, c > 0, c < nblk - 1, mode), axis=1)
        return x + (sh - x) * mu_ref[n:n + 1, :]

    def low_dot(lo, hi_, w, act=None):
        z = low[:, lo:hi_]
        if act is not None:
            z = act(z)
        return jnp.dot(_bf(z), w, preferred_element_type=F32)

    r = lerp(0)
    k = lerp(1)
    v = lerp(2)
    if has_vres:
        gate = jax.nn.sigmoid(v0_ref[...] + low_dot(offs[6], offs[7], v2_ref[...]))
        v = v + (vf_ref[0].astype(F32) - v) * gate
    kq = k * kk_ref[...]
    kk = kq * lax.rsqrt(jnp.maximum(_head_sums(kq * kq, hd), 1e-24))
    r_o[0] = r.astype(r_o.dtype)
    v_o[0] = v.astype(v_o.dtype)
    kk_o[0] = kk.astype(kk_o.dtype)
    for di, (kd_o, a_o, lw_o) in enumerate(((kdf_o, af_o, lwf_o), (kdr_o, ar_o, lwr_o))):
        wl = w0_ref[di:di + 1, :] + low_dot(offs[di], offs[di + 1], w2_ref[di], jnp.tanh)
        softplus = jnp.maximum(-wl, 0.0) + jnp.log1p(jnp.exp(-jnp.abs(wl)))
        lw_o[0] = -jnp.exp(-softplus - DECAY_OFFSET)
        a = jax.nn.sigmoid(a0_ref[di:di + 1, :] + low_dot(offs[2 + di], offs[3 + di], a2_ref[di]))
        a_o[0] = a.astype(a_o.dtype)
        kd_o[0] = (k * (1.0 + (a - 1.0) * ka_ref[...])).astype(kd_o.dtype)


def wkv_prep(p, low, v_first, mu, w0, a0, k_k, k_a, w2, a2, v0, v2, offs, hd, mode):
    b, t, rw3 = p.shape
    rw = rw3 // 3
    q = rw // 4
    rows = GRID_W if mode == "grid" else t
    nblk = t // rows
    has_vres = v_first is not None

    def tok(width, col):
        return pl.BlockSpec((1, rows, width), lambda bi, ci: (bi, ci, col))

    def full(arr):
        nd = arr.ndim
        return pl.BlockSpec(arr.shape, lambda bi, ci: (0,) * nd)

    in_specs = [tok(rw, n) for n in range(3)]
    args = [p, p, p]
    if mode == "grid":
        for n in range(3):
            in_specs += [pl.BlockSpec((1, rows, q), lambda bi, ci, n=n: (bi, jnp.maximum(ci - 1, 0), 4 * n + 2)),
                         pl.BlockSpec((1, rows, q), lambda bi, ci, n=n: (bi, jnp.minimum(ci + 1, nblk - 1), 4 * n + 3))]
            args += [p, p]
    in_specs.append(tok(low.shape[-1], 0))
    args.append(low)
    if has_vres:
        in_specs.append(tok(rw, 0))
        args.append(v_first)
    small = [mu, w0, a0, k_k[None, :], k_a[None, :], w2.astype(BF16), a2.astype(BF16)]
    if has_vres:
        small += [v0[None, :], v2.astype(BF16)]
    in_specs += [full(s) for s in small]
    args += small
    out_spec = tok(rw, 0)
    shapes = [jax.ShapeDtypeStruct((b, t, rw), BF16)] * 7 + [jax.ShapeDtypeStruct((b, t, rw), F32)] * 2
    return pl.pallas_call(
        functools.partial(_wkv_prep_kernel, mode=mode, nblk=nblk, hd=hd, offs=tuple(offs), has_vres=has_vres),
        grid=(b, nblk),
        in_specs=in_specs,
        out_specs=[out_spec] * 9,
        out_shape=shapes,
        compiler_params=_params(("parallel", "parallel")),
        name="wkv_prep_" + mode,
    )(*args)


def _wkv_post_kernel(yf_ref, yr_ref, r_ref, kdf_ref, kdr_ref, v_ref, lg_ref, rk_ref, gw_ref, gb_ref,
                     g2_ref, o_ref, *, hd):
    y = yf_ref[0] + yr_ref[0]
    mean = _head_sums(y, hd) * (1.0 / hd)
    yc = y - mean
    var = _head_sums(yc * yc, hd) * (1.0 / hd)
    y = yc * lax.rsqrt(var + GN_EPS) * gw_ref[...] + gb_ref[...]
    rk = r_ref[0].astype(F32) * (kdf_ref[0].astype(F32) + kdr_ref[0].astype(F32)) * rk_ref[...]
    y = y + _head_sums(rk, hd) * v_ref[0].astype(F32)
    g = jnp.dot(_bf(jax.nn.sigmoid(lg_ref[0])), g2_ref[...], preferred_element_type=F32)
    o_ref[0] = (y * g).astype(o_ref.dtype)


def wkv_post(y_f, y_r, r, kd_f, kd_r, v, low_g, r_k, gn_w, gn_b, g2, hd):
    b, t, rw = y_f.shape
    rows = _pick(t, POST_ROWS, 8)

    def tok(width):
        return pl.BlockSpec((1, rows, width), lambda bi, ci: (bi, ci, 0))

    def full(arr):
        return pl.BlockSpec(arr.shape, lambda bi, ci: (0, 0))

    small = [r_k.reshape(1, rw), gn_w[None, :], gn_b[None, :], g2.astype(BF16)]
    return pl.pallas_call(
        functools.partial(_wkv_post_kernel, hd=hd),
        grid=(b, t // rows),
        in_specs=[tok(rw)] * 6 + [tok(low_g.shape[-1])] + [full(s) for s in small],
        out_specs=tok(rw),
        out_shape=jax.ShapeDtypeStruct((b, t, rw), BF16),
        compiler_params=_params(("parallel", "parallel")),
        name="wkv_post",
    )(y_f, y_r, r, kd_f, kd_r, v, low_g, *small)


def rmsnorm(x, w):
    y = x * lax.rsqrt(jnp.mean(x * x, axis=-1, keepdims=True) + NORM_EPS)
    return y * w


def modulate(x, w, shift, scale):
    return rmsnorm(x, w) * (1 + scale) + shift


def dft_tables(n, scale):
    j = lax.broadcasted_iota(jnp.int32, (n, n), 0)
    k = lax.broadcasted_iota(jnp.int32, (n, n), 1)
    ang = ((j * k) % n).astype(F32) * (2.0 * math.pi / n)
    return jnp.cos(ang) * scale, jnp.sin(ang) * scale


def fourier_mix(u, pos_table, chan_table):
    b, n, fw = u.shape
    gcs = mm(u, chan_table, BF16)
    gcs = gcs.reshape(b, n, 2, fw).transpose(0, 2, 1, 3).reshape(b, 2 * n, fw)
    out = [matmul(pos_table, gcs[i], BF16) for i in range(b)]
    return jnp.stack(out, axis=0)


def depthwise_conv(u, w):
    n = u.shape[1]
    kk = w.shape[0]
    pad = kk // 2
    up = jnp.pad(u, ((0, 0), (pad, pad), (0, 0)))
    out = up[:, 0:n] * w[0]
    for i in range(1, kk):
        out = out + up[:, i:i + n] * w[i]
    return out


def kernel(x, c, ctx, c_ctx, ada_a, ada_b, ada_bias, norm1_w, norm2_w, w_in, mu_rkv, mu_lr, decay_w0, decay_w1, decay_w2, iclr_a0, iclr_a1, iclr_a2, ogate_g1, ogate_g2, k_k, k_a, r_k, gn_w, gn_b, vres_mu, vres_v0, vres_v1, vres_v2, conv_w, gate_w1, gate_w2, gate_b, proj_rwkv, proj_fourier, proj_conv, w_out, ffn_w1, ffn_w3, ffn_w2, router_w, router_b, moe_w1, moe_w3, moe_w2, final_norm_w):
    depth = w_in.shape[0]
    bsz, seq, d = x.shape
    ctx_len = ctx.shape[1]
    rw = mu_rkv.shape[-1]
    nh, hd = r_k.shape[1], r_k.shape[2]
    fw = proj_fourier.shape[1]
    cw = proj_conv.shape[1]
    four_off = 3 * rw
    conv_off = four_off + fw
    n_exp = router_w.shape[-1]
    d_exp = moe_w1.shape[-1]
    n_mod = ada_b.shape[-1] // d
    r_dec, r_icl, r_og, r_vr, r_gate = (decay_w1.shape[-1], iclr_a1.shape[-1], ogate_g1.shape[-1],
                                        vres_v1.shape[-1], gate_w1.shape[-1])

    gc = fw // FOURIER_GROUPS
    cc, cs = dft_tables(gc, gc ** -0.5)
    eye_g = jnp.eye(FOURIER_GROUPS, dtype=F32)
    chan_table = jnp.concatenate([jnp.kron(eye_g, cc), jnp.kron(eye_g, cs)], axis=1).astype(BF16)

    def pos_table(n):
        pc, ps = dft_tables(n, n ** -0.5)
        return jnp.concatenate([pc, -ps], axis=1).astype(BF16)

    pos_lat, pos_ctx = pos_table(seq), pos_table(ctx_len)

    cond_lat = jax.nn.silu(c)
    cond_ctx = jax.nn.silu(c_ctx)[None, :]
    cond = jnp.concatenate([cond_lat, cond_ctx], axis=0)
    cond = jnp.pad(cond, ((0, 16 - cond.shape[0] % 16), (0, 0)))

    def tokens(xs, mod, l, mode, pos_tab, s0, v_first, last_ctx):
        b, t, _ = xs.shape
        has_vres = l > 0
        hd_cat = prep(xs, norm1_w[l], jnp.broadcast_to(mod[0], (b, 1, d)),
                      jnp.broadcast_to(mod[1], (b, 1, d)), mode)

        w_h = [decay_w1[l, 0], decay_w1[l, 1], iclr_a1[l, 0], iclr_a1[l, 1], ogate_g1[l], gate_w1[l]]
        mus = [mu_lr[l, 0], mu_lr[l, 0], mu_lr[l, 1], mu_lr[l, 1], mu_lr[l, 2], None]
        if has_vres:
            w_h.append(vres_v1[l - 1])
            mus.append(vres_mu[l - 1])
        w_dh = [jnp.zeros_like(w) if m is None else w * m[:, None] for w, m in zip(w_h, mus)]
        w_low = jnp.concatenate([jnp.concatenate(w_h, axis=1), jnp.concatenate(w_dh, axis=1)], axis=0)
        n_low = w_low.shape[1]
        w_low = jnp.pad(w_low, ((0, 0), (0, -n_low % LANE)))
        low = matmul(hd_cat, w_low.astype(BF16)).reshape(b, t, -1)
        offs = [0]
        for w in w_h:
            offs.append(offs[-1] + w.shape[1])
        low_g = low[..., offs[4]:offs[5]]
        low_gate = low[..., offs[5]:offs[6]]

        w_in_b = w_in[l].astype(BF16)
        p = matmul(hd_cat, w_in_b, n_cols=four_off).reshape(b, t, -1)
        p_fc = matmul(hd_cat, w_in_b, BF16, n_cols=w_in_b.shape[1] - four_off,
                      col_off=four_off).reshape(b, t, -1)

        r, v, kk, kd_f, kd_r, a_f, a_r, lw_f, lw_r = wkv_prep(
            p, low, v_first, mu_rkv[l], decay_w0[l], iclr_a0[l], k_k[l], k_a[l], decay_w2[l], iclr_a2[l],
            vres_v0[l - 1] if has_vres else None, vres_v2[l - 1] if has_vres else None, offs, hd, mode)

        ys, states = [], []
        for di, (reverse, lw, kd, a) in enumerate(((False, lw_f, kd_f, a_f), (True, lw_r, kd_r, a_r))):
            init = jnp.zeros((b, nh, hd, hd), F32) if s0 is None else s0[di]
            y_d, s_d = wkv(r, lw, kd, v, kk, a, init, reverse, hd)
            ys.append(y_d)
            states.append(s_d)
        if last_ctx:
            return None, states, v

        y_rwkv = wkv_post(ys[0], ys[1], r, kd_f, kd_r, v, low_g, r_k[l], gn_w[l], gn_b[l],
                          ogate_g2[l], hd)

        y_four = fourier_mix(p_fc[..., :fw], pos_tab, chan_table)
        gate_b_ = p_fc[..., fw:fw + cw].astype(F32)
        gate_c_ = p_fc[..., fw + cw:fw + 2 * cw].astype(F32)
        uu = p_fc[..., fw + 2 * cw:].astype(F32)
        y_conv = gate_b_ * depthwise_conv(gate_c_ * uu, conv_w[l])

        rows = b * t

        def flat(z):
            return z.reshape(rows, -1).astype(BF16)

        merged = gated_merge(flat(y_rwkv), flat(y_four), flat(y_conv), flat(low_gate),
                             proj_rwkv[l].astype(BF16), proj_fourier[l].astype(BF16),
                             proj_conv[l].astype(BF16), gate_w2[l].astype(BF16), gate_b[l][None, :])
        xs = resid_matmul(merged, w_out[l].astype(BF16), xs.reshape(rows, d),
                          jnp.broadcast_to(mod[2], (b, 1, d)), t)

        h2 = modulate(xs.reshape(b, t, d), norm2_w[l], mod[3], mod[4])
        j = l // 2
        if l % 2 == 0:
            hid = swiglu_up(flat(h2), ffn_w1[j][None].astype(BF16), ffn_w3[j][None].astype(BF16))
            w2 = ffn_w2[j]
        else:
            logits = jnp.dot(h2, router_w[j], precision=HI) + router_b[j]
            top_val, top_idx = lax.top_k(logits, TOP_K)
            weights = jax.nn.softmax(top_val, axis=-1)
            gate = jnp.sum(jax.nn.one_hot(top_idx, n_exp, dtype=F32) * weights[..., None], axis=-2)
            gate_rep = jnp.repeat(gate.reshape(rows, n_exp), LANE, axis=-1)
            hid = swiglu_up(flat(h2), moe_w1[j].astype(BF16), moe_w3[j].astype(BF16), gate_rep)
            w2 = moe_w2[j].reshape(n_exp * d_exp, d)
        xs = resid_matmul(hid, w2.astype(BF16), xs, jnp.broadcast_to(mod[5], (b, 1, d)), t)
        return xs.reshape(b, t, d), states, v

    xl, xc = x, ctx
    v_first_l = v_first_c = None
    for l in range(depth):
        last = l == depth - 1
        m = mm(mm(cond, ada_a[l]), ada_b[l]) + ada_bias[l]
        mod_l = [m[:bsz, None, i * d:(i + 1) * d] for i in range(n_mod)]
        mod_c = [m[bsz:bsz + 1, None, i * d:(i + 1) * d] for i in range(n_mod)]

        xc_new, ctx_states, vc = tokens(xc, mod_c, l, "seq", pos_ctx, None, v_first_c, last)
        xl, _, vl = tokens(xl, mod_l, l, "grid", pos_lat, ctx_states, v_first_l, False)
        if l == 0:
            v_first_c, v_first_l = vc, vl
        if not last:
            xc = xc_new
    return rmsnorm(xl, final_norm_w)
```

```python
import functools
import math

import jax
import jax.numpy as jnp
from jax import lax
from jax.experimental import pallas as pl
from jax.experimental.pallas import tpu as pltpu

F32 = jnp.float32
BF16 = jnp.bfloat16

GRID_W = 64
FOURIER_GROUPS = 8
TOP_K = 2
NORM_EPS = 1e-6
GN_EPS = 64e-5
DECAY_OFFSET = 0.5
WKV_CHUNK = 64
WKV_HEADS = 32
INV_PASSES = 1
DFT_N1 = 128
PREP_ROWS = 256
POST_ROWS = 256
LANE = 128
VMEM_LIMIT = 56 * 1024 * 1024

HI = lax.Precision.HIGHEST


def _pick(dim, target, align):
    if dim <= target:
        return dim
    t = (target // align) * align
    while t >= align:
        if dim % t == 0:
            return t
        t -= align
    return dim


def _mm_kernel(a_ref, b_ref, o_ref, acc_ref, *, nk):
    k = pl.program_id(2)

    @pl.when(k == 0)
    def _():
        acc_ref[...] = jnp.zeros_like(acc_ref)

    acc_ref[...] += jnp.dot(a_ref[...], b_ref[...], preferred_element_type=F32)

    @pl.when(k == nk - 1)
    def _():
        o_ref[...] = acc_ref[...].astype(o_ref.dtype)


def matmul(a, b, out_dtype=F32, tm=1024, tn=1024, tk=2048, n_cols=None, col_off=0, planes=1):
    m = a.shape[0]
    k = b.shape[0]
    n = b.shape[1] if n_cols is None else n_cols
    tm = _pick(m, tm, 16)
    tn = _pick(math.gcd(n // planes, col_off) if col_off else n // planes, tn, LANE)
    tk = _pick(k, tk, LANE)
    nk = k // tk
    joff = col_off // tn
    if planes == 1:
        out_spec = pl.BlockSpec((tm, tn), lambda i, j, kk: (i, j))
        out_shape = jax.ShapeDtypeStruct((m, n), out_dtype)
    else:
        per = n // planes // tn
        out_spec = pl.BlockSpec((None, tm, tn), lambda i, j, kk: (j // per, i, j % per))
        out_shape = jax.ShapeDtypeStruct((planes, m, n // planes), out_dtype)
    return pl.pallas_call(
        functools.partial(_mm_kernel, nk=nk),
        grid=(m // tm, n // tn, nk),
        in_specs=[pl.BlockSpec((tm, tk), lambda i, j, kk: (i, kk)),
                  pl.BlockSpec((tk, tn), lambda i, j, kk: (kk, j + joff))],
        out_specs=out_spec,
        out_shape=out_shape,
        scratch_shapes=[pltpu.VMEM((tm, tn), F32)],
        compiler_params=pltpu.CompilerParams(
            dimension_semantics=("parallel", "parallel", "arbitrary"),
            vmem_limit_bytes=VMEM_LIMIT),
        name="matmul",
    )(a, b)


def mm(a, b, out_dtype=F32, **kw):
    lead = a.shape[:-1]
    out = matmul(a.reshape(-1, a.shape[-1]).astype(BF16), b.astype(BF16), out_dtype, **kw)
    return out.reshape(lead + (b.shape[-1],))


def _params(sem):
    return pltpu.CompilerParams(dimension_semantics=sem, vmem_limit_bytes=VMEM_LIMIT)


def _swiglu_up_kernel(a_ref, w1_ref, w3_ref, *rest, nk, gated):
    if gated:
        g_ref, o_ref, acc1_ref, acc3_ref = rest
    else:
        o_ref, acc1_ref, acc3_ref = rest
    k = pl.program_id(2)

    @pl.when(k == 0)
    def _():
        acc1_ref[...] = jnp.zeros_like(acc1_ref)
        acc3_ref[...] = jnp.zeros_like(acc3_ref)

    a = a_ref[...]
    acc1_ref[...] += jnp.dot(a, w1_ref[...], preferred_element_type=F32)
    acc3_ref[...] += jnp.dot(a, w3_ref[...], preferred_element_type=F32)

    @pl.when(k == nk - 1)
    def _():
        h1 = acc1_ref[...]
        hid = h1 * jax.nn.sigmoid(h1) * acc3_ref[...]
        if gated:
            hid = hid * jnp.tile(g_ref[...], (1, hid.shape[1] // LANE))
        o_ref[...] = hid.astype(o_ref.dtype)


def swiglu_up(a, w1, w3, gate_rep=None):
    m, k = a.shape
    n_e, _, f = w1.shape
    n = n_e * f
    tm = _pick(m, 1024, 16)
    tk = _pick(k, 2048, LANE)
    tn = _pick(f, 1024, LANE)
    gated = gate_rep is not None
    nk = k // tk
    per = f // tn
    w_spec = pl.BlockSpec((None, tk, tn), lambda i, j, kk: (j // per, kk, j % per))
    in_specs = [pl.BlockSpec((tm, tk), lambda i, j, kk: (i, kk)), w_spec, w_spec]
    args = [a, w1, w3]
    if gated:
        in_specs.append(pl.BlockSpec((tm, LANE), lambda i, j, kk: (i, j // per)))
        args.append(gate_rep)
    return pl.pallas_call(
        functools.partial(_swiglu_up_kernel, nk=nk, gated=gated),
        grid=(m // tm, n // tn, nk),
        in_specs=in_specs,
        out_specs=pl.BlockSpec((tm, tn), lambda i, j, kk: (i, j)),
        out_shape=jax.ShapeDtypeStruct((m, n), BF16),
        scratch_shapes=[pltpu.VMEM((tm, tn), F32), pltpu.VMEM((tm, tn), F32)],
        compiler_params=_params(("parallel", "parallel", "arbitrary")),
        name="swiglu_up",
    )(*args)


def _resid_kernel(a_ref, w_ref, x_ref, g_ref, o_ref, acc_ref, *, nk):
    k = pl.program_id(2)

    @pl.when(k == 0)
    def _():
        acc_ref[...] = jnp.zeros_like(acc_ref)

    acc_ref[...] += jnp.dot(a_ref[...], w_ref[...], preferred_element_type=F32)

    @pl.when(k == nk - 1)
    def _():
        o_ref[...] = x_ref[...] + g_ref[0] * acc_ref[...]


def resid_matmul(a, w, x, g, rows_per_batch):
    m, k = a.shape
    n = w.shape[1]
    tm = _pick(rows_per_batch, 1024, 16)
    tn = _pick(n, 1024, LANE)
    tk = _pick(k, 2048, LANE)
    nk = k // tk
    per = rows_per_batch // tm
    return pl.pallas_call(
        functools.partial(_resid_kernel, nk=nk),
        grid=(m // tm, n // tn, nk),
        in_specs=[pl.BlockSpec((tm, tk), lambda i, j, kk: (i, kk)),
                  pl.BlockSpec((tk, tn), lambda i, j, kk: (kk, j)),
                  pl.BlockSpec((tm, tn), lambda i, j, kk: (i, j)),
                  pl.BlockSpec((1, 1, tn), lambda i, j, kk: (i // per, 0, j))],
        out_specs=pl.BlockSpec((tm, tn), lambda i, j, kk: (i, j)),
        out_shape=jax.ShapeDtypeStruct((m, n), F32),
        scratch_shapes=[pltpu.VMEM((tm, tn), F32)],
        compiler_params=_params(("parallel", "parallel", "arbitrary")),
        name="resid_matmul",
    )(a, w, x, g)


def _merge_kernel(yr_ref, yf_ref, yc_ref, lg_ref, pr_ref, pf_ref, pc_ref,
                  gwr_ref, gwf_ref, gwc_ref, gbr_ref, gbf_ref, gbc_ref, o_ref):
    lg = lg_ref[...]

    def branch(y_ref, p_ref, gw_ref, gb_ref):
        gate = jax.nn.sigmoid(jnp.dot(lg, gw_ref[...], preferred_element_type=F32) + gb_ref[...])
        return gate * jnp.dot(y_ref[...], p_ref[...], preferred_element_type=F32)

    out = (branch(yr_ref, pr_ref, gwr_ref, gbr_ref) + branch(yf_ref, pf_ref, gwf_ref, gbf_ref)
           + branch(yc_ref, pc_ref, gwc_ref, gbc_ref))
    o_ref[...] = out.astype(o_ref.dtype)


def gated_merge(yr, yf, yc, lg, pr, pf, pc, gw2, gb):
    m = yr.shape[0]
    d = pr.shape[1]
    tm = _pick(m, 1024, 16)
    tn = _pick(d, 512, LANE)
    nj = d // tn

    def rows(arr):
        return pl.BlockSpec((tm, arr.shape[1]), lambda i, j: (i, 0))

    def cols(arr, off):
        return pl.BlockSpec((arr.shape[0], tn), lambda i, j: (0, off * nj + j))

    return pl.pallas_call(
        _merge_kernel,
        grid=(m // tm, nj),
        in_specs=[rows(yr), rows(yf), rows(yc), rows(lg), cols(pr, 0), cols(pf, 0), cols(pc, 0),
                  cols(gw2, 0), cols(gw2, 1), cols(gw2, 2), cols(gb, 0), cols(gb, 1), cols(gb, 2)],
        out_specs=pl.BlockSpec((tm, tn), lambda i, j: (i, j)),
        out_shape=jax.ShapeDtypeStruct((m, d), BF16),
        compiler_params=_params(("parallel", "parallel")),
        name="gated_merge",
    )(yr, yf, yc, lg, pr, pf, pc, gw2, gw2, gw2, gb, gb, gb)


def _bf(x):
    return x.astype(BF16)


def _dot(a, b, passes=1):
    if passes == 1:
        return jnp.dot(_bf(a), _bf(b), preferred_element_type=F32)
    a_hi, b_hi = _bf(a), _bf(b)
    a_lo, b_lo = _bf(a - a_hi.astype(F32)), _bf(b - b_hi.astype(F32))
    return (jnp.dot(a_hi, b_hi, preferred_element_type=F32)
            + jnp.dot(a_hi, b_lo, preferred_element_type=F32)
            + jnp.dot(a_lo, b_hi, preferred_element_type=F32))


def _dot_nt(a, b):
    return lax.dot_general(_bf(a), _bf(b), (((1,), (1,)), ((), ())), preferred_element_type=F32)


def _dot_tn(a, b):
    return lax.dot_general(_bf(a), _bf(b), (((0,), (0,)), ((), ())), preferred_element_type=F32)


def _each(f, *lists):
    return [f(*xs) for xs in zip(*lists)]


def _unit_tri_inverse(a_tri, row, col, size):
    def same_block(shift):
        return (row >> shift) == (col >> shift)

    def idot(a, b):
        return _dot(a, b, INV_PASSES)

    eye = (row == col).astype(F32)
    a8 = _each(lambda a: jnp.where(same_block(3), a, 0.0), a_tri)
    a8_2 = _each(idot, a8, a8)
    a8_4 = _each(idot, a8_2, a8_2)
    x = _each(lambda a: eye - a, a8)
    x = _each(lambda xx, d: xx + d, x, _each(idot, x, a8_2))
    x = _each(lambda xx, d: xx + d, x, _each(idot, x, a8_4))
    shift = 3
    while (1 << shift) < size:
        off = same_block(shift + 1) & jnp.logical_not(same_block(shift))
        e = _each(lambda a: jnp.where(off, a, 0.0), a_tri)
        ex = _each(idot, e, x)
        x = _each(lambda xx, d: xx - d, x, _each(idot, x, ex))
        shift += 1
    return x


def _wkv_kernel(r_ref, lw_ref, k_ref, v_ref, kk_ref, a_ref, s0_ref, y_ref, sf_ref,
                st_ref, al_ref, be_ref, rt_ref, kt_ref, vb_ref, pl_ref,
                *, chunk, hd, heads, reverse, nc):
    c = pl.program_id(2)

    @pl.when(c == 0)
    def _():
        st_ref[...] = s0_ref[0]

    row = lax.broadcasted_iota(jnp.int32, (chunk, chunk), 0)
    col = lax.broadcasted_iota(jnp.int32, (chunk, chunk), 1)
    if reverse:
        strict, incl = col > row, col >= row
    else:
        strict, incl = col < row, col <= row
    tri = jnp.where(incl, 1.0, 0.0).astype(BF16)
    last = 0 if reverse else chunk - 1

    lw = lw_ref[0]
    lw_hi = _bf(lw)
    rem = lw - lw_hi.astype(F32)
    lw_mid = _bf(rem)
    lw_lo = _bf(rem - lw_mid.astype(F32))
    cum = (jnp.dot(tri, lw_hi, preferred_element_type=F32)
           + jnp.dot(tri, lw_mid, preferred_element_type=F32)
           + jnp.dot(tri, lw_lo, preferred_element_type=F32))
    p_in = jnp.exp(cum)
    p_inv = jnp.exp(-cum)
    kk = kk_ref[0].astype(F32)
    rt_ref[...] = _bf(r_ref[0].astype(F32) * p_in)
    kt_ref[...] = _bf(k_ref[0].astype(F32) * p_inv)
    be_ref[...] = _bf(a_ref[0].astype(F32) * kk * p_inv)
    al_ref[...] = _bf(kk * jnp.exp(cum - lw))
    vb_ref[...] = _bf(v_ref[0])
    pl_ref[...] = jnp.broadcast_to(p_in[last:last + 1, :], pl_ref.shape)

    sls = [slice(h * hd, (h + 1) * hd) for h in range(heads)]
    alpha = [al_ref[:, sl] for sl in sls]
    beta = [be_ref[:, sl] for sl in sls]
    rt = [rt_ref[:, sl] for sl in sls]
    kt = [kt_ref[:, sl] for sl in sls]
    v = [vb_ref[:, sl] for sl in sls]
    st0 = [st_ref[h] for h in range(heads)]
    st0_b = _each(_bf, st0)

    ar = _each(lambda x, y: jnp.concatenate([x, y], axis=0), alpha, rt)
    x_b = _each(_dot_nt, ar, beta)
    x_k = _each(_dot_nt, ar, kt)
    a_ab = _each(lambda x: jnp.where(strict, x[:chunk], 0.0), x_b)
    a_rb = _each(lambda x: jnp.where(incl, x[chunk:], 0.0), x_b)
    a_ak = _each(lambda x: jnp.where(strict, x[:chunk], 0.0), x_k)
    a_rk = _each(lambda x: jnp.where(incl, x[chunk:], 0.0), x_k)
    t_inv = _unit_tri_inverse(a_ab, row, col, chunk)

    w_t = _each(_dot, t_inv, alpha)
    u0 = _each(_dot, t_inv, _each(_dot, a_ak, v))
    y0 = _each(_dot, a_rk, v)
    ktv = _each(_dot_tn, kt, v)
    u = _each(lambda x, y: x + y, _each(_dot, w_t, st0_b), u0)
    y1 = _each(_dot, rt, st0_b)
    y2 = _each(_dot, a_rb, u)
    btu = _each(_dot_tn, beta, u)
    p_col = [jnp.transpose(pl_ref[:, sl])[:, :1] for sl in sls]
    for h in range(heads):
        y_ref[0, :, sls[h]] = y0[h] + y1[h] - y2[h]
        st_ref[h] = (st0[h] + ktv[h] - btu[h]) * p_col[h]

    @pl.when(c == nc - 1)
    def _():
        sf_ref[0] = st_ref[...]


def wkv(r, lw, k, v, kk, a, s0t, reverse, hd):
    b, t, width = r.shape
    nh = width // hd
    heads = max(hh for hh in (1, 2, 4, 8, 16, 32) if nh % hh == 0 and hh <= WKV_HEADS)
    chunk = WKV_CHUNK
    nc = t // chunk
    blk = heads * hd

    def tok_map(bi, hi, ci):
        return (bi, nc - 1 - ci if reverse else ci, hi)

    tok_spec = pl.BlockSpec((1, chunk, blk), tok_map)
    st_spec = pl.BlockSpec((1, heads, hd, hd), lambda bi, hi, ci: (bi, hi, 0, 0))
    return pl.pallas_call(
        functools.partial(_wkv_kernel, chunk=chunk, hd=hd, heads=heads, reverse=reverse, nc=nc),
        grid=(b, nh // heads, nc),
        in_specs=[tok_spec] * 6 + [st_spec],
        out_specs=[tok_spec, st_spec],
        out_shape=[jax.ShapeDtypeStruct((b, t, width), F32),
                   jax.ShapeDtypeStruct((b, nh, hd, hd), F32)],
        scratch_shapes=[pltpu.VMEM((heads, hd, hd), F32)]
        + [pltpu.VMEM((chunk, blk), BF16)] * 5 + [pltpu.VMEM((8, blk), F32)],
        compiler_params=pltpu.CompilerParams(
            dimension_semantics=("parallel", "parallel", "arbitrary"),
            vmem_limit_bytes=VMEM_LIMIT),
        name="wkv_rev" if reverse else "wkv_fwd",
    )(r, lw, k, v, kk, a, s0t)


def _from_prev(x):
    return pltpu.roll(x, 1, axis=0)


def _from_next(x):
    return pltpu.roll(x, x.shape[0] - 1, axis=0)


def _token_shift(x, up, dn, has_up, has_dn, mode):
    rows, ch = x.shape
    if mode == "seq":
        half = ch // 2
        t = lax.broadcasted_iota(jnp.int32, (rows, half), 0)
        s0 = jnp.where(t == 0, 0.0, _from_prev(x[:, :half]))
        s1 = jnp.where(t == rows - 1, 0.0, _from_next(x[:, half:]))
        return [s0, s1]
    q = ch // 4
    t = lax.broadcasted_iota(jnp.int32, (rows, q), 0) % GRID_W
    s0 = jnp.where(t == 0, 0.0, _from_prev(x[:, :q]))
    s1 = jnp.where(t == GRID_W - 1, 0.0, _from_next(x[:, q:2 * q]))
    up = jnp.where(has_up, up, 0.0)
    dn = jnp.where(has_dn, dn, 0.0)
    if rows == GRID_W:
        s2, s3 = up, dn
    else:
        s2 = jnp.concatenate([up, x[:rows - GRID_W, 2 * q:3 * q]], axis=0)
        s3 = jnp.concatenate([x[GRID_W:, 3 * q:], dn], axis=0)
    return [s0, s1, s2, s3]


def _prep_kernel(*refs, mode, nblk):
    if mode == "grid":
        x_ref, up_ref, dn_ref, w_ref, sh_ref, sc_ref, o_ref = refs
    else:
        x_ref, w_ref, sh_ref, sc_ref, o_ref = refs
    i = pl.program_id(1)
    w, sh, sc = w_ref[...], sh_ref[0], sc_ref[0]

    def modulated(x):
        y = x * lax.rsqrt(jnp.mean(x * x, axis=-1, keepdims=True) + NORM_EPS)
        return (y * w) * (1 + sc) + sh

    h = modulated(x_ref[0])
    d = h.shape[1]
    if mode == "grid":
        q = d // 4
        up = modulated(up_ref[0])[:, 2 * q:3 * q]
        dn = modulated(dn_ref[0])[:, 3 * q:]
        parts = _token_shift(h, up, dn, i > 0, i < nblk - 1, mode)
    else:
        parts = _token_shift(h, None, None, None, None, mode)
    o_ref[:, :d] = h.astype(o_ref.dtype)
    width = d // len(parts)
    for n, s in enumerate(parts):
        lo = n * width
        o_ref[:, d + lo:d + lo + width] = (s - h[:, lo:lo + width]).astype(o_ref.dtype)


def prep(x, w, shift, scale, mode):
    b, t, d = x.shape
    rows = _pick(t, PREP_ROWS, GRID_W) if mode == "grid" else t
    nblk = t // rows
    per = rows // GRID_W
    x_spec = pl.BlockSpec((1, rows, d), lambda bi, i: (bi, i, 0))
    vec_spec = pl.BlockSpec((1, d), lambda bi, i: (0, 0))
    mod_spec = pl.BlockSpec((1, 1, d), lambda bi, i: (bi, 0, 0))
    in_specs, args = [x_spec], [x]
    if mode == "grid":
        last = t // GRID_W - 1
        in_specs += [pl.BlockSpec((1, GRID_W, d), lambda bi, i: (bi, jnp.maximum(i * per - 1, 0), 0)),
                     pl.BlockSpec((1, GRID_W, d), lambda bi, i: (bi, jnp.minimum((i + 1) * per, last), 0))]
        args += [x, x]
    in_specs += [vec_spec, mod_spec, mod_spec]
    args += [w[None, :], shift, scale]
    return pl.pallas_call(
        functools.partial(_prep_kernel, mode=mode, nblk=nblk),
        grid=(b, nblk),
        in_specs=in_specs,
        out_specs=pl.BlockSpec((rows, 2 * d), lambda bi, i: (bi * nblk + i, 0)),
        out_shape=jax.ShapeDtypeStruct((b * t, 2 * d), BF16),
        compiler_params=_params(("parallel", "parallel")),
        name="prep_" + mode,
    )(*args)


def _norm_mod_kernel(x_ref, w_ref, sh_ref, sc_ref, *rest, routed):
    x = x_ref[...]
    y = x * lax.rsqrt(jnp.mean(x * x, axis=-1, keepdims=True) + NORM_EPS)
    h = (y * w_ref[...]) * (1 + sc_ref[0]) + sh_ref[0]
    if routed:
        rw_ref, rb_ref, o_ref, lg_ref = rest
        lg_ref[...] = jnp.dot(h, rw_ref[...], precision=HI, preferred_element_type=F32) + rb_ref[...]
    else:
        (o_ref,) = rest
    o_ref[...] = h.astype(o_ref.dtype)


def norm_mod(x, w, shift, scale, rows_per_batch, router_w=None, router_b=None):
    m, d = x.shape
    rows = _pick(rows_per_batch, PREP_ROWS, 8)
    nblk = rows_per_batch // rows
    routed = router_w is not None
    x_spec = pl.BlockSpec((rows, d), lambda bi, i: (bi * nblk + i, 0))
    mod_spec = pl.BlockSpec((1, 1, d), lambda bi, i: (bi, 0, 0))
    in_specs = [x_spec, pl.BlockSpec((1, d), lambda bi, i: (0, 0)), mod_spec, mod_spec]
    args = [x, w[None, :], shift, scale]
    out_specs, out_shape = [x_spec], [jax.ShapeDtypeStruct((m, d), BF16)]
    if routed:
        n_e = router_w.shape[1]
        pad = -n_e % LANE
        in_specs += [pl.BlockSpec((d, n_e + pad), lambda bi, i: (0, 0)),
                     pl.BlockSpec((1, n_e + pad), lambda bi, i: (0, 0))]
        args += [jnp.pad(router_w, ((0, 0), (0, pad))), jnp.pad(router_b, (0, pad))[None, :]]
        out_specs.append(pl.BlockSpec((rows, n_e + pad), lambda bi, i: (bi * nblk + i, 0)))
        out_shape.append(jax.ShapeDtypeStruct((m, n_e + pad), F32))
    out = pl.pallas_call(
        functools.partial(_norm_mod_kernel, routed=routed),
        grid=(m // rows_per_batch, nblk),
        in_specs=in_specs,
        out_specs=out_specs,
        out_shape=out_shape,
        compiler_params=_params(("parallel", "parallel")),
        name="norm_mod",
    )(*args)
    return (out[0], out[1][:, :router_w.shape[1]]) if routed else (out[0], None)


def _head_sums(x, hd):
    rows, width = x.shape
    nt = width // LANE
    stacked = jnp.concatenate([x[:, i * LANE:(i + 1) * LANE] for i in range(nt)], axis=0)
    li = lax.broadcasted_iota(jnp.int32, (LANE, LANE), 0) // hd
    lj = lax.broadcasted_iota(jnp.int32, (LANE, LANE), 1) // hd
    ones = jnp.where(li == lj, 1.0, 0.0).astype(BF16)
    hi = _bf(stacked)
    rem = stacked - hi.astype(F32)
    mid = _bf(rem)
    lo = _bf(rem - mid.astype(F32))
    s = (jnp.dot(hi, ones, preferred_element_type=F32) + jnp.dot(mid, ones, preferred_element_type=F32)
         + jnp.dot(lo, ones, preferred_element_type=F32))
    return jnp.concatenate([s[i * rows:(i + 1) * rows] for i in range(nt)], axis=1)


def _wkv_prep_kernel(*refs, mode, nblk, hd, offs, has_vres):
    refs = list(refs)
    main = [refs.pop(0) for _ in range(3)]
    halo = [(refs.pop(0), refs.pop(0)) for _ in range(3)] if mode == "grid" else [(None, None)] * 3
    low_ref = refs.pop(0)
    vf_ref = refs.pop(0) if has_vres else None
    mu_ref, w0_ref, a0_ref, kk_ref, ka_ref, w2_ref, a2_ref = [refs.pop(0) for _ in range(7)]
    if has_vres:
        v0_ref, v2_ref = refs.pop(0), refs.pop(0)
    r_o, v_o, kk_o, kdf_o, kdr_o, af_o, ar_o, lwf_o, lwr_o = refs
    c = pl.program_id(1)
    low = low_ref[0]

    def lerp(n):
        x = main[n][0]
        up, dn = (halo[n][0][0], halo[n][1][0]) if mode == "grid" else (None, None)
        sh = jnp.concatenate(_token_shift(x, up, dn, c > 0, c < nblk - 1, mode), axis=1)
        return x + (sh - x) * mu_ref[n:n + 1, :]

    def low_dot(lo, hi_, w, act=None):
        z = low[:, lo:hi_]
        if act is not None:
            z = act(z)
        return jnp.dot(_bf(z), w, preferred_element_type=F32)

    r = lerp(0)
    k = lerp(1)
    v = lerp(2)
    if has_vres:
        gate = jax.nn.sigmoid(v0_ref[...] + low_dot(offs[6], offs[7], v2_ref[...]))
        v = v + (vf_ref[0].astype(F32) - v) * gate
    kq = k * kk_ref[...]
    kk = kq * lax.rsqrt(jnp.maximum(_head_sums(kq * kq, hd), 1e-24))
    r_o[0] = r.astype(r_o.dtype)
    v_o[0] = v.astype(v_o.dtype)
    kk_o[0] = kk.astype(kk_o.dtype)
    for di, (kd_o, a_o, lw_o) in enumerate(((kdf_o, af_o, lwf_o), (kdr_o, ar_o, lwr_o))):
        wl = w0_ref[di:di + 1, :] + low_dot(offs[di], offs[di + 1], w2_ref[di], jnp.tanh)
        softplus = jnp.maximum(-wl, 0.0) + jnp.log1p(jnp.exp(-jnp.abs(wl)))
        lw_o[0] = -jnp.exp(-softplus - DECAY_OFFSET)
        a = jax.nn.sigmoid(a0_ref[di:di + 1, :] + low_dot(offs[2 + di], offs[3 + di], a2_ref[di]))
        a_o[0] = a.astype(a_o.dtype)
        kd_o[0] = (k * (1.0 + (a - 1.0) * ka_ref[...])).astype(kd_o.dtype)


def wkv_prep(p, low, v_first, mu, w0, a0, k_k, k_a, w2, a2, v0, v2, offs, hd, mode):
    b, t, rw3 = p.shape
    rw = rw3 // 3
    q = rw // 4
    rows = GRID_W if mode == "grid" else t
    nblk = t // rows
    has_vres = v_first is not None

    def tok(width, col):
        return pl.BlockSpec((1, rows, width), lambda bi, ci: (bi, ci, col))

    def full(arr):
        nd = arr.ndim
        return pl.BlockSpec(arr.shape, lambda bi, ci: (0,) * nd)

    in_specs = [tok(rw, n) for n in range(3)]
    args = [p, p, p]
    if mode == "grid":
        for n in range(3):
            in_specs += [pl.BlockSpec((1, rows, q), lambda bi, ci, n=n: (bi, jnp.maximum(ci - 1, 0), 4 * n + 2)),
                         pl.BlockSpec((1, rows, q), lambda bi, ci, n=n: (bi, jnp.minimum(ci + 1, nblk - 1), 4 * n + 3))]
            args += [p, p]
    in_specs.append(tok(low.shape[-1], 0))
    args.append(low)
    if has_vres:
        in_specs.append(tok(rw, 0))
        args.append(v_first)
    small = [mu, w0, a0, k_k[None, :], k_a[None, :], w2.astype(BF16), a2.astype(BF16)]
    if has_vres:
        small += [v0[None, :], v2.astype(BF16)]
    in_specs += [full(s) for s in small]
    args += small
    out_spec = tok(rw, 0)
    shapes = [jax.ShapeDtypeStruct((b, t, rw), BF16)] * 7 + [jax.ShapeDtypeStruct((b, t, rw), F32)] * 2
    return pl.pallas_call(
        functools.partial(_wkv_prep_kernel, mode=mode, nblk=nblk, hd=hd, offs=tuple(offs), has_vres=has_vres),
        grid=(b, nblk),
        in_specs=in_specs,
        out_specs=[out_spec] * 9,
        out_shape=shapes,
        compiler_params=_params(("parallel", "parallel")),
        name="wkv_prep_" + mode,
    )(*args)


def _wkv_post_kernel(yf_ref, yr_ref, r_ref, kdf_ref, kdr_ref, v_ref, lg_ref, rk_ref, gw_ref, gb_ref,
                     g2_ref, o_ref, *, hd):
    y = yf_ref[0] + yr_ref[0]
    mean = _head_sums(y, hd) * (1.0 / hd)
    yc = y - mean
    var = _head_sums(yc * yc, hd) * (1.0 / hd)
    y = yc * lax.rsqrt(var + GN_EPS) * gw_ref[...] + gb_ref[...]
    rk = r_ref[0].astype(F32) * (kdf_ref[0].astype(F32) + kdr_ref[0].astype(F32)) * rk_ref[...]
    y = y + _head_sums(rk, hd) * v_ref[0].astype(F32)
    g = jnp.dot(_bf(jax.nn.sigmoid(lg_ref[0])), g2_ref[...], preferred_element_type=F32)
    o_ref[0] = (y * g).astype(o_ref.dtype)


def wkv_post(y_f, y_r, r, kd_f, kd_r, v, low_g, r_k, gn_w, gn_b, g2, hd):
    b, t, rw = y_f.shape
    rows = _pick(t, POST_ROWS, 8)

    def tok(width):
        return pl.BlockSpec((1, rows, width), lambda bi, ci: (bi, ci, 0))

    def full(arr):
        return pl.BlockSpec(arr.shape, lambda bi, ci: (0, 0))

    small = [r_k.reshape(1, rw), gn_w[None, :], gn_b[None, :], g2.astype(BF16)]
    return pl.pallas_call(
        functools.partial(_wkv_post_kernel, hd=hd),
        grid=(b, t // rows),
        in_specs=[tok(rw)] * 6 + [tok(low_g.shape[-1])] + [full(s) for s in small],
        out_specs=tok(rw),
        out_shape=jax.ShapeDtypeStruct((b, t, rw), BF16),
        compiler_params=_params(("parallel", "parallel")),
        name="wkv_post",
    )(y_f, y_r, r, kd_f, kd_r, v, low_g, *small)


def rmsnorm(x, w):
    y = x * lax.rsqrt(jnp.mean(x * x, axis=-1, keepdims=True) + NORM_EPS)
    return y * w


def modulate(x, w, shift, scale):
    return rmsnorm(x, w) * (1 + scale) + shift


def dft_tables(rows, cols, n, scale):
    j = lax.broadcasted_iota(jnp.int32, (rows, cols), 0)
    k = lax.broadcasted_iota(jnp.int32, (rows, cols), 1)
    ang = ((j * k) % n).astype(F32) * (2.0 * math.pi / n)
    return jnp.cos(ang) * scale, jnp.sin(ang) * scale


def _dft_stage_kernel(l_ref, zr_ref, zi_ref, *rest, twiddle):
    z = jnp.concatenate([zr_ref[...], zi_ref[...]], axis=0)
    y = jnp.dot(l_ref[...], z, preferred_element_type=F32)
    if not twiddle:
        rest[0][...] = y.astype(rest[0].dtype)
        return
    twr_ref, twi_ref, o_ref = rest
    half = y.shape[0] // 2
    reps = y.shape[1] // LANE
    tr, ti = jnp.tile(twr_ref[...], (1, reps)), jnp.tile(twi_ref[...], (1, reps))
    yr, yi = y[:half], y[half:]
    o_ref[0] = (yr * tr - yi * ti).astype(o_ref.dtype)
    o_ref[1] = (yr * ti + yi * tr).astype(o_ref.dtype)


class PosDft:
    def __init__(self, n):
        self.n = n
        self.n1 = DFT_N1 if n % DFT_N1 == 0 and n // DFT_N1 >= 8 and (n // DFT_N1) % 8 == 0 else 0
        if not self.n1:
            c, s = dft_tables(n, n, n, n ** -0.5)
            self.direct = jnp.concatenate([c, s], axis=1).astype(BF16)
            return
        n1, n2 = self.n1, n // self.n1
        self.n2 = n2
        c1, s1 = dft_tables(n1, n1, n1, n1 ** -0.5)
        self.l1 = jnp.concatenate([jnp.concatenate([c1, s1], axis=1),
                                   jnp.concatenate([-s1, c1], axis=1)], axis=0).astype(BF16)
        tc, ts = dft_tables(n2, n1, n, 1.0)
        self.twr = jnp.broadcast_to(tc[:, :, None], (n2, n1, LANE))
        self.twi = jnp.broadcast_to(-ts[:, :, None], (n2, n1, LANE))
        c2, s2 = dft_tables(n2, n2, n2, n2 ** -0.5)
        self.l3 = jnp.concatenate([c2, s2], axis=1).astype(BF16)


def fourier_mix(u, pos, chan_table):
    b, n, fw = u.shape
    z = matmul(u.reshape(b * n, fw).astype(BF16), chan_table, BF16, planes=2)
    if not pos.n1:
        z = z.reshape(2, b, n, fw).transpose(1, 0, 2, 3).reshape(b, 2 * n, fw)
        return jnp.stack([matmul(pos.direct, z[i], BF16) for i in range(b)], axis=0)
    n1, n2 = pos.n1, pos.n2
    z = z.reshape(2, b, n1, n2 * fw)
    y = pl.pallas_call(
        functools.partial(_dft_stage_kernel, twiddle=True),
        grid=(b, n2),
        in_specs=[pl.BlockSpec((2 * n1, 2 * n1), lambda bi, j: (0, 0)),
                  pl.BlockSpec((None, None, n1, fw), lambda bi, j: (0, bi, 0, j)),
                  pl.BlockSpec((None, None, n1, fw), lambda bi, j: (1, bi, 0, j)),
                  pl.BlockSpec((None, n1, LANE), lambda bi, j: (j, 0, 0)),
                  pl.BlockSpec((None, n1, LANE), lambda bi, j: (j, 0, 0))],
        out_specs=pl.BlockSpec((None, 2, n1, fw), lambda bi, j: (bi, 0, 0, j)),
        out_shape=jax.ShapeDtypeStruct((b, 2, n1, n2 * fw), BF16),
        compiler_params=_params(("parallel", "parallel")),
        name="dft_stage1",
    )(pos.l1, z, z, pos.twr, pos.twi)
    y = y.reshape(b, 2, n1, n2, fw).transpose(0, 1, 3, 2, 4).reshape(b, 2, n2, n1 * fw)
    tn = _pick(n1 * fw, 4096, LANE)
    out = pl.pallas_call(
        functools.partial(_dft_stage_kernel, twiddle=False),
        grid=(b, n1 * fw // tn),
        in_specs=[pl.BlockSpec((n2, 2 * n2), lambda bi, j: (0, 0)),
                  pl.BlockSpec((None, None, n2, tn), lambda bi, j: (bi, 0, 0, j)),
                  pl.BlockSpec((None, None, n2, tn), lambda bi, j: (bi, 1, 0, j))],
        out_specs=pl.BlockSpec((None, n2, tn), lambda bi, j: (bi, 0, j)),
        out_shape=jax.ShapeDtypeStruct((b, n2, n1 * fw), BF16),
        compiler_params=_params(("parallel", "parallel")),
        name="dft_stage2",
    )(pos.l3, y, y)
    return out.reshape(b, n, fw)


def depthwise_conv(u, w):
    n = u.shape[1]
    kk = w.shape[0]
    pad = kk // 2
    up = jnp.pad(u, ((0, 0), (pad, pad), (0, 0)))
    out = up[:, 0:n] * w[0]
    for i in range(1, kk):
        out = out + up[:, i:i + n] * w[i]
    return out


def kernel(x, c, ctx, c_ctx, ada_a, ada_b, ada_bias, norm1_w, norm2_w, w_in, mu_rkv, mu_lr, decay_w0, decay_w1, decay_w2, iclr_a0, iclr_a1, iclr_a2, ogate_g1, ogate_g2, k_k, k_a, r_k, gn_w, gn_b, vres_mu, vres_v0, vres_v1, vres_v2, conv_w, gate_w1, gate_w2, gate_b, proj_rwkv, proj_fourier, proj_conv, w_out, ffn_w1, ffn_w3, ffn_w2, router_w, router_b, moe_w1, moe_w3, moe_w2, final_norm_w):
    depth = w_in.shape[0]
    bsz, seq, d = x.shape
    ctx_len = ctx.shape[1]
    rw = mu_rkv.shape[-1]
    nh, hd = r_k.shape[1], r_k.shape[2]
    fw = proj_fourier.shape[1]
    cw = proj_conv.shape[1]
    four_off = 3 * rw
    conv_off = four_off + fw
    n_exp = router_w.shape[-1]
    d_exp = moe_w1.shape[-1]
    n_mod = ada_b.shape[-1] // d
    r_dec, r_icl, r_og, r_vr, r_gate = (decay_w1.shape[-1], iclr_a1.shape[-1], ogate_g1.shape[-1],
                                        vres_v1.shape[-1], gate_w1.shape[-1])

    gc = fw // FOURIER_GROUPS
    cc, cs = dft_tables(gc, gc, gc, gc ** -0.5)
    eye_g = jnp.eye(FOURIER_GROUPS, dtype=F32)
    chan_table = jnp.concatenate([jnp.kron(eye_g, cc), -jnp.kron(eye_g, cs)], axis=1).astype(BF16)
    pos_lat, pos_ctx = PosDft(seq), PosDft(ctx_len)

    cond_lat = jax.nn.silu(c)
    cond_ctx = jax.nn.silu(c_ctx)[None, :]
    cond = jnp.concatenate([cond_lat, cond_ctx], axis=0)
    cond = jnp.pad(cond, ((0, 16 - cond.shape[0] % 16), (0, 0)))

    def tokens(xs, mod, l, mode, pos_tab, s0, v_first, last_ctx):
        b, t, _ = xs.shape
        has_vres = l > 0
        hd_cat = prep(xs, norm1_w[l], jnp.broadcast_to(mod[0], (b, 1, d)),
                      jnp.broadcast_to(mod[1], (b, 1, d)), mode)

        w_h = [decay_w1[l, 0], decay_w1[l, 1], iclr_a1[l, 0], iclr_a1[l, 1], ogate_g1[l], gate_w1[l]]
        mus = [mu_lr[l, 0], mu_lr[l, 0], mu_lr[l, 1], mu_lr[l, 1], mu_lr[l, 2], None]
        if has_vres:
            w_h.append(vres_v1[l - 1])
            mus.append(vres_mu[l - 1])
        w_dh = [jnp.zeros_like(w) if m is None else w * m[:, None] for w, m in zip(w_h, mus)]
        w_low = jnp.concatenate([jnp.concatenate(w_h, axis=1), jnp.concatenate(w_dh, axis=1)], axis=0)
        n_low = w_low.shape[1]
        w_low = jnp.pad(w_low, ((0, 0), (0, -n_low % LANE)))
        low = matmul(hd_cat, w_low.astype(BF16)).reshape(b, t, -1)
        offs = [0]
        for w in w_h:
            offs.append(offs[-1] + w.shape[1])
        low_g = low[..., offs[4]:offs[5]]
        low_gate = low[..., offs[5]:offs[6]]

        w_in_b = w_in[l].astype(BF16)
        p = matmul(hd_cat, w_in_b, n_cols=four_off).reshape(b, t, -1)
        p_fc = matmul(hd_cat, w_in_b, BF16, n_cols=w_in_b.shape[1] - four_off,
                      col_off=four_off).reshape(b, t, -1)

        r, v, kk, kd_f, kd_r, a_f, a_r, lw_f, lw_r = wkv_prep(
            p, low, v_first, mu_rkv[l], decay_w0[l], iclr_a0[l], k_k[l], k_a[l], decay_w2[l], iclr_a2[l],
            vres_v0[l - 1] if has_vres else None, vres_v2[l - 1] if has_vres else None, offs, hd, mode)

        ys, states = [], []
        for di, (reverse, lw, kd, a) in enumerate(((False, lw_f, kd_f, a_f), (True, lw_r, kd_r, a_r))):
            init = jnp.zeros((b, nh, hd, hd), F32) if s0 is None else s0[di]
            y_d, s_d = wkv(r, lw, kd, v, kk, a, init, reverse, hd)
            ys.append(y_d)
            states.append(s_d)
        if last_ctx:
            return None, states, v

        y_rwkv = wkv_post(ys[0], ys[1], r, kd_f, kd_r, v, low_g, r_k[l], gn_w[l], gn_b[l],
                          ogate_g2[l], hd)

        y_four = fourier_mix(p_fc[..., :fw], pos_tab, chan_table)
        gate_b_ = p_fc[..., fw:fw + cw].astype(F32)
        gate_c_ = p_fc[..., fw + cw:fw + 2 * cw].astype(F32)
        uu = p_fc[..., fw + 2 * cw:].astype(F32)
        y_conv = gate_b_ * depthwise_conv(gate_c_ * uu, conv_w[l])

        rows = b * t

        def flat(z):
            return z.reshape(rows, -1).astype(BF16)

        merged = gated_merge(flat(y_rwkv), flat(y_four), flat(y_conv), flat(low_gate),
                             proj_rwkv[l].astype(BF16), proj_fourier[l].astype(BF16),
                             proj_conv[l].astype(BF16), gate_w2[l].astype(BF16), gate_b[l][None, :])
        xs = resid_matmul(merged, w_out[l].astype(BF16), xs.reshape(rows, d),
                          jnp.broadcast_to(mod[2], (b, 1, d)), t)

        j = l // 2
        dense = l % 2 == 0
        h2, logits = norm_mod(xs, norm2_w[l], jnp.broadcast_to(mod[3], (b, 1, d)),
                              jnp.broadcast_to(mod[4], (b, 1, d)), t,
                              None if dense else router_w[j], None if dense else router_b[j])
        if dense:
            hid = swiglu_up(h2, ffn_w1[j][None].astype(BF16), ffn_w3[j][None].astype(BF16))
            w2 = ffn_w2[j]
        else:
            top_val, top_idx = lax.top_k(logits, TOP_K)
            weights = jax.nn.softmax(top_val, axis=-1)
            gate = jnp.sum(jax.nn.one_hot(top_idx, n_exp, dtype=F32) * weights[..., None], axis=-2)
            gate_rep = jnp.repeat(gate, LANE, axis=-1)
            hid = swiglu_up(h2, moe_w1[j].astype(BF16), moe_w3[j].astype(BF16), gate_rep)
            w2 = moe_w2[j].reshape(n_exp * d_exp, d)
        xs = resid_matmul(hid, w2.astype(BF16), xs, jnp.broadcast_to(mod[5], (b, 1, d)), t)
        return xs.reshape(b, t, d), states, v

    xl, xc = x, ctx
    v_first_l = v_first_c = None
    for l in range(depth):
        last = l == depth - 1
        m = mm(mm(cond, ada_a[l]), ada_b[l]) + ada_bias[l]
        mod_l = [m[:bsz, None, i * d:(i + 1) * d] for i in range(n_mod)]
        mod_c = [m[bsz:bsz + 1, None, i * d:(i + 1) * d] for i in range(n_mod)]

        xc_new, ctx_states, vc = tokens(xc, mod_c, l, "seq", pos_ctx, None, v_first_c, last)
        xl, _, vl = tokens(xl, mod_l, l, "grid", pos_lat, ctx_states, v_first_l, False)
        if l == 0:
            v_first_c, v_first_l = vc, vl
        if not last:
            xc = xc_new
    return rmsnorm(xl, final_norm_w)
```

```python
import functools
import math

import jax
import jax.numpy as jnp
from jax import lax
from jax.experimental import pallas as pl
from jax.experimental.pallas import tpu as pltpu

F32 = jnp.float32
BF16 = jnp.bfloat16

GRID_W = 64
FOURIER_GROUPS = 8
TOP_K = 2
NORM_EPS = 1e-6
GN_EPS = 64e-5
DECAY_OFFSET = 0.5
WKV_CHUNK = 64
WKV_HEADS = 32
DFT_N1 = 128
PREP_ROWS = 256
POST_ROWS = 256
LANE = 128
VMEM_LIMIT = 56 * 1024 * 1024

HI = lax.Precision.HIGHEST


def _pick(dim, target, align):
    if dim <= target:
        return dim
    t = (target // align) * align
    while t >= align:
        if dim % t == 0:
            return t
        t -= align
    return dim


def _mm_kernel(a_ref, b_ref, o_ref, acc_ref, *, nk):
    k = pl.program_id(2)

    @pl.when(k == 0)
    def _():
        acc_ref[...] = jnp.zeros_like(acc_ref)

    acc_ref[...] += jnp.dot(a_ref[...], b_ref[...], preferred_element_type=F32)

    @pl.when(k == nk - 1)
    def _():
        o_ref[...] = acc_ref[...].astype(o_ref.dtype)


def matmul(a, b, out_dtype=F32, tm=1024, tn=1024, tk=2048, n_cols=None, col_off=0, planes=1, layer=None):
    m = a.shape[0]
    k = b.shape[-2]
    n = b.shape[-1] if n_cols is None else n_cols
    tm = _pick(m, tm, 16)
    tn = _pick(math.gcd(n // planes, col_off) if col_off else n // planes, tn, LANE)
    tk = _pick(k, tk, LANE)
    nk = k // tk
    joff = col_off // tn
    if planes == 1:
        out_spec = pl.BlockSpec((tm, tn), lambda i, j, kk: (i, j))
        out_shape = jax.ShapeDtypeStruct((m, n), out_dtype)
    else:
        per = n // planes // tn
        out_spec = pl.BlockSpec((None, tm, tn), lambda i, j, kk: (j // per, i, j % per))
        out_shape = jax.ShapeDtypeStruct((planes, m, n // planes), out_dtype)
    if layer is None:
        b_spec = pl.BlockSpec((tk, tn), lambda i, j, kk: (kk, j + joff))
    else:
        b_spec = pl.BlockSpec((None, tk, tn), lambda i, j, kk: (layer, kk, j + joff))
    return pl.pallas_call(
        functools.partial(_mm_kernel, nk=nk),
        grid=(m // tm, n // tn, nk),
        in_specs=[pl.BlockSpec((tm, tk), lambda i, j, kk: (i, kk)), b_spec],
        out_specs=out_spec,
        out_shape=out_shape,
        scratch_shapes=[pltpu.VMEM((tm, tn), F32)],
        compiler_params=pltpu.CompilerParams(
            dimension_semantics=("parallel", "parallel", "arbitrary"),
            vmem_limit_bytes=VMEM_LIMIT),
        name="matmul",
    )(a, b)


def mm(a, b, out_dtype=F32, **kw):
    lead = a.shape[:-1]
    out = matmul(a.reshape(-1, a.shape[-1]).astype(BF16), b.astype(BF16), out_dtype, **kw)
    return out.reshape(lead + (b.shape[-1],))


def _params(sem):
    return pltpu.CompilerParams(dimension_semantics=sem, vmem_limit_bytes=VMEM_LIMIT)


def _swiglu_up_kernel(a_ref, w1_ref, w3_ref, *rest, nk, gated):
    if gated:
        g_ref, o_ref, acc1_ref, acc3_ref = rest
    else:
        o_ref, acc1_ref, acc3_ref = rest
    k = pl.program_id(2)

    @pl.when(k == 0)
    def _():
        acc1_ref[...] = jnp.zeros_like(acc1_ref)
        acc3_ref[...] = jnp.zeros_like(acc3_ref)

    a = a_ref[...]
    acc1_ref[...] += jnp.dot(a, w1_ref[...], preferred_element_type=F32)
    acc3_ref[...] += jnp.dot(a, w3_ref[...], preferred_element_type=F32)

    @pl.when(k == nk - 1)
    def _():
        h1 = acc1_ref[...]
        hid = h1 * jax.nn.sigmoid(h1) * acc3_ref[...]
        if gated:
            hid = hid * jnp.tile(g_ref[...], (1, hid.shape[1] // LANE))
        o_ref[...] = hid.astype(o_ref.dtype)


def swiglu_up(a, w1, w3, gate_rep=None, e0=0, n_e=None):
    m, k = a.shape
    f = w1.shape[-1]
    n_e = w1.shape[0] if n_e is None else n_e
    n = n_e * f
    tm = _pick(m, 1024, 16)
    tk = _pick(k, 2048, LANE)
    tn = _pick(f, 1024, LANE)
    gated = gate_rep is not None
    nk = k // tk
    per = f // tn
    w_spec = pl.BlockSpec((None, tk, tn), lambda i, j, kk: (e0 + j // per, kk, j % per))
    in_specs = [pl.BlockSpec((tm, tk), lambda i, j, kk: (i, kk)), w_spec, w_spec]
    args = [a, w1, w3]
    if gated:
        in_specs.append(pl.BlockSpec((tm, LANE), lambda i, j, kk: (i, j // per)))
        args.append(gate_rep)
    return pl.pallas_call(
        functools.partial(_swiglu_up_kernel, nk=nk, gated=gated),
        grid=(m // tm, n // tn, nk),
        in_specs=in_specs,
        out_specs=pl.BlockSpec((tm, tn), lambda i, j, kk: (i, j)),
        out_shape=jax.ShapeDtypeStruct((m, n), BF16),
        scratch_shapes=[pltpu.VMEM((tm, tn), F32), pltpu.VMEM((tm, tn), F32)],
        compiler_params=_params(("parallel", "parallel", "arbitrary")),
        name="swiglu_up",
    )(*args)


def _resid_kernel(a_ref, w_ref, x_ref, g_ref, o_ref, acc_ref, *, nk):
    k = pl.program_id(2)

    @pl.when(k == 0)
    def _():
        acc_ref[...] = jnp.zeros_like(acc_ref)

    acc_ref[...] += jnp.dot(a_ref[...], w_ref[...], preferred_element_type=F32)

    @pl.when(k == nk - 1)
    def _():
        o_ref[...] = x_ref[...] + g_ref[0] * acc_ref[...]


def resid_matmul(a, w, x, g, rows_per_batch, layer):
    m, k = a.shape
    n = w.shape[-1]
    tm = _pick(rows_per_batch, 1024, 16)
    tn = _pick(n, 1024, LANE)
    tk = _pick(k, 2048, LANE)
    nk = k // tk
    per = rows_per_batch // tm
    return pl.pallas_call(
        functools.partial(_resid_kernel, nk=nk),
        grid=(m // tm, n // tn, nk),
        in_specs=[pl.BlockSpec((tm, tk), lambda i, j, kk: (i, kk)),
                  pl.BlockSpec((None, tk, tn), lambda i, j, kk: (layer, kk, j)),
                  pl.BlockSpec((tm, tn), lambda i, j, kk: (i, j)),
                  pl.BlockSpec((1, 1, tn), lambda i, j, kk: (i // per, 0, j))],
        out_specs=pl.BlockSpec((tm, tn), lambda i, j, kk: (i, j)),
        out_shape=jax.ShapeDtypeStruct((m, n), F32),
        scratch_shapes=[pltpu.VMEM((tm, tn), F32)],
        compiler_params=_params(("parallel", "parallel", "arbitrary")),
        name="resid_matmul",
    )(a, w, x, g)


def _merge_kernel(yr_ref, yf_ref, yc_ref, lg_ref, pr_ref, pf_ref, pc_ref,
                  gwr_ref, gwf_ref, gwc_ref, gbr_ref, gbf_ref, gbc_ref, o_ref):
    lg = lg_ref[...]

    def branch(y_ref, p_ref, gw_ref, gb_ref):
        gate = jax.nn.sigmoid(jnp.dot(lg, gw_ref[...], preferred_element_type=F32) + gb_ref[...])
        return gate * jnp.dot(y_ref[...], p_ref[...], preferred_element_type=F32)

    out = (branch(yr_ref, pr_ref, gwr_ref, gbr_ref) + branch(yf_ref, pf_ref, gwf_ref, gbf_ref)
           + branch(yc_ref, pc_ref, gwc_ref, gbc_ref))
    o_ref[...] = out.astype(o_ref.dtype)


def gated_merge(yr, yf, yc, lg, pr, pf, pc, gw2, gb, layer):
    m = yr.shape[0]
    d = pr.shape[-1]
    tm = _pick(m, 1024, 16)
    tn = _pick(d, 512, LANE)
    nj = d // tn

    def rows(arr):
        return pl.BlockSpec((tm, arr.shape[1]), lambda i, j: (i, 0))

    def cols(arr, off):
        return pl.BlockSpec((None, arr.shape[1], tn), lambda i, j: (layer, 0, off * nj + j))

    return pl.pallas_call(
        _merge_kernel,
        grid=(m // tm, nj),
        in_specs=[rows(yr), rows(yf), rows(yc), rows(lg), cols(pr, 0), cols(pf, 0), cols(pc, 0),
                  cols(gw2, 0), cols(gw2, 1), cols(gw2, 2), cols(gb, 0), cols(gb, 1), cols(gb, 2)],
        out_specs=pl.BlockSpec((tm, tn), lambda i, j: (i, j)),
        out_shape=jax.ShapeDtypeStruct((m, d), BF16),
        compiler_params=_params(("parallel", "parallel")),
        name="gated_merge",
    )(yr, yf, yc, lg, pr, pf, pc, gw2, gw2, gw2, gb, gb, gb)


def _bf(x):
    return x.astype(BF16)


def _dot(a, b):
    return jnp.dot(_bf(a), _bf(b), preferred_element_type=F32)


def _dot_nt(a, b):
    return lax.dot_general(_bf(a), _bf(b), (((1,), (1,)), ((), ())), preferred_element_type=F32)


def _dot_tn(a, b):
    return lax.dot_general(_bf(a), _bf(b), (((0,), (0,)), ((), ())), preferred_element_type=F32)


def _each(f, *lists):
    return [f(*xs) for xs in zip(*lists)]


def _block_diag(x, width, pack):
    if pack == 1:
        return x
    lane_g = lax.broadcasted_iota(jnp.int32, x.shape, 1) // width
    zero = jnp.zeros_like(x)
    return jnp.concatenate([jnp.where(lane_g == g, x, zero) for g in range(pack)], axis=0)


def _unit_tri_inverse(a_tri, row, col, size, pack):
    def same_block(shift):
        return (row >> shift) == (col >> shift)

    def idot(a, b):
        return _dot(a, _block_diag(_bf(b), size, pack))

    eye = (row == col).astype(F32)
    a8 = _each(lambda a: jnp.where(same_block(3), a, 0.0), a_tri)
    a8_2 = _each(idot, a8, a8)
    a8_4 = _each(idot, a8_2, a8_2)
    x = _each(lambda a: eye - a, a8)
    x = _each(lambda xx, d: xx + d, x, _each(idot, x, a8_2))
    x = _each(lambda xx, d: xx + d, x, _each(idot, x, a8_4))
    shift = 3
    while (1 << shift) < size:
        off = same_block(shift + 1) & jnp.logical_not(same_block(shift))
        e = _each(lambda a: jnp.where(off, a, 0.0), a_tri)
        ex = _each(idot, e, x)
        x = _each(lambda xx, d: xx - d, x, _each(idot, x, ex))
        shift += 1
    return x


def _wkv_kernel(r_ref, lw_ref, k_ref, v_ref, kk_ref, a_ref, s0_ref, y_ref, sf_ref,
                st_ref, al_ref, be_ref, rt_ref, kt_ref, vb_ref, pl_ref,
                *, chunk, hd, heads, pack, reverse, nc):
    c = pl.program_id(2)

    @pl.when(c == 0)
    def _():
        st_ref[...] = s0_ref[0]

    row = lax.broadcasted_iota(jnp.int32, (chunk, pack * chunk), 0)
    col = lax.broadcasted_iota(jnp.int32, (chunk, pack * chunk), 1) % chunk
    if reverse:
        strict, incl = col > row, col >= row
    else:
        strict, incl = col < row, col <= row
    tri = jnp.where(incl[:, :chunk], 1.0, 0.0).astype(BF16)
    last = 0 if reverse else chunk - 1

    lw = lw_ref[0]
    lw_hi = _bf(lw)
    rem = lw - lw_hi.astype(F32)
    lw_mid = _bf(rem)
    lw_lo = _bf(rem - lw_mid.astype(F32))
    cum = (jnp.dot(tri, lw_hi, preferred_element_type=F32)
           + jnp.dot(tri, lw_mid, preferred_element_type=F32)
           + jnp.dot(tri, lw_lo, preferred_element_type=F32))
    p_in = jnp.exp(cum)
    p_inv = jnp.exp(-cum)
    kk = kk_ref[0].astype(F32)
    rt_ref[...] = _bf(r_ref[0].astype(F32) * p_in)
    kt_ref[...] = _bf(k_ref[0].astype(F32) * p_inv)
    be_ref[...] = _bf(a_ref[0].astype(F32) * kk * p_inv)
    al_ref[...] = _bf(kk * jnp.exp(cum - lw))
    vb_ref[...] = _bf(v_ref[0])
    pl_ref[...] = jnp.broadcast_to(p_in[last:last + 1, :], pl_ref.shape)

    width = pack * hd
    groups = heads // pack
    sls = [slice(g * width, (g + 1) * width) for g in range(groups)]
    alpha = [al_ref[:, sl] for sl in sls]
    beta = [be_ref[:, sl] for sl in sls]
    rt = [rt_ref[:, sl] for sl in sls]
    kt = [kt_ref[:, sl] for sl in sls]
    v = [vb_ref[:, sl] for sl in sls]
    st0 = [st_ref[g] for g in range(groups)]
    st0_b = _each(_bf, st0)

    def bd_keys(x):
        return _block_diag(x, hd, pack)

    def bdot(a, x):
        return _dot(a, _block_diag(_bf(x), hd, pack))

    ar = _each(lambda x, y: jnp.concatenate([x, y], axis=0), alpha, rt)
    x_b = _each(_dot_nt, ar, _each(bd_keys, beta))
    x_k = _each(_dot_nt, ar, _each(bd_keys, kt))
    a_ab = _each(lambda x: jnp.where(strict, x[:chunk], 0.0), x_b)
    a_rb = _each(lambda x: jnp.where(incl, x[chunk:], 0.0), x_b)
    a_ak = _each(lambda x: jnp.where(strict, x[:chunk], 0.0), x_k)
    a_rk = _each(lambda x: jnp.where(incl, x[chunk:], 0.0), x_k)
    t_inv = _unit_tri_inverse(a_ab, row, col, chunk, pack)

    w_t = _each(bdot, t_inv, alpha)
    u0 = _each(bdot, t_inv, _each(bdot, a_ak, v))
    y0 = _each(bdot, a_rk, v)
    ktv = _each(_dot_tn, kt, v)
    u = _each(lambda x, y: x + y, _each(_dot, w_t, st0_b), u0)
    y1 = _each(_dot, rt, st0_b)
    y2 = _each(bdot, a_rb, u)
    btu = _each(_dot_tn, beta, u)
    p_col = [jnp.transpose(pl_ref[:, sl])[:, :1] for sl in sls]
    own = (lax.broadcasted_iota(jnp.int32, (width, width), 0) // hd
           == lax.broadcasted_iota(jnp.int32, (width, width), 1) // hd)
    for g in range(groups):
        y_ref[0, :, sls[g]] = y0[g] + y1[g] - y2[g]
        st_ref[g] = (st0[g] + jnp.where(own, ktv[g] - btu[g], 0.0)) * p_col[g]

    @pl.when(c == nc - 1)
    def _():
        sf_ref[0] = st_ref[...]


def wkv_pack(nh, hd):
    pack = LANE // hd if hd < LANE and LANE % hd == 0 and hd == WKV_CHUNK else 1
    return pack if nh % pack == 0 else 1


def wkv_state_shape(b, nh, hd):
    pack = wkv_pack(nh, hd)
    return (b, nh // pack, pack * hd, pack * hd)


def wkv(r, lw, k, v, kk, a, s0t, reverse, hd):
    b, t, width = r.shape
    nh = width // hd
    pack = wkv_pack(nh, hd)
    heads = max(hh for hh in (1, 2, 4, 8, 16, 32) if nh % hh == 0 and hh <= WKV_HEADS and hh % pack == 0)
    chunk = WKV_CHUNK
    nc = t // chunk
    blk = heads * hd
    groups = heads // pack
    gw = pack * hd

    def tok_map(bi, hi, ci):
        return (bi, nc - 1 - ci if reverse else ci, hi)

    tok_spec = pl.BlockSpec((1, chunk, blk), tok_map)
    st_spec = pl.BlockSpec((1, groups, gw, gw), lambda bi, hi, ci: (bi, hi, 0, 0))
    return pl.pallas_call(
        functools.partial(_wkv_kernel, chunk=chunk, hd=hd, heads=heads, pack=pack, reverse=reverse, nc=nc),
        grid=(b, nh // heads, nc),
        in_specs=[tok_spec] * 6 + [st_spec],
        out_specs=[tok_spec, st_spec],
        out_shape=[jax.ShapeDtypeStruct((b, t, width), F32),
                   jax.ShapeDtypeStruct(wkv_state_shape(b, nh, hd), F32)],
        scratch_shapes=[pltpu.VMEM((groups, gw, gw), F32)]
        + [pltpu.VMEM((chunk, blk), BF16)] * 5 + [pltpu.VMEM((8, blk), F32)],
        compiler_params=pltpu.CompilerParams(
            dimension_semantics=("parallel", "parallel", "arbitrary"),
            vmem_limit_bytes=VMEM_LIMIT),
        name="wkv_rev" if reverse else "wkv_fwd",
    )(r, lw, k, v, kk, a, s0t)


def _from_prev(x):
    return pltpu.roll(x, 1, axis=0)


def _from_next(x):
    return pltpu.roll(x, x.shape[0] - 1, axis=0)


def _token_shift(x, up, dn, has_up, has_dn, mode):
    rows, ch = x.shape
    if mode == "seq":
        half = ch // 2
        t = lax.broadcasted_iota(jnp.int32, (rows, half), 0)
        s0 = jnp.where(t == 0, 0.0, _from_prev(x[:, :half]))
        s1 = jnp.where(t == rows - 1, 0.0, _from_next(x[:, half:]))
        return [s0, s1]
    q = ch // 4
    t = lax.broadcasted_iota(jnp.int32, (rows, q), 0) % GRID_W
    s0 = jnp.where(t == 0, 0.0, _from_prev(x[:, :q]))
    s1 = jnp.where(t == GRID_W - 1, 0.0, _from_next(x[:, q:2 * q]))
    up = jnp.where(has_up, up, 0.0)
    dn = jnp.where(has_dn, dn, 0.0)
    if rows == GRID_W:
        s2, s3 = up, dn
    else:
        s2 = jnp.concatenate([up, x[:rows - GRID_W, 2 * q:3 * q]], axis=0)
        s3 = jnp.concatenate([x[GRID_W:, 3 * q:], dn], axis=0)
    return [s0, s1, s2, s3]


def _prep_kernel(*refs, mode, nblk):
    if mode == "grid":
        x_ref, up_ref, dn_ref, w_ref, sh_ref, sc_ref, o_ref = refs
    else:
        x_ref, w_ref, sh_ref, sc_ref, o_ref = refs
    i = pl.program_id(1)
    w, sh, sc = w_ref[...], sh_ref[0], sc_ref[0]

    def modulated(x):
        y = x * lax.rsqrt(jnp.mean(x * x, axis=-1, keepdims=True) + NORM_EPS)
        return (y * w) * (1 + sc) + sh

    h = modulated(x_ref[0])
    d = h.shape[1]
    if mode == "grid":
        q = d // 4
        up = modulated(up_ref[0])[:, 2 * q:3 * q]
        dn = modulated(dn_ref[0])[:, 3 * q:]
        parts = _token_shift(h, up, dn, i > 0, i < nblk - 1, mode)
    else:
        parts = _token_shift(h, None, None, None, None, mode)
    o_ref[:, :d] = h.astype(o_ref.dtype)
    width = d // len(parts)
    for n, s in enumerate(parts):
        lo = n * width
        o_ref[:, d + lo:d + lo + width] = (s - h[:, lo:lo + width]).astype(o_ref.dtype)


def prep(x, w, shift, scale, mode):
    b, t, d = x.shape
    rows = _pick(t, PREP_ROWS, GRID_W) if mode == "grid" else t
    nblk = t // rows
    per = rows // GRID_W
    x_spec = pl.BlockSpec((1, rows, d), lambda bi, i: (bi, i, 0))
    vec_spec = pl.BlockSpec((1, d), lambda bi, i: (0, 0))
    mod_spec = pl.BlockSpec((1, 1, d), lambda bi, i: (bi, 0, 0))
    in_specs, args = [x_spec], [x]
    if mode == "grid":
        last = t // GRID_W - 1
        in_specs += [pl.BlockSpec((1, GRID_W, d), lambda bi, i: (bi, jnp.maximum(i * per - 1, 0), 0)),
                     pl.BlockSpec((1, GRID_W, d), lambda bi, i: (bi, jnp.minimum((i + 1) * per, last), 0))]
        args += [x, x]
    in_specs += [vec_spec, mod_spec, mod_spec]
    args += [w[None, :], shift, scale]
    return pl.pallas_call(
        functools.partial(_prep_kernel, mode=mode, nblk=nblk),
        grid=(b, nblk),
        in_specs=in_specs,
        out_specs=pl.BlockSpec((rows, 2 * d), lambda bi, i: (bi * nblk + i, 0)),
        out_shape=jax.ShapeDtypeStruct((b * t, 2 * d), BF16),
        compiler_params=_params(("parallel", "parallel")),
        name="prep_" + mode,
    )(*args)


def _norm_mod_kernel(x_ref, w_ref, sh_ref, sc_ref, *rest, routed):
    x = x_ref[...]
    y = x * lax.rsqrt(jnp.mean(x * x, axis=-1, keepdims=True) + NORM_EPS)
    h = (y * w_ref[...]) * (1 + sc_ref[0]) + sh_ref[0]
    if routed:
        rw_ref, rb_ref, o_ref, lg_ref = rest
        lg_ref[...] = jnp.dot(h, rw_ref[...], precision=HI, preferred_element_type=F32) + rb_ref[...]
    else:
        (o_ref,) = rest
    o_ref[...] = h.astype(o_ref.dtype)


def norm_mod(x, w, shift, scale, rows_per_batch, router_w=None, router_b=None):
    m, d = x.shape
    rows = _pick(rows_per_batch, PREP_ROWS, 8)
    nblk = rows_per_batch // rows
    routed = router_w is not None
    x_spec = pl.BlockSpec((rows, d), lambda bi, i: (bi * nblk + i, 0))
    mod_spec = pl.BlockSpec((1, 1, d), lambda bi, i: (bi, 0, 0))
    in_specs = [x_spec, pl.BlockSpec((1, d), lambda bi, i: (0, 0)), mod_spec, mod_spec]
    args = [x, w[None, :], shift, scale]
    out_specs, out_shape = [x_spec], [jax.ShapeDtypeStruct((m, d), BF16)]
    if routed:
        n_e = router_w.shape[1]
        pad = -n_e % LANE
        in_specs += [pl.BlockSpec((d, n_e + pad), lambda bi, i: (0, 0)),
                     pl.BlockSpec((1, n_e + pad), lambda bi, i: (0, 0))]
        args += [jnp.pad(router_w, ((0, 0), (0, pad))), jnp.pad(router_b, (0, pad))[None, :]]
        out_specs.append(pl.BlockSpec((rows, n_e + pad), lambda bi, i: (bi * nblk + i, 0)))
        out_shape.append(jax.ShapeDtypeStruct((m, n_e + pad), F32))
    out = pl.pallas_call(
        functools.partial(_norm_mod_kernel, routed=routed),
        grid=(m // rows_per_batch, nblk),
        in_specs=in_specs,
        out_specs=out_specs,
        out_shape=out_shape,
        compiler_params=_params(("parallel", "parallel")),
        name="norm_mod",
    )(*args)
    return (out[0], out[1][:, :router_w.shape[1]]) if routed else (out[0], None)


def _head_sums(x, hd):
    rows, width = x.shape
    nt = width // LANE
    stacked = jnp.concatenate([x[:, i * LANE:(i + 1) * LANE] for i in range(nt)], axis=0)
    li = lax.broadcasted_iota(jnp.int32, (LANE, LANE), 0) // hd
    lj = lax.broadcasted_iota(jnp.int32, (LANE, LANE), 1) // hd
    ones = jnp.where(li == lj, 1.0, 0.0).astype(BF16)
    hi = _bf(stacked)
    rem = stacked - hi.astype(F32)
    mid = _bf(rem)
    lo = _bf(rem - mid.astype(F32))
    s = (jnp.dot(hi, ones, preferred_element_type=F32) + jnp.dot(mid, ones, preferred_element_type=F32)
         + jnp.dot(lo, ones, preferred_element_type=F32))
    return jnp.concatenate([s[i * rows:(i + 1) * rows] for i in range(nt)], axis=1)


def _wkv_prep_kernel(*refs, mode, nblk, hd, offs, has_vres):
    refs = list(refs)
    main = [refs.pop(0) for _ in range(3)]
    halo = [(refs.pop(0), refs.pop(0)) for _ in range(3)] if mode == "grid" else [(None, None)] * 3
    low_ref = refs.pop(0)
    vf_ref = refs.pop(0) if has_vres else None
    mu_ref, w0_ref, a0_ref, kk_ref, ka_ref, w2_ref, a2_ref = [refs.pop(0) for _ in range(7)]
    if has_vres:
        v0_ref, v2_ref = refs.pop(0), refs.pop(0)
    r_o, v_o, kk_o, kdf_o, kdr_o, af_o, ar_o, lwf_o, lwr_o = refs
    c = pl.program_id(1)
    low = low_ref[0]

    def lerp(n):
        x = main[n][0].astype(F32)
        up, dn = ((halo[n][0][0].astype(F32), halo[n][1][0].astype(F32)) if mode == "grid"
                  else (None, None))
        sh = jnp.concatenate(_token_shift(x, up, dn, c > 0, c < nblk - 1, mode), axis=1)
        return x + (sh - x) * mu_ref[n:n + 1, :]

    def low_dot(lo, hi_, w, act=None):
        z = low[:, lo:hi_]
        if act is not None:
            z = act(z)
        return jnp.dot(_bf(z), w, preferred_element_type=F32)

    r = lerp(0)
    k = lerp(1)
    v = lerp(2)
    if has_vres:
        gate = jax.nn.sigmoid(v0_ref[...] + low_dot(offs[6], offs[7], v2_ref[...]))
        v = v + (vf_ref[0].astype(F32) - v) * gate
    kq = k * kk_ref[...]
    kk = kq * lax.rsqrt(jnp.maximum(_head_sums(kq * kq, hd), 1e-24))
    r_o[0] = r.astype(r_o.dtype)
    v_o[0] = v.astype(v_o.dtype)
    kk_o[0] = kk.astype(kk_o.dtype)
    for di, (kd_o, a_o, lw_o) in enumerate(((kdf_o, af_o, lwf_o), (kdr_o, ar_o, lwr_o))):
        wl = w0_ref[di:di + 1, :] + low_dot(offs[di], offs[di + 1], w2_ref[di], jnp.tanh)
        softplus = jnp.maximum(-wl, 0.0) + jnp.log1p(jnp.exp(-jnp.abs(wl)))
        lw_o[0] = -jnp.exp(-softplus - DECAY_OFFSET)
        a = jax.nn.sigmoid(a0_ref[di:di + 1, :] + low_dot(offs[2 + di], offs[3 + di], a2_ref[di]))
        a_o[0] = a.astype(a_o.dtype)
        kd_o[0] = (k * (1.0 + (a - 1.0) * ka_ref[...])).astype(kd_o.dtype)


def wkv_prep(p, low, v_first, mu, w0, a0, k_k, k_a, w2, a2, v0, v2, offs, hd, mode):
    b, t, rw3 = p.shape
    rw = rw3 // 3
    q = rw // 4
    rows = GRID_W if mode == "grid" else t
    nblk = t // rows
    has_vres = v_first is not None

    def tok(width, col):
        return pl.BlockSpec((1, rows, width), lambda bi, ci: (bi, ci, col))

    def full(arr):
        nd = arr.ndim
        return pl.BlockSpec(arr.shape, lambda bi, ci: (0,) * nd)

    in_specs = [tok(rw, n) for n in range(3)]
    args = [p, p, p]
    if mode == "grid":
        for n in range(3):
            in_specs += [pl.BlockSpec((1, rows, q), lambda bi, ci, n=n: (bi, jnp.maximum(ci - 1, 0), 4 * n + 2)),
                         pl.BlockSpec((1, rows, q), lambda bi, ci, n=n: (bi, jnp.minimum(ci + 1, nblk - 1), 4 * n + 3))]
            args += [p, p]
    in_specs.append(tok(low.shape[-1], 0))
    args.append(low)
    if has_vres:
        in_specs.append(tok(rw, 0))
        args.append(v_first)
    small = [mu, w0, a0, k_k[None, :], k_a[None, :], w2.astype(BF16), a2.astype(BF16)]
    if has_vres:
        small += [v0[None, :], v2.astype(BF16)]
    in_specs += [full(s) for s in small]
    args += small
    out_spec = tok(rw, 0)
    shapes = [jax.ShapeDtypeStruct((b, t, rw), BF16)] * 7 + [jax.ShapeDtypeStruct((b, t, rw), F32)] * 2
    return pl.pallas_call(
        functools.partial(_wkv_prep_kernel, mode=mode, nblk=nblk, hd=hd, offs=tuple(offs), has_vres=has_vres),
        grid=(b, nblk),
        in_specs=in_specs,
        out_specs=[out_spec] * 9,
        out_shape=shapes,
        compiler_params=_params(("parallel", "parallel")),
        name="wkv_prep_" + mode,
    )(*args)


def _wkv_post_kernel(yf_ref, yr_ref, r_ref, kdf_ref, kdr_ref, v_ref, lg_ref, rk_ref, gw_ref, gb_ref,
                     g2_ref, o_ref, *, hd):
    y = yf_ref[0] + yr_ref[0]
    mean = _head_sums(y, hd) * (1.0 / hd)
    yc = y - mean
    var = _head_sums(yc * yc, hd) * (1.0 / hd)
    y = yc * lax.rsqrt(var + GN_EPS) * gw_ref[...] + gb_ref[...]
    rk = r_ref[0].astype(F32) * (kdf_ref[0].astype(F32) + kdr_ref[0].astype(F32)) * rk_ref[...]
    y = y + _head_sums(rk, hd) * v_ref[0].astype(F32)
    g = jnp.dot(_bf(jax.nn.sigmoid(lg_ref[0])), g2_ref[...], preferred_element_type=F32)
    o_ref[0] = (y * g).astype(o_ref.dtype)


def wkv_post(y_f, y_r, r, kd_f, kd_r, v, low_g, r_k, gn_w, gn_b, g2, hd):
    b, t, rw = y_f.shape
    rows = _pick(t, POST_ROWS, 8)

    def tok(width):
        return pl.BlockSpec((1, rows, width), lambda bi, ci: (bi, ci, 0))

    def full(arr):
        return pl.BlockSpec(arr.shape, lambda bi, ci: (0, 0))

    small = [r_k.reshape(1, rw), gn_w[None, :], gn_b[None, :], g2.astype(BF16)]
    return pl.pallas_call(
        functools.partial(_wkv_post_kernel, hd=hd),
        grid=(b, t // rows),
        in_specs=[tok(rw)] * 6 + [tok(low_g.shape[-1])] + [full(s) for s in small],
        out_specs=tok(rw),
        out_shape=jax.ShapeDtypeStruct((b, t, rw), BF16),
        compiler_params=_params(("parallel", "parallel")),
        name="wkv_post",
    )(y_f, y_r, r, kd_f, kd_r, v, low_g, *small)


def rmsnorm(x, w):
    y = x * lax.rsqrt(jnp.mean(x * x, axis=-1, keepdims=True) + NORM_EPS)
    return y * w


def modulate(x, w, shift, scale):
    return rmsnorm(x, w) * (1 + scale) + shift


def dft_tables(rows, cols, n, scale):
    j = lax.broadcasted_iota(jnp.int32, (rows, cols), 0)
    k = lax.broadcasted_iota(jnp.int32, (rows, cols), 1)
    ang = ((j * k) % n).astype(F32) * (2.0 * math.pi / n)
    return jnp.cos(ang) * scale, jnp.sin(ang) * scale


def _dft_stage_kernel(l_ref, zr_ref, zi_ref, *rest, twiddle):
    z = jnp.concatenate([zr_ref[...], zi_ref[...]], axis=0)
    y = jnp.dot(l_ref[...], z, preferred_element_type=F32)
    if not twiddle:
        rest[0][...] = y.astype(rest[0].dtype)
        return
    twr_ref, twi_ref, o_ref = rest
    half = y.shape[0] // 2
    reps = y.shape[1] // LANE
    tr, ti = jnp.tile(twr_ref[...], (1, reps)), jnp.tile(twi_ref[...], (1, reps))
    yr, yi = y[:half], y[half:]
    o_ref[0] = (yr * tr - yi * ti).astype(o_ref.dtype)
    o_ref[1] = (yr * ti + yi * tr).astype(o_ref.dtype)


class PosDft:
    def __init__(self, n):
        self.n = n
        self.n1 = DFT_N1 if n % DFT_N1 == 0 and n // DFT_N1 >= 8 and (n // DFT_N1) % 8 == 0 else 0
        if not self.n1:
            c, s = dft_tables(n, n, n, n ** -0.5)
            self.direct = jnp.concatenate([c, s], axis=1).astype(BF16)
            return
        n1, n2 = self.n1, n // self.n1
        self.n2 = n2
        c1, s1 = dft_tables(n1, n1, n1, n1 ** -0.5)
        self.l1 = jnp.concatenate([jnp.concatenate([c1, s1], axis=1),
                                   jnp.concatenate([-s1, c1], axis=1)], axis=0).astype(BF16)
        tc, ts = dft_tables(n2, n1, n, 1.0)
        self.twr = jnp.broadcast_to(tc[:, :, None], (n2, n1, LANE))
        self.twi = jnp.broadcast_to(-ts[:, :, None], (n2, n1, LANE))
        c2, s2 = dft_tables(n2, n2, n2, n2 ** -0.5)
        self.l3 = jnp.concatenate([c2, s2], axis=1).astype(BF16)


def fourier_mix(u, pos, chan_table):
    b, n, fw = u.shape
    z = matmul(u.reshape(b * n, fw).astype(BF16), chan_table, BF16, planes=2)
    if not pos.n1:
        z = z.reshape(2, b, n, fw).transpose(1, 0, 2, 3).reshape(b, 2 * n, fw)
        return jnp.stack([matmul(pos.direct, z[i], BF16) for i in range(b)], axis=0)
    n1, n2 = pos.n1, pos.n2
    z = z.reshape(2, b, n1, n2 * fw)
    y = pl.pallas_call(
        functools.partial(_dft_stage_kernel, twiddle=True),
        grid=(b, n2),
        in_specs=[pl.BlockSpec((2 * n1, 2 * n1), lambda bi, j: (0, 0)),
                  pl.BlockSpec((None, None, n1, fw), lambda bi, j: (0, bi, 0, j)),
                  pl.BlockSpec((None, None, n1, fw), lambda bi, j: (1, bi, 0, j)),
                  pl.BlockSpec((None, n1, LANE), lambda bi, j: (j, 0, 0)),
                  pl.BlockSpec((None, n1, LANE), lambda bi, j: (j, 0, 0))],
        out_specs=pl.BlockSpec((None, 2, None, n1, fw), lambda bi, j: (bi, 0, j, 0, 0)),
        out_shape=jax.ShapeDtypeStruct((b, 2, n2, n1, fw), BF16),
        compiler_params=_params(("parallel", "parallel")),
        name="dft_stage1",
    )(pos.l1, z, z, pos.twr, pos.twi)
    y = y.reshape(b, 2, n2, n1 * fw)
    tn = _pick(n1 * fw, 4096, LANE)
    out = pl.pallas_call(
        functools.partial(_dft_stage_kernel, twiddle=False),
        grid=(b, n1 * fw // tn),
        in_specs=[pl.BlockSpec((n2, 2 * n2), lambda bi, j: (0, 0)),
                  pl.BlockSpec((None, None, n2, tn), lambda bi, j: (bi, 0, 0, j)),
                  pl.BlockSpec((None, None, n2, tn), lambda bi, j: (bi, 1, 0, j))],
        out_specs=pl.BlockSpec((None, n2, tn), lambda bi, j: (bi, 0, j)),
        out_shape=jax.ShapeDtypeStruct((b, n2, n1 * fw), BF16),
        compiler_params=_params(("parallel", "parallel")),
        name="dft_stage2",
    )(pos.l3, y, y)
    return out.reshape(b, n, fw)


def depthwise_conv(u, w):
    n = u.shape[1]
    kk = w.shape[0]
    pad = kk // 2
    up = jnp.pad(u, ((0, 0), (pad, pad), (0, 0)))
    out = up[:, 0:n] * w[0]
    for i in range(1, kk):
        out = out + up[:, i:i + n] * w[i]
    return out


def kernel(x, c, ctx, c_ctx, ada_a, ada_b, ada_bias, norm1_w, norm2_w, w_in, mu_rkv, mu_lr, decay_w0, decay_w1, decay_w2, iclr_a0, iclr_a1, iclr_a2, ogate_g1, ogate_g2, k_k, k_a, r_k, gn_w, gn_b, vres_mu, vres_v0, vres_v1, vres_v2, conv_w, gate_w1, gate_w2, gate_b, proj_rwkv, proj_fourier, proj_conv, w_out, ffn_w1, ffn_w3, ffn_w2, router_w, router_b, moe_w1, moe_w3, moe_w2, final_norm_w):
    depth = w_in.shape[0]
    bsz, seq, d = x.shape
    ctx_len = ctx.shape[1]
    rw = mu_rkv.shape[-1]
    nh, hd = r_k.shape[1], r_k.shape[2]
    fw = proj_fourier.shape[1]
    cw = proj_conv.shape[1]
    four_off = 3 * rw
    conv_off = four_off + fw
    n_exp = router_w.shape[-1]
    d_exp = moe_w1.shape[-1]
    n_mod = ada_b.shape[-1] // d
    r_dec, r_icl, r_og, r_vr, r_gate = (decay_w1.shape[-1], iclr_a1.shape[-1], ogate_g1.shape[-1],
                                        vres_v1.shape[-1], gate_w1.shape[-1])

    gc = fw // FOURIER_GROUPS
    cc, cs = dft_tables(gc, gc, gc, gc ** -0.5)
    eye_g = jnp.eye(FOURIER_GROUPS, dtype=F32)
    chan_table = jnp.concatenate([jnp.kron(eye_g, cc), -jnp.kron(eye_g, cs)], axis=1).astype(BF16)
    pos_lat, pos_ctx = PosDft(seq), PosDft(ctx_len)

    w_in_b, w_out_b = w_in.astype(BF16), w_out.astype(BF16)
    proj_r_b, proj_f_b, proj_c_b = proj_rwkv.astype(BF16), proj_fourier.astype(BF16), proj_conv.astype(BF16)
    gate_w2_b, gate_b3 = gate_w2.astype(BF16), gate_b[:, None, :]
    ffn_w1_b, ffn_w3_b, ffn_w2_b = ffn_w1.astype(BF16), ffn_w3.astype(BF16), ffn_w2.astype(BF16)
    moe_w1_b = moe_w1.astype(BF16).reshape((-1,) + moe_w1.shape[2:])
    moe_w3_b = moe_w3.astype(BF16).reshape((-1,) + moe_w3.shape[2:])
    moe_w2_b = moe_w2.astype(BF16).reshape(moe_w2.shape[0], n_exp * d_exp, d)

    cond_lat = jax.nn.silu(c)
    cond_ctx = jax.nn.silu(c_ctx)[None, :]
    cond = jnp.concatenate([cond_lat, cond_ctx], axis=0)
    cond = jnp.pad(cond, ((0, 16 - cond.shape[0] % 16), (0, 0)))

    def tokens(xs, mod, l, mode, pos_tab, s0, v_first, last_ctx):
        b, t, _ = xs.shape
        has_vres = l > 0
        hd_cat = prep(xs, norm1_w[l], jnp.broadcast_to(mod[0], (b, 1, d)),
                      jnp.broadcast_to(mod[1], (b, 1, d)), mode)

        w_h = [decay_w1[l, 0], decay_w1[l, 1], iclr_a1[l, 0], iclr_a1[l, 1], ogate_g1[l], gate_w1[l]]
        mus = [mu_lr[l, 0], mu_lr[l, 0], mu_lr[l, 1], mu_lr[l, 1], mu_lr[l, 2], None]
        if has_vres:
            w_h.append(vres_v1[l - 1])
            mus.append(vres_mu[l - 1])
        w_dh = [jnp.zeros_like(w) if m is None else w * m[:, None] for w, m in zip(w_h, mus)]
        w_low = jnp.concatenate([jnp.concatenate(w_h, axis=1), jnp.concatenate(w_dh, axis=1)], axis=0)
        n_low = w_low.shape[1]
        w_low = jnp.pad(w_low, ((0, 0), (0, -n_low % LANE)))
        low = matmul(hd_cat, w_low.astype(BF16)).reshape(b, t, -1)
        offs = [0]
        for w in w_h:
            offs.append(offs[-1] + w.shape[1])
        low_g = low[..., offs[4]:offs[5]]
        low_gate = low[..., offs[5]:offs[6]]

        p = matmul(hd_cat, w_in_b, BF16, n_cols=four_off, layer=l).reshape(b, t, -1)
        p_fc = matmul(hd_cat, w_in_b, BF16, n_cols=w_in_b.shape[-1] - four_off,
                      col_off=four_off, layer=l).reshape(b, t, -1)

        r, v, kk, kd_f, kd_r, a_f, a_r, lw_f, lw_r = wkv_prep(
            p, low, v_first, mu_rkv[l], decay_w0[l], iclr_a0[l], k_k[l], k_a[l], decay_w2[l], iclr_a2[l],
            vres_v0[l - 1] if has_vres else None, vres_v2[l - 1] if has_vres else None, offs, hd, mode)

        ys, states = [], []
        for di, (reverse, lw, kd, a) in enumerate(((False, lw_f, kd_f, a_f), (True, lw_r, kd_r, a_r))):
            init = jnp.zeros(wkv_state_shape(b, nh, hd), F32) if s0 is None else s0[di]
            y_d, s_d = wkv(r, lw, kd, v, kk, a, init, reverse, hd)
            ys.append(y_d)
            states.append(s_d)
        if last_ctx:
            return None, states, v

        y_rwkv = wkv_post(ys[0], ys[1], r, kd_f, kd_r, v, low_g, r_k[l], gn_w[l], gn_b[l],
                          ogate_g2[l], hd)

        y_four = fourier_mix(p_fc[..., :fw], pos_tab, chan_table)
        gate_b_ = p_fc[..., fw:fw + cw].astype(F32)
        gate_c_ = p_fc[..., fw + cw:fw + 2 * cw].astype(F32)
        uu = p_fc[..., fw + 2 * cw:].astype(F32)
        y_conv = gate_b_ * depthwise_conv(gate_c_ * uu, conv_w[l])

        rows = b * t

        def flat(z):
            return z.reshape(rows, -1).astype(BF16)

        merged = gated_merge(flat(y_rwkv), flat(y_four), flat(y_conv), flat(low_gate),
                             proj_r_b, proj_f_b, proj_c_b, gate_w2_b, gate_b3, l)
        xs = resid_matmul(merged, w_out_b, xs.reshape(rows, d),
                          jnp.broadcast_to(mod[2], (b, 1, d)), t, l)

        j = l // 2
        dense = l % 2 == 0
        h2, logits = norm_mod(xs, norm2_w[l], jnp.broadcast_to(mod[3], (b, 1, d)),
                              jnp.broadcast_to(mod[4], (b, 1, d)), t,
                              None if dense else router_w[j], None if dense else router_b[j])
        if dense:
            hid = swiglu_up(h2, ffn_w1_b, ffn_w3_b, e0=j, n_e=1)
            w2 = ffn_w2_b
        else:
            top_val, top_idx = lax.top_k(logits, TOP_K)
            weights = jax.nn.softmax(top_val, axis=-1)
            gate = jnp.sum(jax.nn.one_hot(top_idx, n_exp, dtype=F32) * weights[..., None], axis=-2)
            gate_rep = jnp.repeat(gate, LANE, axis=-1)
            hid = swiglu_up(h2, moe_w1_b, moe_w3_b, gate_rep, e0=j * n_exp, n_e=n_exp)
            w2 = moe_w2_b
        xs = resid_matmul(hid, w2, xs, jnp.broadcast_to(mod[5], (b, 1, d)), t, j)
        return xs.reshape(b, t, d), states, v

    xl, xc = x, ctx
    v_first_l = v_first_c = None
    for l in range(depth):
        last = l == depth - 1
        m = mm(mm(cond, ada_a[l]), ada_b[l]) + ada_bias[l]
        mod_l = [m[:bsz, None, i * d:(i + 1) * d] for i in range(n_mod)]
        mod_c = [m[bsz:bsz + 1, None, i * d:(i + 1) * d] for i in range(n_mod)]

        xc_new, ctx_states, vc = tokens(xc, mod_c, l, "seq", pos_ctx, None, v_first_c, last)
        xl, _, vl = tokens(xl, mod_l, l, "grid", pos_lat, ctx_states, v_first_l, False)
        if l == 0:
            v_first_c, v_first_l = vc, vl
        if not last:
            xc = xc_new
    return rmsnorm(xl, final_norm_w)
```

```python
import functools
import math

import jax
import jax.numpy as jnp
from jax import lax
from jax.experimental import pallas as pl
from jax.experimental.pallas import tpu as pltpu

F32 = jnp.float32
BF16 = jnp.bfloat16

GRID_W = 64
FOURIER_GROUPS = 8
TOP_K = 2
NORM_EPS = 1e-6
GN_EPS = 64e-5
DECAY_OFFSET = 0.5
WKV_CHUNK = 64
WKV_HEADS = 32
DFT_N1 = 128
PREP_ROWS = 256
POST_ROWS = 256
LANE = 128
VMEM_LIMIT = 56 * 1024 * 1024

HI = lax.Precision.HIGHEST


def _pick(dim, target, align):
    if dim <= target:
        return dim
    t = (target // align) * align
    while t >= align:
        if dim % t == 0:
            return t
        t -= align
    return dim


def _mm_kernel(a_ref, b_ref, o_ref, acc_ref, *, nk):
    k = pl.program_id(2)

    @pl.when(k == 0)
    def _():
        acc_ref[...] = jnp.zeros_like(acc_ref)

    acc_ref[...] += jnp.dot(a_ref[...], b_ref[...], preferred_element_type=F32)

    @pl.when(k == nk - 1)
    def _():
        o_ref[...] = acc_ref[...].astype(o_ref.dtype)


def matmul(a, b, out_dtype=F32, tm=1024, tn=1024, tk=2048, n_cols=None, col_off=0, planes=1, layer=None):
    m = a.shape[0]
    k = b.shape[-2]
    n = b.shape[-1] if n_cols is None else n_cols
    tm = _pick(m, tm, 16)
    tn = _pick(math.gcd(n // planes, col_off) if col_off else n // planes, tn, LANE)
    tk = _pick(k, tk, LANE)
    nk = k // tk
    joff = col_off // tn
    if planes == 1:
        out_spec = pl.BlockSpec((tm, tn), lambda i, j, kk: (i, j))
        out_shape = jax.ShapeDtypeStruct((m, n), out_dtype)
    else:
        per = n // planes // tn
        out_spec = pl.BlockSpec((None, tm, tn), lambda i, j, kk: (j // per, i, j % per))
        out_shape = jax.ShapeDtypeStruct((planes, m, n // planes), out_dtype)
    if layer is None:
        b_spec = pl.BlockSpec((tk, tn), lambda i, j, kk: (kk, j + joff))
    else:
        b_spec = pl.BlockSpec((None, tk, tn), lambda i, j, kk: (layer, kk, j + joff))
    return pl.pallas_call(
        functools.partial(_mm_kernel, nk=nk),
        grid=(m // tm, n // tn, nk),
        in_specs=[pl.BlockSpec((tm, tk), lambda i, j, kk: (i, kk)), b_spec],
        out_specs=out_spec,
        out_shape=out_shape,
        scratch_shapes=[pltpu.VMEM((tm, tn), F32)],
        compiler_params=pltpu.CompilerParams(
            dimension_semantics=("parallel", "parallel", "arbitrary"),
            vmem_limit_bytes=VMEM_LIMIT),
        name="matmul",
    )(a, b)


def mm(a, b, out_dtype=F32, **kw):
    lead = a.shape[:-1]
    out = matmul(a.reshape(-1, a.shape[-1]).astype(BF16), b.astype(BF16), out_dtype, **kw)
    return out.reshape(lead + (b.shape[-1],))


def _params(sem):
    return pltpu.CompilerParams(dimension_semantics=sem, vmem_limit_bytes=VMEM_LIMIT)


def _swiglu_up_kernel(a_ref, w1_ref, w3_ref, *rest, nk, gated):
    if gated:
        g_ref, o_ref, acc1_ref, acc3_ref = rest
    else:
        o_ref, acc1_ref, acc3_ref = rest
    k = pl.program_id(2)

    @pl.when(k == 0)
    def _():
        acc1_ref[...] = jnp.zeros_like(acc1_ref)
        acc3_ref[...] = jnp.zeros_like(acc3_ref)

    a = a_ref[...]
    acc1_ref[...] += jnp.dot(a, w1_ref[...], preferred_element_type=F32)
    acc3_ref[...] += jnp.dot(a, w3_ref[...], preferred_element_type=F32)

    @pl.when(k == nk - 1)
    def _():
        h1 = acc1_ref[...]
        hid = h1 * jax.nn.sigmoid(h1) * acc3_ref[...]
        if gated:
            hid = hid * jnp.tile(g_ref[...], (1, hid.shape[1] // LANE))
        o_ref[...] = hid.astype(o_ref.dtype)


def swiglu_up(a, w1, w3, gate_rep=None, e0=0, n_e=None):
    m, k = a.shape
    f = w1.shape[-1]
    n_e = w1.shape[0] if n_e is None else n_e
    n = n_e * f
    tm = _pick(m, 1024, 16)
    tk = _pick(k, 2048, LANE)
    tn = _pick(f, 1024, LANE)
    gated = gate_rep is not None
    nk = k // tk
    per = f // tn
    w_spec = pl.BlockSpec((None, tk, tn), lambda i, j, kk: (e0 + j // per, kk, j % per))
    in_specs = [pl.BlockSpec((tm, tk), lambda i, j, kk: (i, kk)), w_spec, w_spec]
    args = [a, w1, w3]
    if gated:
        in_specs.append(pl.BlockSpec((tm, LANE), lambda i, j, kk: (i, j // per)))
        args.append(gate_rep)
    return pl.pallas_call(
        functools.partial(_swiglu_up_kernel, nk=nk, gated=gated),
        grid=(m // tm, n // tn, nk),
        in_specs=in_specs,
        out_specs=pl.BlockSpec((tm, tn), lambda i, j, kk: (i, j)),
        out_shape=jax.ShapeDtypeStruct((m, n), BF16),
        scratch_shapes=[pltpu.VMEM((tm, tn), F32), pltpu.VMEM((tm, tn), F32)],
        compiler_params=_params(("parallel", "parallel", "arbitrary")),
        name="swiglu_up",
    )(*args)


def _resid_kernel(a_ref, w_ref, x_ref, g_ref, o_ref, acc_ref, *, nk):
    k = pl.program_id(2)

    @pl.when(k == 0)
    def _():
        acc_ref[...] = jnp.zeros_like(acc_ref)

    acc_ref[...] += jnp.dot(a_ref[...], w_ref[...], preferred_element_type=F32)

    @pl.when(k == nk - 1)
    def _():
        o_ref[...] = x_ref[...] + g_ref[0] * acc_ref[...]


def resid_matmul(a, w, x, g, rows_per_batch, layer):
    m, k = a.shape
    n = w.shape[-1]
    tm = _pick(rows_per_batch, 1024, 16)
    tn = _pick(n, 1024, LANE)
    tk = _pick(k, 2048, LANE)
    nk = k // tk
    per = rows_per_batch // tm
    return pl.pallas_call(
        functools.partial(_resid_kernel, nk=nk),
        grid=(m // tm, n // tn, nk),
        in_specs=[pl.BlockSpec((tm, tk), lambda i, j, kk: (i, kk)),
                  pl.BlockSpec((None, tk, tn), lambda i, j, kk: (layer, kk, j)),
                  pl.BlockSpec((tm, tn), lambda i, j, kk: (i, j)),
                  pl.BlockSpec((1, 1, tn), lambda i, j, kk: (i // per, 0, j))],
        out_specs=pl.BlockSpec((tm, tn), lambda i, j, kk: (i, j)),
        out_shape=jax.ShapeDtypeStruct((m, n), F32),
        scratch_shapes=[pltpu.VMEM((tm, tn), F32)],
        compiler_params=_params(("parallel", "parallel", "arbitrary")),
        name="resid_matmul",
    )(a, w, x, g)


def _merge_kernel(yr_ref, yf_ref, yc_ref, lg_ref, pr_ref, pf_ref, pc_ref,
                  gwr_ref, gwf_ref, gwc_ref, gbr_ref, gbf_ref, gbc_ref, o_ref):
    lg = lg_ref[...]

    def branch(y_ref, p_ref, gw_ref, gb_ref):
        gate = jax.nn.sigmoid(jnp.dot(lg, gw_ref[...], preferred_element_type=F32) + gb_ref[...])
        return gate * jnp.dot(y_ref[...], p_ref[...], preferred_element_type=F32)

    out = (branch(yr_ref, pr_ref, gwr_ref, gbr_ref) + branch(yf_ref, pf_ref, gwf_ref, gbf_ref)
           + branch(yc_ref, pc_ref, gwc_ref, gbc_ref))
    o_ref[...] = out.astype(o_ref.dtype)


def gated_merge(yr, yf, yc, lg, pr, pf, pc, gw2, gb, layer):
    m = yr.shape[0]
    d = pr.shape[-1]
    tm = _pick(m, 1024, 16)
    tn = _pick(d, 512, LANE)
    nj = d // tn

    def rows(arr):
        return pl.BlockSpec((tm, arr.shape[1]), lambda i, j: (i, 0))

    def cols(arr, off):
        return pl.BlockSpec((None, arr.shape[1], tn), lambda i, j: (layer, 0, off * nj + j))

    return pl.pallas_call(
        _merge_kernel,
        grid=(m // tm, nj),
        in_specs=[rows(yr), rows(yf), rows(yc), rows(lg), cols(pr, 0), cols(pf, 0), cols(pc, 0),
                  cols(gw2, 0), cols(gw2, 1), cols(gw2, 2), cols(gb, 0), cols(gb, 1), cols(gb, 2)],
        out_specs=pl.BlockSpec((tm, tn), lambda i, j: (i, j)),
        out_shape=jax.ShapeDtypeStruct((m, d), BF16),
        compiler_params=_params(("parallel", "parallel")),
        name="gated_merge",
    )(yr, yf, yc, lg, pr, pf, pc, gw2, gw2, gw2, gb, gb, gb)


def _bf(x):
    return x.astype(BF16)


def _dot(a, b):
    return jnp.dot(_bf(a), _bf(b), preferred_element_type=F32)


def _dot_nt(a, b):
    return lax.dot_general(_bf(a), _bf(b), (((1,), (1,)), ((), ())), preferred_element_type=F32)


def _dot_tn(a, b):
    return lax.dot_general(_bf(a), _bf(b), (((0,), (0,)), ((), ())), preferred_element_type=F32)


def _each(f, *lists):
    return [f(*xs) for xs in zip(*lists)]


def _block_diag(x, width, pack):
    if pack == 1:
        return x
    lane_g = lax.broadcasted_iota(jnp.int32, x.shape, 1) // width
    zero = jnp.zeros_like(x)
    return jnp.concatenate([jnp.where(lane_g == g, x, zero) for g in range(pack)], axis=0)


def _unit_tri_inverse(a_tri, row, col, size, pack):
    def same_block(shift):
        return (row >> shift) == (col >> shift)

    def idot(a, b):
        return _dot(a, _block_diag(_bf(b), size, pack))

    eye = (row == col).astype(F32)
    a8 = _each(lambda a: jnp.where(same_block(3), a, 0.0), a_tri)
    a8_2 = _each(idot, a8, a8)
    a8_4 = _each(idot, a8_2, a8_2)
    x = _each(lambda a: eye - a, a8)
    x = _each(lambda xx, d: xx + d, x, _each(idot, x, a8_2))
    x = _each(lambda xx, d: xx + d, x, _each(idot, x, a8_4))
    shift = 3
    while (1 << shift) < size:
        off = same_block(shift + 1) & jnp.logical_not(same_block(shift))
        e = _each(lambda a: jnp.where(off, a, 0.0), a_tri)
        ex = _each(idot, e, x)
        x = _each(lambda xx, d: xx - d, x, _each(idot, x, ex))
        shift += 1
    return x


def _wkv_kernel(*refs, chunk, hd, heads, pack, dirs, nc):
    nd = len(dirs)
    ins = [refs[6 * d:6 * d + 6] for d in range(nd)]
    s0_refs = refs[6 * nd:7 * nd]
    y_refs = refs[7 * nd:8 * nd]
    sf_refs = refs[8 * nd:9 * nd]
    st_ref, al_ref, be_ref, rt_ref, kt_ref, vb_ref, pl_ref = refs[9 * nd:]
    c = pl.program_id(2)

    @pl.when(c == 0)
    def _():
        for d in range(nd):
            st_ref[d] = s0_refs[d][0]

    row = lax.broadcasted_iota(jnp.int32, (chunk, pack * chunk), 0)
    col = lax.broadcasted_iota(jnp.int32, (chunk, pack * chunk), 1) % chunk
    width = pack * hd
    groups = heads // pack
    sls = [slice(g * width, (g + 1) * width) for g in range(groups)]
    strict, incl = [], []
    for d, reverse in enumerate(dirs):
        st_d, in_d = (col > row, col >= row) if reverse else (col < row, col <= row)
        strict += [st_d] * groups
        incl += [in_d] * groups
        tri = jnp.where(in_d[:, :chunk], 1.0, 0.0).astype(BF16)
        last = 0 if reverse else chunk - 1
        r_ref, lw_ref, k_ref, v_ref, kk_ref, a_ref = ins[d]

        lw = lw_ref[0]
        lw_hi = _bf(lw)
        rem = lw - lw_hi.astype(F32)
        lw_mid = _bf(rem)
        lw_lo = _bf(rem - lw_mid.astype(F32))
        cum = (jnp.dot(tri, lw_hi, preferred_element_type=F32)
               + jnp.dot(tri, lw_mid, preferred_element_type=F32)
               + jnp.dot(tri, lw_lo, preferred_element_type=F32))
        p_in = jnp.exp(cum)
        p_inv = jnp.exp(-cum)
        kk = kk_ref[0].astype(F32)
        rt_ref[d] = _bf(r_ref[0].astype(F32) * p_in)
        kt_ref[d] = _bf(k_ref[0].astype(F32) * p_inv)
        be_ref[d] = _bf(a_ref[0].astype(F32) * kk * p_inv)
        al_ref[d] = _bf(kk * jnp.exp(cum - lw))
        vb_ref[d] = _bf(v_ref[0])
        pl_ref[d] = jnp.broadcast_to(p_in[last:last + 1, :], pl_ref.shape[1:])

    ent = [(d, g) for d in range(nd) for g in range(groups)]
    alpha = [al_ref[d, :, sls[g]] for d, g in ent]
    beta = [be_ref[d, :, sls[g]] for d, g in ent]
    rt = [rt_ref[d, :, sls[g]] for d, g in ent]
    kt = [kt_ref[d, :, sls[g]] for d, g in ent]
    v = [vb_ref[d, :, sls[g]] for d, g in ent]
    st0 = [st_ref[d, g] for d, g in ent]
    st0_b = _each(_bf, st0)

    def bd_keys(x):
        return _block_diag(x, hd, pack)

    def bdot(a, x):
        return _dot(a, _block_diag(_bf(x), hd, pack))

    ar = _each(lambda x, y: jnp.concatenate([x, y], axis=0), alpha, rt)
    x_b = _each(_dot_nt, ar, _each(bd_keys, beta))
    x_k = _each(_dot_nt, ar, _each(bd_keys, kt))
    a_ab = _each(lambda x, m: jnp.where(m, x[:chunk], 0.0), x_b, strict)
    a_rb = _each(lambda x, m: jnp.where(m, x[chunk:], 0.0), x_b, incl)
    a_ak = _each(lambda x, m: jnp.where(m, x[:chunk], 0.0), x_k, strict)
    a_rk = _each(lambda x, m: jnp.where(m, x[chunk:], 0.0), x_k, incl)
    t_inv = _unit_tri_inverse(a_ab, row, col, chunk, pack)

    w_t = _each(bdot, t_inv, alpha)
    u0 = _each(bdot, t_inv, _each(bdot, a_ak, v))
    y0 = _each(bdot, a_rk, v)
    ktv = _each(_dot_tn, kt, v)
    u = _each(lambda x, y: x + y, _each(_dot, w_t, st0_b), u0)
    y1 = _each(_dot, rt, st0_b)
    y2 = _each(bdot, a_rb, u)
    btu = _each(_dot_tn, beta, u)
    p_col = [jnp.transpose(pl_ref[d, :, sls[g]])[:, :1] for d, g in ent]
    own = (lax.broadcasted_iota(jnp.int32, (width, width), 0) // hd
           == lax.broadcasted_iota(jnp.int32, (width, width), 1) // hd)
    for n, (d, g) in enumerate(ent):
        y_refs[d][0, :, sls[g]] = y0[n] + y1[n] - y2[n]
        st_ref[d, g] = (st0[n] + jnp.where(own, ktv[n] - btu[n], 0.0)) * p_col[n]

    @pl.when(c == nc - 1)
    def _():
        for d in range(nd):
            sf_refs[d][0] = st_ref[d]


def wkv_pack(nh, hd):
    pack = LANE // hd if hd < LANE and LANE % hd == 0 and hd == WKV_CHUNK else 1
    return pack if nh % pack == 0 else 1


def wkv_state_shape(b, nh, hd):
    pack = wkv_pack(nh, hd)
    return (b, nh // pack, pack * hd, pack * hd)


def wkv(r, v, kk, per_dir, s0, hd, dirs=(False, True)):
    b, t, width = r.shape
    nh = width // hd
    pack = wkv_pack(nh, hd)
    heads = max(hh for hh in (1, 2, 4, 8, 16, 32) if nh % hh == 0 and hh <= WKV_HEADS and hh % pack == 0)
    chunk = WKV_CHUNK
    nc = t // chunk
    blk = heads * hd
    groups = heads // pack
    gw = pack * hd
    nd = len(dirs)

    def tok_spec(reverse):
        return pl.BlockSpec((1, chunk, blk), lambda bi, hi, ci: (bi, nc - 1 - ci if reverse else ci, hi))

    st_spec = pl.BlockSpec((1, groups, gw, gw), lambda bi, hi, ci: (bi, hi, 0, 0))
    in_specs, args = [], []
    for d, reverse in enumerate(dirs):
        lw, k, a = per_dir[d]
        in_specs += [tok_spec(reverse)] * 6
        args += [r, lw, k, v, kk, a]
    in_specs += [st_spec] * nd
    args += list(s0)
    outs = pl.pallas_call(
        functools.partial(_wkv_kernel, chunk=chunk, hd=hd, heads=heads, pack=pack, dirs=tuple(dirs), nc=nc),
        grid=(b, nh // heads, nc),
        in_specs=in_specs,
        out_specs=[tok_spec(reverse) for reverse in dirs] + [st_spec] * nd,
        out_shape=[jax.ShapeDtypeStruct((b, t, width), F32)] * nd
        + [jax.ShapeDtypeStruct(wkv_state_shape(b, nh, hd), F32)] * nd,
        scratch_shapes=[pltpu.VMEM((nd, groups, gw, gw), F32)]
        + [pltpu.VMEM((nd, chunk, blk), BF16)] * 5 + [pltpu.VMEM((nd, 8, blk), F32)],
        compiler_params=pltpu.CompilerParams(
            dimension_semantics=("parallel", "parallel", "arbitrary"),
            vmem_limit_bytes=VMEM_LIMIT),
        name="wkv",
    )(*args)
    return outs[:nd], outs[nd:]


def _from_prev(x):
    return pltpu.roll(x, 1, axis=0)


def _from_next(x):
    return pltpu.roll(x, x.shape[0] - 1, axis=0)


def _token_shift(x, up, dn, has_up, has_dn, mode):
    rows, ch = x.shape
    if mode == "seq":
        half = ch // 2
        t = lax.broadcasted_iota(jnp.int32, (rows, half), 0)
        s0 = jnp.where(t == 0, 0.0, _from_prev(x[:, :half]))
        s1 = jnp.where(t == rows - 1, 0.0, _from_next(x[:, half:]))
        return [s0, s1]
    q = ch // 4
    t = lax.broadcasted_iota(jnp.int32, (rows, q), 0) % GRID_W
    s0 = jnp.where(t == 0, 0.0, _from_prev(x[:, :q]))
    s1 = jnp.where(t == GRID_W - 1, 0.0, _from_next(x[:, q:2 * q]))
    up = jnp.where(has_up, up, 0.0)
    dn = jnp.where(has_dn, dn, 0.0)
    if rows == GRID_W:
        s2, s3 = up, dn
    else:
        s2 = jnp.concatenate([up, x[:rows - GRID_W, 2 * q:3 * q]], axis=0)
        s3 = jnp.concatenate([x[GRID_W:, 3 * q:], dn], axis=0)
    return [s0, s1, s2, s3]


def _prep_kernel(*refs, mode, nblk):
    if mode == "grid":
        x_ref, up_ref, dn_ref, w_ref, sh_ref, sc_ref, o_ref = refs
    else:
        x_ref, w_ref, sh_ref, sc_ref, o_ref = refs
    i = pl.program_id(1)
    w, sh, sc = w_ref[...], sh_ref[0], sc_ref[0]

    def modulated(x):
        y = x * lax.rsqrt(jnp.mean(x * x, axis=-1, keepdims=True) + NORM_EPS)
        return (y * w) * (1 + sc) + sh

    h = modulated(x_ref[0])
    d = h.shape[1]
    if mode == "grid":
        q = d // 4
        up = modulated(up_ref[0])[:, 2 * q:3 * q]
        dn = modulated(dn_ref[0])[:, 3 * q:]
        parts = _token_shift(h, up, dn, i > 0, i < nblk - 1, mode)
    else:
        parts = _token_shift(h, None, None, None, None, mode)
    o_ref[:, :d] = h.astype(o_ref.dtype)
    width = d // len(parts)
    for n, s in enumerate(parts):
        lo = n * width
        o_ref[:, d + lo:d + lo + width] = (s - h[:, lo:lo + width]).astype(o_ref.dtype)


def prep(x, w, shift, scale, mode):
    b, t, d = x.shape
    rows = _pick(t, PREP_ROWS, GRID_W) if mode == "grid" else t
    nblk = t // rows
    per = rows // GRID_W
    x_spec = pl.BlockSpec((1, rows, d), lambda bi, i: (bi, i, 0))
    vec_spec = pl.BlockSpec((1, d), lambda bi, i: (0, 0))
    mod_spec = pl.BlockSpec((1, 1, d), lambda bi, i: (bi, 0, 0))
    in_specs, args = [x_spec], [x]
    if mode == "grid":
        last = t // GRID_W - 1
        in_specs += [pl.BlockSpec((1, GRID_W, d), lambda bi, i: (bi, jnp.maximum(i * per - 1, 0), 0)),
                     pl.BlockSpec((1, GRID_W, d), lambda bi, i: (bi, jnp.minimum((i + 1) * per, last), 0))]
        args += [x, x]
    in_specs += [vec_spec, mod_spec, mod_spec]
    args += [w[None, :], shift, scale]
    return pl.pallas_call(
        functools.partial(_prep_kernel, mode=mode, nblk=nblk),
        grid=(b, nblk),
        in_specs=in_specs,
        out_specs=pl.BlockSpec((rows, 2 * d), lambda bi, i: (bi * nblk + i, 0)),
        out_shape=jax.ShapeDtypeStruct((b * t, 2 * d), BF16),
        compiler_params=_params(("parallel", "parallel")),
        name="prep_" + mode,
    )(*args)


def _norm_mod_kernel(x_ref, w_ref, sh_ref, sc_ref, *rest, routed):
    x = x_ref[...]
    y = x * lax.rsqrt(jnp.mean(x * x, axis=-1, keepdims=True) + NORM_EPS)
    h = (y * w_ref[...]) * (1 + sc_ref[0]) + sh_ref[0]
    if routed:
        rw_ref, rb_ref, o_ref, lg_ref = rest
        lg_ref[...] = jnp.dot(h, rw_ref[...], precision=HI, preferred_element_type=F32) + rb_ref[...]
    else:
        (o_ref,) = rest
    o_ref[...] = h.astype(o_ref.dtype)


def norm_mod(x, w, shift, scale, rows_per_batch, router_w=None, router_b=None):
    m, d = x.shape
    rows = _pick(rows_per_batch, PREP_ROWS, 8)
    nblk = rows_per_batch // rows
    routed = router_w is not None
    x_spec = pl.BlockSpec((rows, d), lambda bi, i: (bi * nblk + i, 0))
    mod_spec = pl.BlockSpec((1, 1, d), lambda bi, i: (bi, 0, 0))
    in_specs = [x_spec, pl.BlockSpec((1, d), lambda bi, i: (0, 0)), mod_spec, mod_spec]
    args = [x, w[None, :], shift, scale]
    out_specs, out_shape = [x_spec], [jax.ShapeDtypeStruct((m, d), BF16)]
    if routed:
        n_e = router_w.shape[1]
        pad = -n_e % LANE
        in_specs += [pl.BlockSpec((d, n_e + pad), lambda bi, i: (0, 0)),
                     pl.BlockSpec((1, n_e + pad), lambda bi, i: (0, 0))]
        args += [jnp.pad(router_w, ((0, 0), (0, pad))), jnp.pad(router_b, (0, pad))[None, :]]
        out_specs.append(pl.BlockSpec((rows, n_e + pad), lambda bi, i: (bi * nblk + i, 0)))
        out_shape.append(jax.ShapeDtypeStruct((m, n_e + pad), F32))
    out = pl.pallas_call(
        functools.partial(_norm_mod_kernel, routed=routed),
        grid=(m // rows_per_batch, nblk),
        in_specs=in_specs,
        out_specs=out_specs,
        out_shape=out_shape,
        compiler_params=_params(("parallel", "parallel")),
        name="norm_mod",
    )(*args)
    return (out[0], out[1][:, :router_w.shape[1]]) if routed else (out[0], None)


def _head_sums(x, hd):
    rows, width = x.shape
    nt = width // LANE
    stacked = jnp.concatenate([x[:, i * LANE:(i + 1) * LANE] for i in range(nt)], axis=0)
    li = lax.broadcasted_iota(jnp.int32, (LANE, LANE), 0) // hd
    lj = lax.broadcasted_iota(jnp.int32, (LANE, LANE), 1) // hd
    ones = jnp.where(li == lj, 1.0, 0.0).astype(BF16)
    hi = _bf(stacked)
    rem = stacked - hi.astype(F32)
    mid = _bf(rem)
    lo = _bf(rem - mid.astype(F32))
    s = (jnp.dot(hi, ones, preferred_element_type=F32) + jnp.dot(mid, ones, preferred_element_type=F32)
         + jnp.dot(lo, ones, preferred_element_type=F32))
    return jnp.concatenate([s[i * rows:(i + 1) * rows] for i in range(nt)], axis=1)


def _wkv_prep_kernel(*refs, mode, nblk, hd, offs, has_vres):
    refs = list(refs)
    main = [refs.pop(0) for _ in range(3)]
    halo = [(refs.pop(0), refs.pop(0)) for _ in range(3)] if mode == "grid" else [(None, None)] * 3
    low_ref = refs.pop(0)
    vf_ref = refs.pop(0) if has_vres else None
    mu_ref, w0_ref, a0_ref, kk_ref, ka_ref, w2_ref, a2_ref = [refs.pop(0) for _ in range(7)]
    if has_vres:
        v0_ref, v2_ref = refs.pop(0), refs.pop(0)
    r_o, v_o, kk_o, kdf_o, kdr_o, af_o, ar_o, lwf_o, lwr_o = refs
    c = pl.program_id(1)
    low = low_ref[0]

    def lerp(n):
        x = main[n][0].astype(F32)
        up, dn = ((halo[n][0][0].astype(F32), halo[n][1][0].astype(F32)) if mode == "grid"
                  else (None, None))
        sh = jnp.concatenate(_token_shift(x, up, dn, c > 0, c < nblk - 1, mode), axis=1)
        return x + (sh - x) * mu_ref[n:n + 1, :]

    def low_dot(lo, hi_, w, act=None):
        z = low[:, lo:hi_]
        if act is not None:
            z = act(z)
        return jnp.dot(_bf(z), w, preferred_element_type=F32)

    r = lerp(0)
    k = lerp(1)
    v = lerp(2)
    if has_vres:
        gate = jax.nn.sigmoid(v0_ref[...] + low_dot(offs[6], offs[7], v2_ref[...]))
        v = v + (vf_ref[0].astype(F32) - v) * gate
    kq = k * kk_ref[...]
    kk = kq * lax.rsqrt(jnp.maximum(_head_sums(kq * kq, hd), 1e-24))
    r_o[0] = r.astype(r_o.dtype)
    v_o[0] = v.astype(v_o.dtype)
    kk_o[0] = kk.astype(kk_o.dtype)
    for di, (kd_o, a_o, lw_o) in enumerate(((kdf_o, af_o, lwf_o), (kdr_o, ar_o, lwr_o))):
        wl = w0_ref[di:di + 1, :] + low_dot(offs[di], offs[di + 1], w2_ref[di], jnp.tanh)
        softplus = jnp.maximum(-wl, 0.0) + jnp.log1p(jnp.exp(-jnp.abs(wl)))
        lw_o[0] = -jnp.exp(-softplus - DECAY_OFFSET)
        a = jax.nn.sigmoid(a0_ref[di:di + 1, :] + low_dot(offs[2 + di], offs[3 + di], a2_ref[di]))
        a_o[0] = a.astype(a_o.dtype)
        kd_o[0] = (k * (1.0 + (a - 1.0) * ka_ref[...])).astype(kd_o.dtype)


def wkv_prep(p, low, v_first, mu, w0, a0, k_k, k_a, w2, a2, v0, v2, offs, hd, mode):
    b, t, rw3 = p.shape
    rw = rw3 // 3
    q = rw // 4
    rows = GRID_W if mode == "grid" else t
    nblk = t // rows
    has_vres = v_first is not None

    def tok(width, col):
        return pl.BlockSpec((1, rows, width), lambda bi, ci: (bi, ci, col))

    def full(arr):
        nd = arr.ndim
        return pl.BlockSpec(arr.shape, lambda bi, ci: (0,) * nd)

    in_specs = [tok(rw, n) for n in range(3)]
    args = [p, p, p]
    if mode == "grid":
        for n in range(3):
            in_specs += [pl.BlockSpec((1, rows, q), lambda bi, ci, n=n: (bi, jnp.maximum(ci - 1, 0), 4 * n + 2)),
                         pl.BlockSpec((1, rows, q), lambda bi, ci, n=n: (bi, jnp.minimum(ci + 1, nblk - 1), 4 * n + 3))]
            args += [p, p]
    in_specs.append(tok(low.shape[-1], 0))
    args.append(low)
    if has_vres:
        in_specs.append(tok(rw, 0))
        args.append(v_first)
    small = [mu, w0, a0, k_k[None, :], k_a[None, :], w2.astype(BF16), a2.astype(BF16)]
    if has_vres:
        small += [v0[None, :], v2.astype(BF16)]
    in_specs += [full(s) for s in small]
    args += small
    out_spec = tok(rw, 0)
    shapes = [jax.ShapeDtypeStruct((b, t, rw), BF16)] * 7 + [jax.ShapeDtypeStruct((b, t, rw), F32)] * 2
    return pl.pallas_call(
        functools.partial(_wkv_prep_kernel, mode=mode, nblk=nblk, hd=hd, offs=tuple(offs), has_vres=has_vres),
        grid=(b, nblk),
        in_specs=in_specs,
        out_specs=[out_spec] * 9,
        out_shape=shapes,
        compiler_params=_params(("parallel", "parallel")),
        name="wkv_prep_" + mode,
    )(*args)


def _wkv_post_kernel(yf_ref, yr_ref, r_ref, kdf_ref, kdr_ref, v_ref, lg_ref, rk_ref, gw_ref, gb_ref,
                     g2_ref, o_ref, *, hd):
    y = yf_ref[0] + yr_ref[0]
    mean = _head_sums(y, hd) * (1.0 / hd)
    yc = y - mean
    var = _head_sums(yc * yc, hd) * (1.0 / hd)
    y = yc * lax.rsqrt(var + GN_EPS) * gw_ref[...] + gb_ref[...]
    rk = r_ref[0].astype(F32) * (kdf_ref[0].astype(F32) + kdr_ref[0].astype(F32)) * rk_ref[...]
    y = y + _head_sums(rk, hd) * v_ref[0].astype(F32)
    g = jnp.dot(_bf(jax.nn.sigmoid(lg_ref[0])), g2_ref[...], preferred_element_type=F32)
    o_ref[0] = (y * g).astype(o_ref.dtype)


def wkv_post(y_f, y_r, r, kd_f, kd_r, v, low_g, r_k, gn_w, gn_b, g2, hd):
    b, t, rw = y_f.shape
    rows = _pick(t, POST_ROWS, 8)

    def tok(width):
        return pl.BlockSpec((1, rows, width), lambda bi, ci: (bi, ci, 0))

    def full(arr):
        return pl.BlockSpec(arr.shape, lambda bi, ci: (0, 0))

    small = [r_k.reshape(1, rw), gn_w[None, :], gn_b[None, :], g2.astype(BF16)]
    return pl.pallas_call(
        functools.partial(_wkv_post_kernel, hd=hd),
        grid=(b, t // rows),
        in_specs=[tok(rw)] * 6 + [tok(low_g.shape[-1])] + [full(s) for s in small],
        out_specs=tok(rw),
        out_shape=jax.ShapeDtypeStruct((b, t, rw), BF16),
        compiler_params=_params(("parallel", "parallel")),
        name="wkv_post",
    )(y_f, y_r, r, kd_f, kd_r, v, low_g, *small)


def rmsnorm(x, w):
    y = x * lax.rsqrt(jnp.mean(x * x, axis=-1, keepdims=True) + NORM_EPS)
    return y * w


def modulate(x, w, shift, scale):
    return rmsnorm(x, w) * (1 + scale) + shift


def dft_tables(rows, cols, n, scale):
    j = lax.broadcasted_iota(jnp.int32, (rows, cols), 0)
    k = lax.broadcasted_iota(jnp.int32, (rows, cols), 1)
    ang = ((j * k) % n).astype(F32) * (2.0 * math.pi / n)
    return jnp.cos(ang) * scale, jnp.sin(ang) * scale


def _dft_stage_kernel(l_ref, zr_ref, zi_ref, *rest, twiddle):
    z = jnp.concatenate([zr_ref[...], zi_ref[...]], axis=0)
    y = jnp.dot(l_ref[...], z, preferred_element_type=F32)
    if not twiddle:
        rest[0][...] = y.astype(rest[0].dtype)
        return
    twr_ref, twi_ref, o_ref = rest
    half = y.shape[0] // 2
    reps = y.shape[1] // LANE
    tr, ti = jnp.tile(twr_ref[...], (1, reps)), jnp.tile(twi_ref[...], (1, reps))
    yr, yi = y[:half], y[half:]
    o_ref[0] = (yr * tr - yi * ti).astype(o_ref.dtype)
    o_ref[1] = (yr * ti + yi * tr).astype(o_ref.dtype)


class PosDft:
    def __init__(self, n):
        self.n = n
        self.n1 = DFT_N1 if n % DFT_N1 == 0 and n // DFT_N1 >= 8 and (n // DFT_N1) % 8 == 0 else 0
        if not self.n1:
            c, s = dft_tables(n, n, n, n ** -0.5)
            self.direct = jnp.concatenate([c, s], axis=1).astype(BF16)
            return
        n1, n2 = self.n1, n // self.n1
        self.n2 = n2
        c1, s1 = dft_tables(n1, n1, n1, n1 ** -0.5)
        self.l1 = jnp.concatenate([jnp.concatenate([c1, s1], axis=1),
                                   jnp.concatenate([-s1, c1], axis=1)], axis=0).astype(BF16)
        tc, ts = dft_tables(n2, n1, n, 1.0)
        self.twr = jnp.broadcast_to(tc[:, :, None], (n2, n1, LANE))
        self.twi = jnp.broadcast_to(-ts[:, :, None], (n2, n1, LANE))
        c2, s2 = dft_tables(n2, n2, n2, n2 ** -0.5)
        self.l3 = jnp.concatenate([c2, s2], axis=1).astype(BF16)


def fourier_mix(u, pos, chan_table):
    b, n, fw = u.shape
    z = matmul(u.reshape(b * n, fw).astype(BF16), chan_table, BF16, planes=2)
    if not pos.n1:
        z = z.reshape(2, b, n, fw).transpose(1, 0, 2, 3).reshape(b, 2 * n, fw)
        return jnp.stack([matmul(pos.direct, z[i], BF16) for i in range(b)], axis=0)
    n1, n2 = pos.n1, pos.n2
    z = z.reshape(2, b, n1, n2 * fw)
    y = pl.pallas_call(
        functools.partial(_dft_stage_kernel, twiddle=True),
        grid=(b, n2),
        in_specs=[pl.BlockSpec((2 * n1, 2 * n1), lambda bi, j: (0, 0)),
                  pl.BlockSpec((None, None, n1, fw), lambda bi, j: (0, bi, 0, j)),
                  pl.BlockSpec((None, None, n1, fw), lambda bi, j: (1, bi, 0, j)),
                  pl.BlockSpec((None, n1, LANE), lambda bi, j: (j, 0, 0)),
                  pl.BlockSpec((None, n1, LANE), lambda bi, j: (j, 0, 0))],
        out_specs=pl.BlockSpec((None, 2, None, n1, fw), lambda bi, j: (bi, 0, j, 0, 0)),
        out_shape=jax.ShapeDtypeStruct((b, 2, n2, n1, fw), BF16),
        compiler_params=_params(("parallel", "parallel")),
        name="dft_stage1",
    )(pos.l1, z, z, pos.twr, pos.twi)
    y = y.reshape(b, 2, n2, n1 * fw)
    tn = _pick(n1 * fw, 4096, LANE)
    out = pl.pallas_call(
        functools.partial(_dft_stage_kernel, twiddle=False),
        grid=(b, n1 * fw // tn),
        in_specs=[pl.BlockSpec((n2, 2 * n2), lambda bi, j: (0, 0)),
                  pl.BlockSpec((None, None, n2, tn), lambda bi, j: (bi, 0, 0, j)),
                  pl.BlockSpec((None, None, n2, tn), lambda bi, j: (bi, 1, 0, j))],
        out_specs=pl.BlockSpec((None, n2, tn), lambda bi, j: (bi, 0, j)),
        out_shape=jax.ShapeDtypeStruct((b, n2, n1 * fw), BF16),
        compiler_params=_params(("parallel", "parallel")),
        name="dft_stage2",
    )(pos.l3, y, y)
    return out.reshape(b, n, fw)


def depthwise_conv(u, w):
    n = u.shape[1]
    kk = w.shape[0]
    pad = kk // 2
    up = jnp.pad(u, ((0, 0), (pad, pad), (0, 0)))
    out = up[:, 0:n] * w[0]
    for i in range(1, kk):
        out = out + up[:, i:i + n] * w[i]
    return out


def kernel(x, c, ctx, c_ctx, ada_a, ada_b, ada_bias, norm1_w, norm2_w, w_in, mu_rkv, mu_lr, decay_w0, decay_w1, decay_w2, iclr_a0, iclr_a1, iclr_a2, ogate_g1, ogate_g2, k_k, k_a, r_k, gn_w, gn_b, vres_mu, vres_v0, vres_v1, vres_v2, conv_w, gate_w1, gate_w2, gate_b, proj_rwkv, proj_fourier, proj_conv, w_out, ffn_w1, ffn_w3, ffn_w2, router_w, router_b, moe_w1, moe_w3, moe_w2, final_norm_w):
    depth = w_in.shape[0]
    bsz, seq, d = x.shape
    ctx_len = ctx.shape[1]
    rw = mu_rkv.shape[-1]
    nh, hd = r_k.shape[1], r_k.shape[2]
    fw = proj_fourier.shape[1]
    cw = proj_conv.shape[1]
    four_off = 3 * rw
    conv_off = four_off + fw
    n_exp = router_w.shape[-1]
    d_exp = moe_w1.shape[-1]
    n_mod = ada_b.shape[-1] // d
    r_dec, r_icl, r_og, r_vr, r_gate = (decay_w1.shape[-1], iclr_a1.shape[-1], ogate_g1.shape[-1],
                                        vres_v1.shape[-1], gate_w1.shape[-1])

    gc = fw // FOURIER_GROUPS
    cc, cs = dft_tables(gc, gc, gc, gc ** -0.5)
    eye_g = jnp.eye(FOURIER_GROUPS, dtype=F32)
    chan_table = jnp.concatenate([jnp.kron(eye_g, cc), -jnp.kron(eye_g, cs)], axis=1).astype(BF16)
    pos_lat, pos_ctx = PosDft(seq), PosDft(ctx_len)

    w_in_b, w_out_b = w_in.astype(BF16), w_out.astype(BF16)
    proj_r_b, proj_f_b, proj_c_b = proj_rwkv.astype(BF16), proj_fourier.astype(BF16), proj_conv.astype(BF16)
    gate_w2_b, gate_b3 = gate_w2.astype(BF16), gate_b[:, None, :]
    ffn_w1_b, ffn_w3_b, ffn_w2_b = ffn_w1.astype(BF16), ffn_w3.astype(BF16), ffn_w2.astype(BF16)
    moe_w1_b = moe_w1.astype(BF16).reshape((-1,) + moe_w1.shape[2:])
    moe_w3_b = moe_w3.astype(BF16).reshape((-1,) + moe_w3.shape[2:])
    moe_w2_b = moe_w2.astype(BF16).reshape(moe_w2.shape[0], n_exp * d_exp, d)

    cond_lat = jax.nn.silu(c)
    cond_ctx = jax.nn.silu(c_ctx)[None, :]
    cond = jnp.concatenate([cond_lat, cond_ctx], axis=0)
    cond = jnp.pad(cond, ((0, 16 - cond.shape[0] % 16), (0, 0)))

    def tokens(xs, mod, l, mode, pos_tab, s0, v_first, last_ctx):
        b, t, _ = xs.shape
        has_vres = l > 0
        hd_cat = prep(xs, norm1_w[l], jnp.broadcast_to(mod[0], (b, 1, d)),
                      jnp.broadcast_to(mod[1], (b, 1, d)), mode)

        w_h = [decay_w1[l, 0], decay_w1[l, 1], iclr_a1[l, 0], iclr_a1[l, 1], ogate_g1[l], gate_w1[l]]
        mus = [mu_lr[l, 0], mu_lr[l, 0], mu_lr[l, 1], mu_lr[l, 1], mu_lr[l, 2], None]
        if has_vres:
            w_h.append(vres_v1[l - 1])
            mus.append(vres_mu[l - 1])
        w_dh = [jnp.zeros_like(w) if m is None else w * m[:, None] for w, m in zip(w_h, mus)]
        w_low = jnp.concatenate([jnp.concatenate(w_h, axis=1), jnp.concatenate(w_dh, axis=1)], axis=0)
        n_low = w_low.shape[1]
        w_low = jnp.pad(w_low, ((0, 0), (0, -n_low % LANE)))
        low = matmul(hd_cat, w_low.astype(BF16)).reshape(b, t, -1)
        offs = [0]
        for w in w_h:
            offs.append(offs[-1] + w.shape[1])
        low_g = low[..., offs[4]:offs[5]]
        low_gate = low[..., offs[5]:offs[6]]

        p = matmul(hd_cat, w_in_b, BF16, n_cols=four_off, layer=l).reshape(b, t, -1)
        p_fc = matmul(hd_cat, w_in_b, BF16, n_cols=w_in_b.shape[-1] - four_off,
                      col_off=four_off, layer=l).reshape(b, t, -1)

        r, v, kk, kd_f, kd_r, a_f, a_r, lw_f, lw_r = wkv_prep(
            p, low, v_first, mu_rkv[l], decay_w0[l], iclr_a0[l], k_k[l], k_a[l], decay_w2[l], iclr_a2[l],
            vres_v0[l - 1] if has_vres else None, vres_v2[l - 1] if has_vres else None, offs, hd, mode)

        if s0 is None:
            s0 = [jnp.zeros(wkv_state_shape(b, nh, hd), F32)] * 2
        ys, states = wkv(r, v, kk, [(lw_f, kd_f, a_f), (lw_r, kd_r, a_r)], s0, hd)
        if last_ctx:
            return None, states, v

        y_rwkv = wkv_post(ys[0], ys[1], r, kd_f, kd_r, v, low_g, r_k[l], gn_w[l], gn_b[l],
                          ogate_g2[l], hd)

        y_four = fourier_mix(p_fc[..., :fw], pos_tab, chan_table)
        gate_b_ = p_fc[..., fw:fw + cw].astype(F32)
        gate_c_ = p_fc[..., fw + cw:fw + 2 * cw].astype(F32)
        uu = p_fc[..., fw + 2 * cw:].astype(F32)
        y_conv = gate_b_ * depthwise_conv(gate_c_ * uu, conv_w[l])

        rows = b * t

        def flat(z):
            return z.reshape(rows, -1).astype(BF16)

        merged = gated_merge(flat(y_rwkv), flat(y_four), flat(y_conv), flat(low_gate),
                             proj_r_b, proj_f_b, proj_c_b, gate_w2_b, gate_b3, l)
        xs = resid_matmul(merged, w_out_b, xs.reshape(rows, d),
                          jnp.broadcast_to(mod[2], (b, 1, d)), t, l)

        j = l // 2
        dense = l % 2 == 0
        h2, logits = norm_mod(xs, norm2_w[l], jnp.broadcast_to(mod[3], (b, 1, d)),
                              jnp.broadcast_to(mod[4], (b, 1, d)), t,
                              None if dense else router_w[j], None if dense else router_b[j])
        if dense:
            hid = swiglu_up(h2, ffn_w1_b, ffn_w3_b, e0=j, n_e=1)
            w2 = ffn_w2_b
        else:
            top_val, top_idx = lax.top_k(logits, TOP_K)
            weights = jax.nn.softmax(top_val, axis=-1)
            gate = jnp.sum(jax.nn.one_hot(top_idx, n_exp, dtype=F32) * weights[..., None], axis=-2)
            gate_rep = jnp.repeat(gate, LANE, axis=-1)
            hid = swiglu_up(h2, moe_w1_b, moe_w3_b, gate_rep, e0=j * n_exp, n_e=n_exp)
            w2 = moe_w2_b
        xs = resid_matmul(hid, w2, xs, jnp.broadcast_to(mod[5], (b, 1, d)), t, j)
        return xs.reshape(b, t, d), states, v

    xl, xc = x, ctx
    v_first_l = v_first_c = None
    for l in range(depth):
        last = l == depth - 1
        m = mm(mm(cond, ada_a[l]), ada_b[l]) + ada_bias[l]
        mod_l = [m[:bsz, None, i * d:(i + 1) * d] for i in range(n_mod)]
        mod_c = [m[bsz:bsz + 1, None, i * d:(i + 1) * d] for i in range(n_mod)]

        xc_new, ctx_states, vc = tokens(xc, mod_c, l, "seq", pos_ctx, None, v_first_c, last)
        xl, _, vl = tokens(xl, mod_l, l, "grid", pos_lat, ctx_states, v_first_l, False)
        if l == 0:
            v_first_c, v_first_l = vc, vl
        if not last:
            xc = xc_new
    return rmsnorm(xl, final_norm_w)
```

```python
import functools
import math

import jax
import jax.numpy as jnp
from jax import lax
from jax.experimental import pallas as pl
from jax.experimental.pallas import tpu as pltpu

F32 = jnp.float32
BF16 = jnp.bfloat16

GRID_W = 64
FOURIER_GROUPS = 8
TOP_K = 2
NORM_EPS = 1e-6
GN_EPS = 64e-5
DECAY_OFFSET = 0.5
WKV_CHUNK = 64
WKV_HEADS = 32
MOE_TILE = 512
MOE_SPARSE_MIN_ROWS = 4096
DFT_N1 = 128
PREP_ROWS = 256
POST_ROWS = 256
LANE = 128
VMEM_LIMIT = 56 * 1024 * 1024

HI = lax.Precision.HIGHEST


def _pick(dim, target, align):
    if dim <= target:
        return dim
    t = (target // align) * align
    while t >= align:
        if dim % t == 0:
            return t
        t -= align
    return dim


def _mm_kernel(a_ref, b_ref, o_ref, acc_ref, *, nk):
    k = pl.program_id(2)

    @pl.when(k == 0)
    def _():
        acc_ref[...] = jnp.zeros_like(acc_ref)

    acc_ref[...] += jnp.dot(a_ref[...], b_ref[...], preferred_element_type=F32)

    @pl.when(k == nk - 1)
    def _():
        o_ref[...] = acc_ref[...].astype(o_ref.dtype)


def matmul(a, b, out_dtype=F32, tm=1024, tn=1024, tk=2048, n_cols=None, col_off=0, planes=1, layer=None):
    m = a.shape[0]
    k = b.shape[-2]
    n = b.shape[-1] if n_cols is None else n_cols
    tm = _pick(m, tm, 16)
    tn = _pick(math.gcd(n // planes, col_off) if col_off else n // planes, tn, LANE)
    tk = _pick(k, tk, LANE)
    nk = k // tk
    joff = col_off // tn
    if planes == 1:
        out_spec = pl.BlockSpec((tm, tn), lambda i, j, kk: (i, j))
        out_shape = jax.ShapeDtypeStruct((m, n), out_dtype)
    else:
        per = n // planes // tn
        out_spec = pl.BlockSpec((None, tm, tn), lambda i, j, kk: (j // per, i, j % per))
        out_shape = jax.ShapeDtypeStruct((planes, m, n // planes), out_dtype)
    if layer is None:
        b_spec = pl.BlockSpec((tk, tn), lambda i, j, kk: (kk, j + joff))
    else:
        b_spec = pl.BlockSpec((None, tk, tn), lambda i, j, kk: (layer, kk, j + joff))
    return pl.pallas_call(
        functools.partial(_mm_kernel, nk=nk),
        grid=(m // tm, n // tn, nk),
        in_specs=[pl.BlockSpec((tm, tk), lambda i, j, kk: (i, kk)), b_spec],
        out_specs=out_spec,
        out_shape=out_shape,
        scratch_shapes=[pltpu.VMEM((tm, tn), F32)],
        compiler_params=pltpu.CompilerParams(
            dimension_semantics=("parallel", "parallel", "arbitrary"),
            vmem_limit_bytes=VMEM_LIMIT),
        name="matmul",
    )(a, b)


def mm(a, b, out_dtype=F32, **kw):
    lead = a.shape[:-1]
    out = matmul(a.reshape(-1, a.shape[-1]).astype(BF16), b.astype(BF16), out_dtype, **kw)
    return out.reshape(lead + (b.shape[-1],))


def _params(sem):
    return pltpu.CompilerParams(dimension_semantics=sem, vmem_limit_bytes=VMEM_LIMIT)


def _swiglu_up_kernel(a_ref, w1_ref, w3_ref, *rest, nk, gated):
    if gated:
        g_ref, o_ref, acc1_ref, acc3_ref = rest
    else:
        o_ref, acc1_ref, acc3_ref = rest
    k = pl.program_id(2)

    @pl.when(k == 0)
    def _():
        acc1_ref[...] = jnp.zeros_like(acc1_ref)
        acc3_ref[...] = jnp.zeros_like(acc3_ref)

    a = a_ref[...]
    acc1_ref[...] += jnp.dot(a, w1_ref[...], preferred_element_type=F32)
    acc3_ref[...] += jnp.dot(a, w3_ref[...], preferred_element_type=F32)

    @pl.when(k == nk - 1)
    def _():
        h1 = acc1_ref[...]
        hid = h1 * jax.nn.sigmoid(h1) * acc3_ref[...]
        if gated:
            hid = hid * jnp.tile(g_ref[...], (1, hid.shape[1] // LANE))
        o_ref[...] = hid.astype(o_ref.dtype)


def swiglu_up(a, w1, w3, gate_rep=None, e0=0, n_e=None):
    m, k = a.shape
    f = w1.shape[-1]
    n_e = w1.shape[0] if n_e is None else n_e
    n = n_e * f
    tm = _pick(m, 1024, 16)
    tk = _pick(k, 2048, LANE)
    tn = _pick(f, 1024, LANE)
    gated = gate_rep is not None
    nk = k // tk
    per = f // tn
    w_spec = pl.BlockSpec((None, tk, tn), lambda i, j, kk: (e0 + j // per, kk, j % per))
    in_specs = [pl.BlockSpec((tm, tk), lambda i, j, kk: (i, kk)), w_spec, w_spec]
    args = [a, w1, w3]
    if gated:
        in_specs.append(pl.BlockSpec((tm, LANE), lambda i, j, kk: (i, j // per)))
        args.append(gate_rep)
    return pl.pallas_call(
        functools.partial(_swiglu_up_kernel, nk=nk, gated=gated),
        grid=(m // tm, n // tn, nk),
        in_specs=in_specs,
        out_specs=pl.BlockSpec((tm, tn), lambda i, j, kk: (i, j)),
        out_shape=jax.ShapeDtypeStruct((m, n), BF16),
        scratch_shapes=[pltpu.VMEM((tm, tn), F32), pltpu.VMEM((tm, tn), F32)],
        compiler_params=_params(("parallel", "parallel", "arbitrary")),
        name="swiglu_up",
    )(*args)


def _resid_kernel(a_ref, w_ref, x_ref, g_ref, o_ref, acc_ref, *, nk):
    k = pl.program_id(2)

    @pl.when(k == 0)
    def _():
        acc_ref[...] = jnp.zeros_like(acc_ref)

    acc_ref[...] += jnp.dot(a_ref[...], w_ref[...], preferred_element_type=F32)

    @pl.when(k == nk - 1)
    def _():
        o_ref[...] = x_ref[...] + g_ref[0] * acc_ref[...]


def resid_matmul(a, w, x, g, rows_per_batch, layer):
    m, k = a.shape
    n = w.shape[-1]
    tm = _pick(rows_per_batch, 1024, 16)
    tn = _pick(n, 1024, LANE)
    tk = _pick(k, 2048, LANE)
    nk = k // tk
    per = rows_per_batch // tm
    return pl.pallas_call(
        functools.partial(_resid_kernel, nk=nk),
        grid=(m // tm, n // tn, nk),
        in_specs=[pl.BlockSpec((tm, tk), lambda i, j, kk: (i, kk)),
                  pl.BlockSpec((None, tk, tn), lambda i, j, kk: (layer, kk, j)),
                  pl.BlockSpec((tm, tn), lambda i, j, kk: (i, j)),
                  pl.BlockSpec((1, 1, tn), lambda i, j, kk: (i // per, 0, j))],
        out_specs=pl.BlockSpec((tm, tn), lambda i, j, kk: (i, j)),
        out_shape=jax.ShapeDtypeStruct((m, n), F32),
        scratch_shapes=[pltpu.VMEM((tm, tn), F32)],
        compiler_params=_params(("parallel", "parallel", "arbitrary")),
        name="resid_matmul",
    )(a, w, x, g)


def _grouped_up_kernel(te_ref, a_ref, w1_ref, w3_ref, o_ref, acc1_ref, acc3_ref, *, nk):
    del te_ref
    k = pl.program_id(2)

    @pl.when(k == 0)
    def _():
        acc1_ref[...] = jnp.zeros_like(acc1_ref)
        acc3_ref[...] = jnp.zeros_like(acc3_ref)

    a = a_ref[...]
    acc1_ref[...] += jnp.dot(a, w1_ref[...], preferred_element_type=F32)
    acc3_ref[...] += jnp.dot(a, w3_ref[...], preferred_element_type=F32)

    @pl.when(k == nk - 1)
    def _():
        h1 = acc1_ref[...]
        o_ref[...] = (h1 * jax.nn.sigmoid(h1) * acc3_ref[...]).astype(o_ref.dtype)


def _grouped_down_kernel(te_ref, a_ref, w_ref, g_ref, o_ref):
    del te_ref
    y = jnp.dot(a_ref[...], w_ref[...], preferred_element_type=F32)
    o_ref[...] = (y * jnp.tile(g_ref[...], (1, y.shape[1] // LANE))).astype(o_ref.dtype)


def grouped_swiglu(x, tile_expert, row_gate, w1, w3, w2, tm):
    r, d = x.shape
    f = w1.shape[-1]
    tk = _pick(d, 2048, LANE)
    tn = _pick(f, 1024, LANE)
    nk = d // tk
    hid = pl.pallas_call(
        functools.partial(_grouped_up_kernel, nk=nk),
        grid_spec=pltpu.PrefetchScalarGridSpec(
            num_scalar_prefetch=1,
            grid=(r // tm, f // tn, nk),
            in_specs=[pl.BlockSpec((tm, tk), lambda i, j, kk, te: (i, kk)),
                      pl.BlockSpec((None, tk, tn), lambda i, j, kk, te: (te[i], kk, j)),
                      pl.BlockSpec((None, tk, tn), lambda i, j, kk, te: (te[i], kk, j))],
            out_specs=pl.BlockSpec((tm, tn), lambda i, j, kk, te: (i, j)),
            scratch_shapes=[pltpu.VMEM((tm, tn), F32), pltpu.VMEM((tm, tn), F32)]),
        out_shape=jax.ShapeDtypeStruct((r, f), BF16),
        compiler_params=_params(("parallel", "parallel", "arbitrary")),
        name="grouped_up",
    )(tile_expert, x, w1, w3)
    tn2 = _pick(d, 1024, LANE)
    return pl.pallas_call(
        _grouped_down_kernel,
        grid_spec=pltpu.PrefetchScalarGridSpec(
            num_scalar_prefetch=1,
            grid=(r // tm, d // tn2),
            in_specs=[pl.BlockSpec((tm, f), lambda i, j, te: (i, 0)),
                      pl.BlockSpec((None, f, tn2), lambda i, j, te: (te[i], 0, j)),
                      pl.BlockSpec((tm, LANE), lambda i, j, te: (i, 0))],
            out_specs=pl.BlockSpec((tm, tn2), lambda i, j, te: (i, j))),
        out_shape=jax.ShapeDtypeStruct((r, d), BF16),
        compiler_params=_params(("parallel", "parallel")),
        name="grouped_down",
    )(tile_expert, hid, w2, row_gate)


def moe_dispatch(top_idx, weights, n_exp, tm):
    m, k = top_idx.shape
    n = m * k
    e_flat = top_idx.reshape(n)
    order = jnp.argsort(e_flat, stable=True)
    e_sorted = e_flat[order]
    counts = jnp.sum(jax.nn.one_hot(e_flat, n_exp, dtype=jnp.int32), axis=0)
    padded = (counts + tm - 1) // tm * tm
    pad_end = jnp.cumsum(padded)
    pad_start = pad_end - padded
    start = jnp.cumsum(counts) - counts
    dest_sorted = pad_start[e_sorted] + jnp.arange(n, dtype=jnp.int32) - start[e_sorted]
    r = n + n_exp * tm
    src_token = jnp.zeros((r,), jnp.int32).at[dest_sorted].set(order // k)
    row_gate = jnp.zeros((r,), F32).at[dest_sorted].set(weights.reshape(n)[order])
    tile_start = jnp.arange(r // tm, dtype=jnp.int32) * tm
    tile_expert = jnp.minimum(jnp.searchsorted(pad_end, tile_start, side="right"), n_exp - 1)
    dest = jnp.zeros((n,), jnp.int32).at[order].set(dest_sorted).reshape(m, k)
    return src_token, row_gate, tile_expert.astype(jnp.int32), dest


def _merge_kernel(yr_ref, yf_ref, yc_ref, lg_ref, pr_ref, pf_ref, pc_ref,
                  gwr_ref, gwf_ref, gwc_ref, gbr_ref, gbf_ref, gbc_ref, o_ref):
    lg = lg_ref[...]

    def branch(y_ref, p_ref, gw_ref, gb_ref):
        gate = jax.nn.sigmoid(jnp.dot(lg, gw_ref[...], preferred_element_type=F32) + gb_ref[...])
        return gate * jnp.dot(y_ref[...], p_ref[...], preferred_element_type=F32)

    out = (branch(yr_ref, pr_ref, gwr_ref, gbr_ref) + branch(yf_ref, pf_ref, gwf_ref, gbf_ref)
           + branch(yc_ref, pc_ref, gwc_ref, gbc_ref))
    o_ref[...] = out.astype(o_ref.dtype)


def gated_merge(yr, yf, yc, lg, pr, pf, pc, gw2, gb, layer):
    m = yr.shape[0]
    d = pr.shape[-1]
    tm = _pick(m, 1024, 16)
    tn = _pick(d, 512, LANE)
    nj = d // tn

    def rows(arr):
        return pl.BlockSpec((tm, arr.shape[1]), lambda i, j: (i, 0))

    def cols(arr, off):
        return pl.BlockSpec((None, arr.shape[1], tn), lambda i, j: (layer, 0, off * nj + j))

    return pl.pallas_call(
        _merge_kernel,
        grid=(m // tm, nj),
        in_specs=[rows(yr), rows(yf), rows(yc), rows(lg), cols(pr, 0), cols(pf, 0), cols(pc, 0),
                  cols(gw2, 0), cols(gw2, 1), cols(gw2, 2), cols(gb, 0), cols(gb, 1), cols(gb, 2)],
        out_specs=pl.BlockSpec((tm, tn), lambda i, j: (i, j)),
        out_shape=jax.ShapeDtypeStruct((m, d), BF16),
        compiler_params=_params(("parallel", "parallel")),
        name="gated_merge",
    )(yr, yf, yc, lg, pr, pf, pc, gw2, gw2, gw2, gb, gb, gb)


def _bf(x):
    return x.astype(BF16)


def _dot(a, b):
    return jnp.dot(_bf(a), _bf(b), preferred_element_type=F32)


def _dot_nt(a, b):
    return lax.dot_general(_bf(a), _bf(b), (((1,), (1,)), ((), ())), preferred_element_type=F32)


def _dot_tn(a, b):
    return lax.dot_general(_bf(a), _bf(b), (((0,), (0,)), ((), ())), preferred_element_type=F32)


def _each(f, *lists):
    return [f(*xs) for xs in zip(*lists)]


def _block_diag(x, width, pack):
    if pack == 1:
        return x
    lane_g = lax.broadcasted_iota(jnp.int32, x.shape, 1) // width
    zero = jnp.zeros_like(x)
    return jnp.concatenate([jnp.where(lane_g == g, x, zero) for g in range(pack)], axis=0)


def _unit_tri_inverse(a_tri, row, col, size, pack):
    def same_block(shift):
        return (row >> shift) == (col >> shift)

    def idot(a, b):
        return _dot(a, _block_diag(_bf(b), size, pack))

    eye = (row == col).astype(F32)
    a8 = _each(lambda a: jnp.where(same_block(3), a, 0.0), a_tri)
    a8_2 = _each(idot, a8, a8)
    a8_4 = _each(idot, a8_2, a8_2)
    x = _each(lambda a: eye - a, a8)
    x = _each(lambda xx, d: xx + d, x, _each(idot, x, a8_2))
    x = _each(lambda xx, d: xx + d, x, _each(idot, x, a8_4))
    shift = 3
    while (1 << shift) < size:
        off = same_block(shift + 1) & jnp.logical_not(same_block(shift))
        e = _each(lambda a: jnp.where(off, a, 0.0), a_tri)
        ex = _each(idot, e, x)
        x = _each(lambda xx, d: xx - d, x, _each(idot, x, ex))
        shift += 1
    return x


def _wkv_kernel(*refs, chunk, hd, heads, pack, dirs, nc):
    nd = len(dirs)
    ins = [refs[6 * d:6 * d + 6] for d in range(nd)]
    s0_refs = refs[6 * nd:7 * nd]
    y_refs = refs[7 * nd:8 * nd]
    sf_refs = refs[8 * nd:9 * nd]
    st_ref, al_ref, be_ref, rt_ref, kt_ref, vb_ref, pl_ref = refs[9 * nd:]
    c = pl.program_id(2)

    @pl.when(c == 0)
    def _():
        for d in range(nd):
            st_ref[d] = s0_refs[d][0]

    row = lax.broadcasted_iota(jnp.int32, (chunk, pack * chunk), 0)
    col = lax.broadcasted_iota(jnp.int32, (chunk, pack * chunk), 1) % chunk
    width = pack * hd
    groups = heads // pack
    sls = [slice(g * width, (g + 1) * width) for g in range(groups)]
    strict, incl = [], []
    for d, reverse in enumerate(dirs):
        st_d, in_d = (col > row, col >= row) if reverse else (col < row, col <= row)
        strict += [st_d] * groups
        incl += [in_d] * groups
        tri = jnp.where(in_d[:, :chunk], 1.0, 0.0).astype(BF16)
        last = 0 if reverse else chunk - 1
        r_ref, lw_ref, k_ref, v_ref, kk_ref, a_ref = ins[d]

        lw = lw_ref[0]
        lw_hi = _bf(lw)
        rem = lw - lw_hi.astype(F32)
        lw_mid = _bf(rem)
        lw_lo = _bf(rem - lw_mid.astype(F32))
        cum = (jnp.dot(tri, lw_hi, preferred_element_type=F32)
               + jnp.dot(tri, lw_mid, preferred_element_type=F32)
               + jnp.dot(tri, lw_lo, preferred_element_type=F32))
        p_in = jnp.exp(cum)
        p_inv = jnp.exp(-cum)
        kk = kk_ref[0].astype(F32)
        rt_ref[d] = _bf(r_ref[0].astype(F32) * p_in)
        kt_ref[d] = _bf(k_ref[0].astype(F32) * p_inv)
        be_ref[d] = _bf(a_ref[0].astype(F32) * kk * p_inv)
        al_ref[d] = _bf(kk * jnp.exp(cum - lw))
        vb_ref[d] = _bf(v_ref[0])
        pl_ref[d] = jnp.broadcast_to(p_in[last:last + 1, :], pl_ref.shape[1:])

    ent = [(d, g) for d in range(nd) for g in range(groups)]
    alpha = [al_ref[d, :, sls[g]] for d, g in ent]
    beta = [be_ref[d, :, sls[g]] for d, g in ent]
    rt = [rt_ref[d, :, sls[g]] for d, g in ent]
    kt = [kt_ref[d, :, sls[g]] for d, g in ent]
    v = [vb_ref[d, :, sls[g]] for d, g in ent]
    st0 = [st_ref[d, g] for d, g in ent]
    st0_b = _each(_bf, st0)

    def bd_keys(x):
        return _block_diag(x, hd, pack)

    def bdot(a, x):
        return _dot(a, _block_diag(_bf(x), hd, pack))

    ar = _each(lambda x, y: jnp.concatenate([x, y], axis=0), alpha, rt)
    x_b = _each(_dot_nt, ar, _each(bd_keys, beta))
    x_k = _each(_dot_nt, ar, _each(bd_keys, kt))
    a_ab = _each(lambda x, m: jnp.where(m, x[:chunk], 0.0), x_b, strict)
    a_rb = _each(lambda x, m: jnp.where(m, x[chunk:], 0.0), x_b, incl)
    a_ak = _each(lambda x, m: jnp.where(m, x[:chunk], 0.0), x_k, strict)
    a_rk = _each(lambda x, m: jnp.where(m, x[chunk:], 0.0), x_k, incl)
    t_inv = _unit_tri_inverse(a_ab, row, col, chunk, pack)

    w_t = _each(bdot, t_inv, alpha)
    u0 = _each(bdot, t_inv, _each(bdot, a_ak, v))
    y0 = _each(bdot, a_rk, v)
    ktv = _each(_dot_tn, kt, v)
    u = _each(lambda x, y: x + y, _each(_dot, w_t, st0_b), u0)
    y1 = _each(_dot, rt, st0_b)
    y2 = _each(bdot, a_rb, u)
    btu = _each(_dot_tn, beta, u)
    p_col = [jnp.transpose(pl_ref[d, :, sls[g]])[:, :1] for d, g in ent]
    own = (lax.broadcasted_iota(jnp.int32, (width, width), 0) // hd
           == lax.broadcasted_iota(jnp.int32, (width, width), 1) // hd)
    for n, (d, g) in enumerate(ent):
        y_refs[d][0, :, sls[g]] = y0[n] + y1[n] - y2[n]
        st_ref[d, g] = (st0[n] + jnp.where(own, ktv[n] - btu[n], 0.0)) * p_col[n]

    @pl.when(c == nc - 1)
    def _():
        for d in range(nd):
            sf_refs[d][0] = st_ref[d]


def wkv_pack(nh, hd):
    pack = LANE // hd if hd < LANE and LANE % hd == 0 and hd == WKV_CHUNK else 1
    return pack if nh % pack == 0 else 1


def wkv_state_shape(b, nh, hd):
    pack = wkv_pack(nh, hd)
    return (b, nh // pack, pack * hd, pack * hd)


def wkv(r, v, kk, per_dir, s0, hd, dirs=(False, True)):
    b, t, width = r.shape
    nh = width // hd
    pack = wkv_pack(nh, hd)
    heads = max(hh for hh in (1, 2, 4, 8, 16, 32) if nh % hh == 0 and hh <= WKV_HEADS and hh % pack == 0)
    chunk = WKV_CHUNK
    nc = t // chunk
    blk = heads * hd
    groups = heads // pack
    gw = pack * hd
    nd = len(dirs)

    def tok_spec(reverse):
        return pl.BlockSpec((1, chunk, blk), lambda bi, hi, ci: (bi, nc - 1 - ci if reverse else ci, hi))

    st_spec = pl.BlockSpec((1, groups, gw, gw), lambda bi, hi, ci: (bi, hi, 0, 0))
    in_specs, args = [], []
    for d, reverse in enumerate(dirs):
        lw, k, a = per_dir[d]
        in_specs += [tok_spec(reverse)] * 6
        args += [r, lw, k, v, kk, a]
    in_specs += [st_spec] * nd
    args += list(s0)
    outs = pl.pallas_call(
        functools.partial(_wkv_kernel, chunk=chunk, hd=hd, heads=heads, pack=pack, dirs=tuple(dirs), nc=nc),
        grid=(b, nh // heads, nc),
        in_specs=in_specs,
        out_specs=[tok_spec(reverse) for reverse in dirs] + [st_spec] * nd,
        out_shape=[jax.ShapeDtypeStruct((b, t, width), F32)] * nd
        + [jax.ShapeDtypeStruct(wkv_state_shape(b, nh, hd), F32)] * nd,
        scratch_shapes=[pltpu.VMEM((nd, groups, gw, gw), F32)]
        + [pltpu.VMEM((nd, chunk, blk), BF16)] * 5 + [pltpu.VMEM((nd, 8, blk), F32)],
        compiler_params=pltpu.CompilerParams(
            dimension_semantics=("parallel", "parallel", "arbitrary"),
            vmem_limit_bytes=VMEM_LIMIT),
        name="wkv",
    )(*args)
    return outs[:nd], outs[nd:]


def _from_prev(x):
    return pltpu.roll(x, 1, axis=0)


def _from_next(x):
    return pltpu.roll(x, x.shape[0] - 1, axis=0)


def _token_shift(x, up, dn, has_up, has_dn, mode):
    rows, ch = x.shape
    if mode == "seq":
        half = ch // 2
        t = lax.broadcasted_iota(jnp.int32, (rows, half), 0)
        s0 = jnp.where(t == 0, 0.0, _from_prev(x[:, :half]))
        s1 = jnp.where(t == rows - 1, 0.0, _from_next(x[:, half:]))
        return [s0, s1]
    q = ch // 4
    t = lax.broadcasted_iota(jnp.int32, (rows, q), 0) % GRID_W
    s0 = jnp.where(t == 0, 0.0, _from_prev(x[:, :q]))
    s1 = jnp.where(t == GRID_W - 1, 0.0, _from_next(x[:, q:2 * q]))
    up = jnp.where(has_up, up, 0.0)
    dn = jnp.where(has_dn, dn, 0.0)
    if rows == GRID_W:
        s2, s3 = up, dn
    else:
        s2 = jnp.concatenate([up, x[:rows - GRID_W, 2 * q:3 * q]], axis=0)
        s3 = jnp.concatenate([x[GRID_W:, 3 * q:], dn], axis=0)
    return [s0, s1, s2, s3]


def _prep_kernel(*refs, mode, nblk):
    if mode == "grid":
        x_ref, up_ref, dn_ref, w_ref, sh_ref, sc_ref, o_ref = refs
    else:
        x_ref, w_ref, sh_ref, sc_ref, o_ref = refs
    i = pl.program_id(1)
    w, sh, sc = w_ref[...], sh_ref[0], sc_ref[0]

    def modulated(x):
        y = x * lax.rsqrt(jnp.mean(x * x, axis=-1, keepdims=True) + NORM_EPS)
        return (y * w) * (1 + sc) + sh

    h = modulated(x_ref[0])
    d = h.shape[1]
    if mode == "grid":
        q = d // 4
        up = modulated(up_ref[0])[:, 2 * q:3 * q]
        dn = modulated(dn_ref[0])[:, 3 * q:]
        parts = _token_shift(h, up, dn, i > 0, i < nblk - 1, mode)
    else:
        parts = _token_shift(h, None, None, None, None, mode)
    o_ref[:, :d] = h.astype(o_ref.dtype)
    width = d // len(parts)
    for n, s in enumerate(parts):
        lo = n * width
        o_ref[:, d + lo:d + lo + width] = (s - h[:, lo:lo + width]).astype(o_ref.dtype)


def prep(x, w, shift, scale, mode):
    b, t, d = x.shape
    rows = _pick(t, PREP_ROWS, GRID_W) if mode == "grid" else t
    nblk = t // rows
    per = rows // GRID_W
    x_spec = pl.BlockSpec((1, rows, d), lambda bi, i: (bi, i, 0))
    vec_spec = pl.BlockSpec((1, d), lambda bi, i: (0, 0))
    mod_spec = pl.BlockSpec((1, 1, d), lambda bi, i: (bi, 0, 0))
    in_specs, args = [x_spec], [x]
    if mode == "grid":
        last = t // GRID_W - 1
        in_specs += [pl.BlockSpec((1, GRID_W, d), lambda bi, i: (bi, jnp.maximum(i * per - 1, 0), 0)),
                     pl.BlockSpec((1, GRID_W, d), lambda bi, i: (bi, jnp.minimum((i + 1) * per, last), 0))]
        args += [x, x]
    in_specs += [vec_spec, mod_spec, mod_spec]
    args += [w[None, :], shift, scale]
    return pl.pallas_call(
        functools.partial(_prep_kernel, mode=mode, nblk=nblk),
        grid=(b, nblk),
        in_specs=in_specs,
        out_specs=pl.BlockSpec((rows, 2 * d), lambda bi, i: (bi * nblk + i, 0)),
        out_shape=jax.ShapeDtypeStruct((b * t, 2 * d), BF16),
        compiler_params=_params(("parallel", "parallel")),
        name="prep_" + mode,
    )(*args)


def _norm_mod_kernel(x_ref, w_ref, sh_ref, sc_ref, *rest, routed):
    x = x_ref[...]
    y = x * lax.rsqrt(jnp.mean(x * x, axis=-1, keepdims=True) + NORM_EPS)
    h = (y * w_ref[...]) * (1 + sc_ref[0]) + sh_ref[0]
    if routed:
        rw_ref, rb_ref, o_ref, lg_ref = rest
        lg_ref[...] = jnp.dot(h, rw_ref[...], precision=HI, preferred_element_type=F32) + rb_ref[...]
    else:
        (o_ref,) = rest
    o_ref[...] = h.astype(o_ref.dtype)


def norm_mod(x, w, shift, scale, rows_per_batch, router_w=None, router_b=None):
    m, d = x.shape
    rows = _pick(rows_per_batch, PREP_ROWS, 8)
    nblk = rows_per_batch // rows
    routed = router_w is not None
    x_spec = pl.BlockSpec((rows, d), lambda bi, i: (bi * nblk + i, 0))
    mod_spec = pl.BlockSpec((1, 1, d), lambda bi, i: (bi, 0, 0))
    in_specs = [x_spec, pl.BlockSpec((1, d), lambda bi, i: (0, 0)), mod_spec, mod_spec]
    args = [x, w[None, :], shift, scale]
    out_specs, out_shape = [x_spec], [jax.ShapeDtypeStruct((m, d), BF16)]
    if routed:
        n_e = router_w.shape[1]
        pad = -n_e % LANE
        in_specs += [pl.BlockSpec((d, n_e + pad), lambda bi, i: (0, 0)),
                     pl.BlockSpec((1, n_e + pad), lambda bi, i: (0, 0))]
        args += [jnp.pad(router_w, ((0, 0), (0, pad))), jnp.pad(router_b, (0, pad))[None, :]]
        out_specs.append(pl.BlockSpec((rows, n_e + pad), lambda bi, i: (bi * nblk + i, 0)))
        out_shape.append(jax.ShapeDtypeStruct((m, n_e + pad), F32))
    out = pl.pallas_call(
        functools.partial(_norm_mod_kernel, routed=routed),
        grid=(m // rows_per_batch, nblk),
        in_specs=in_specs,
        out_specs=out_specs,
        out_shape=out_shape,
        compiler_params=_params(("parallel", "parallel")),
        name="norm_mod",
    )(*args)
    return (out[0], out[1][:, :router_w.shape[1]]) if routed else (out[0], None)


def _head_sums(x, hd):
    rows, width = x.shape
    nt = width // LANE
    stacked = jnp.concatenate([x[:, i * LANE:(i + 1) * LANE] for i in range(nt)], axis=0)
    li = lax.broadcasted_iota(jnp.int32, (LANE, LANE), 0) // hd
    lj = lax.broadcasted_iota(jnp.int32, (LANE, LANE), 1) // hd
    ones = jnp.where(li == lj, 1.0, 0.0).astype(BF16)
    hi = _bf(stacked)
    rem = stacked - hi.astype(F32)
    mid = _bf(rem)
    lo = _bf(rem - mid.astype(F32))
    s = (jnp.dot(hi, ones, preferred_element_type=F32) + jnp.dot(mid, ones, preferred_element_type=F32)
         + jnp.dot(lo, ones, preferred_element_type=F32))
    return jnp.concatenate([s[i * rows:(i + 1) * rows] for i in range(nt)], axis=1)


def _wkv_prep_kernel(*refs, mode, nblk, hd, offs, has_vres):
    refs = list(refs)
    main = [refs.pop(0) for _ in range(3)]
    halo = [(refs.pop(0), refs.pop(0)) for _ in range(3)] if mode == "grid" else [(None, None)] * 3
    low_ref = refs.pop(0)
    vf_ref = refs.pop(0) if has_vres else None
    mu_ref, w0_ref, a0_ref, kk_ref, ka_ref, w2_ref, a2_ref = [refs.pop(0) for _ in range(7)]
    if has_vres:
        v0_ref, v2_ref = refs.pop(0), refs.pop(0)
    r_o, v_o, kk_o, kdf_o, kdr_o, af_o, ar_o, lwf_o, lwr_o = refs
    c = pl.program_id(1)
    low = low_ref[0]

    def lerp(n):
        x = main[n][0].astype(F32)
        up, dn = ((halo[n][0][0].astype(F32), halo[n][1][0].astype(F32)) if mode == "grid"
                  else (None, None))
        sh = jnp.concatenate(_token_shift(x, up, dn, c > 0, c < nblk - 1, mode), axis=1)
        return x + (sh - x) * mu_ref[n:n + 1, :]

    def low_dot(lo, hi_, w, act=None):
        z = low[:, lo:hi_]
        if act is not None:
            z = act(z)
        return jnp.dot(_bf(z), w, preferred_element_type=F32)

    r = lerp(0)
    k = lerp(1)
    v = lerp(2)
    if has_vres:
        gate = jax.nn.sigmoid(v0_ref[...] + low_dot(offs[6], offs[7], v2_ref[...]))
        v = v + (vf_ref[0].astype(F32) - v) * gate
    kq = k * kk_ref[...]
    kk = kq * lax.rsqrt(jnp.maximum(_head_sums(kq * kq, hd), 1e-24))
    r_o[0] = r.astype(r_o.dtype)
    v_o[0] = v.astype(v_o.dtype)
    kk_o[0] = kk.astype(kk_o.dtype)
    for di, (kd_o, a_o, lw_o) in enumerate(((kdf_o, af_o, lwf_o), (kdr_o, ar_o, lwr_o))):
        wl = w0_ref[di:di + 1, :] + low_dot(offs[di], offs[di + 1], w2_ref[di], jnp.tanh)
        softplus = jnp.maximum(-wl, 0.0) + jnp.log1p(jnp.exp(-jnp.abs(wl)))
        lw_o[0] = -jnp.exp(-softplus - DECAY_OFFSET)
        a = jax.nn.sigmoid(a0_ref[di:di + 1, :] + low_dot(offs[2 + di], offs[3 + di], a2_ref[di]))
        a_o[0] = a.astype(a_o.dtype)
        kd_o[0] = (k * (1.0 + (a - 1.0) * ka_ref[...])).astype(kd_o.dtype)


def wkv_prep(p, low, v_first, mu, w0, a0, k_k, k_a, w2, a2, v0, v2, offs, hd, mode):
    b, t, rw3 = p.shape
    rw = rw3 // 3
    q = rw // 4
    rows = GRID_W if mode == "grid" else t
    nblk = t // rows
    has_vres = v_first is not None

    def tok(width, col):
        return pl.BlockSpec((1, rows, width), lambda bi, ci: (bi, ci, col))

    def full(arr):
        nd = arr.ndim
        return pl.BlockSpec(arr.shape, lambda bi, ci: (0,) * nd)

    in_specs = [tok(rw, n) for n in range(3)]
    args = [p, p, p]
    if mode == "grid":
        for n in range(3):
            in_specs += [pl.BlockSpec((1, rows, q), lambda bi, ci, n=n: (bi, jnp.maximum(ci - 1, 0), 4 * n + 2)),
                         pl.BlockSpec((1, rows, q), lambda bi, ci, n=n: (bi, jnp.minimum(ci + 1, nblk - 1), 4 * n + 3))]
            args += [p, p]
    in_specs.append(tok(low.shape[-1], 0))
    args.append(low)
    if has_vres:
        in_specs.append(tok(rw, 0))
        args.append(v_first)
    small = [mu, w0, a0, k_k[None, :], k_a[None, :], w2.astype(BF16), a2.astype(BF16)]
    if has_vres:
        small += [v0[None, :], v2.astype(BF16)]
    in_specs += [full(s) for s in small]
    args += small
    out_spec = tok(rw, 0)
    shapes = [jax.ShapeDtypeStruct((b, t, rw), BF16)] * 7 + [jax.ShapeDtypeStruct((b, t, rw), F32)] * 2
    return pl.pallas_call(
        functools.partial(_wkv_prep_kernel, mode=mode, nblk=nblk, hd=hd, offs=tuple(offs), has_vres=has_vres),
        grid=(b, nblk),
        in_specs=in_specs,
        out_specs=[out_spec] * 9,
        out_shape=shapes,
        compiler_params=_params(("parallel", "parallel")),
        name="wkv_prep_" + mode,
    )(*args)


def _wkv_post_kernel(yf_ref, yr_ref, r_ref, kdf_ref, kdr_ref, v_ref, lg_ref, rk_ref, gw_ref, gb_ref,
                     g2_ref, o_ref, *, hd):
    y = yf_ref[0] + yr_ref[0]
    mean = _head_sums(y, hd) * (1.0 / hd)
    yc = y - mean
    var = _head_sums(yc * yc, hd) * (1.0 / hd)
    y = yc * lax.rsqrt(var + GN_EPS) * gw_ref[...] + gb_ref[...]
    rk = r_ref[0].astype(F32) * (kdf_ref[0].astype(F32) + kdr_ref[0].astype(F32)) * rk_ref[...]
    y = y + _head_sums(rk, hd) * v_ref[0].astype(F32)
    g = jnp.dot(_bf(jax.nn.sigmoid(lg_ref[0])), g2_ref[...], preferred_element_type=F32)
    o_ref[0] = (y * g).astype(o_ref.dtype)


def wkv_post(y_f, y_r, r, kd_f, kd_r, v, low_g, r_k, gn_w, gn_b, g2, hd):
    b, t, rw = y_f.shape
    rows = _pick(t, POST_ROWS, 8)

    def tok(width):
        return pl.BlockSpec((1, rows, width), lambda bi, ci: (bi, ci, 0))

    def full(arr):
        return pl.BlockSpec(arr.shape, lambda bi, ci: (0, 0))

    small = [r_k.reshape(1, rw), gn_w[None, :], gn_b[None, :], g2.astype(BF16)]
    return pl.pallas_call(
        functools.partial(_wkv_post_kernel, hd=hd),
        grid=(b, t // rows),
        in_specs=[tok(rw)] * 6 + [tok(low_g.shape[-1])] + [full(s) for s in small],
        out_specs=tok(rw),
        out_shape=jax.ShapeDtypeStruct((b, t, rw), BF16),
        compiler_params=_params(("parallel", "parallel")),
        name="wkv_post",
    )(y_f, y_r, r, kd_f, kd_r, v, low_g, *small)


def rmsnorm(x, w):
    y = x * lax.rsqrt(jnp.mean(x * x, axis=-1, keepdims=True) + NORM_EPS)
    return y * w


def modulate(x, w, shift, scale):
    return rmsnorm(x, w) * (1 + scale) + shift


def dft_tables(rows, cols, n, scale):
    j = lax.broadcasted_iota(jnp.int32, (rows, cols), 0)
    k = lax.broadcasted_iota(jnp.int32, (rows, cols), 1)
    ang = ((j * k) % n).astype(F32) * (2.0 * math.pi / n)
    return jnp.cos(ang) * scale, jnp.sin(ang) * scale


def _dft_stage_kernel(l_ref, zr_ref, zi_ref, *rest, twiddle):
    z = jnp.concatenate([zr_ref[...], zi_ref[...]], axis=0)
    y = jnp.dot(l_ref[...], z, preferred_element_type=F32)
    if not twiddle:
        rest[0][...] = y.astype(rest[0].dtype)
        return
    twr_ref, twi_ref, o_ref = rest
    half = y.shape[0] // 2
    reps = y.shape[1] // LANE
    tr, ti = jnp.tile(twr_ref[...], (1, reps)), jnp.tile(twi_ref[...], (1, reps))
    yr, yi = y[:half], y[half:]
    o_ref[0] = (yr * tr - yi * ti).astype(o_ref.dtype)
    o_ref[1] = (yr * ti + yi * tr).astype(o_ref.dtype)


class PosDft:
    def __init__(self, n):
        self.n = n
        self.n1 = DFT_N1 if n % DFT_N1 == 0 and n // DFT_N1 >= 8 and (n // DFT_N1) % 8 == 0 else 0
        if not self.n1:
            c, s = dft_tables(n, n, n, n ** -0.5)
            self.direct = jnp.concatenate([c, s], axis=1).astype(BF16)
            return
        n1, n2 = self.n1, n // self.n1
        self.n2 = n2
        c1, s1 = dft_tables(n1, n1, n1, n1 ** -0.5)
        self.l1 = jnp.concatenate([jnp.concatenate([c1, s1], axis=1),
                                   jnp.concatenate([-s1, c1], axis=1)], axis=0).astype(BF16)
        tc, ts = dft_tables(n2, n1, n, 1.0)
        self.twr = jnp.broadcast_to(tc[:, :, None], (n2, n1, LANE))
        self.twi = jnp.broadcast_to(-ts[:, :, None], (n2, n1, LANE))
        c2, s2 = dft_tables(n2, n2, n2, n2 ** -0.5)
        self.l3 = jnp.concatenate([c2, s2], axis=1).astype(BF16)


def fourier_mix(u, pos, chan_table):
    b, n, fw = u.shape
    z = matmul(u.reshape(b * n, fw).astype(BF16), chan_table, BF16, planes=2)
    if not pos.n1:
        z = z.reshape(2, b, n, fw).transpose(1, 0, 2, 3).reshape(b, 2 * n, fw)
        return jnp.stack([matmul(pos.direct, z[i], BF16) for i in range(b)], axis=0)
    n1, n2 = pos.n1, pos.n2
    z = z.reshape(2, b, n1, n2 * fw)
    y = pl.pallas_call(
        functools.partial(_dft_stage_kernel, twiddle=True),
        grid=(b, n2),
        in_specs=[pl.BlockSpec((2 * n1, 2 * n1), lambda bi, j: (0, 0)),
                  pl.BlockSpec((None, None, n1, fw), lambda bi, j: (0, bi, 0, j)),
                  pl.BlockSpec((None, None, n1, fw), lambda bi, j: (1, bi, 0, j)),
                  pl.BlockSpec((None, n1, LANE), lambda bi, j: (j, 0, 0)),
                  pl.BlockSpec((None, n1, LANE), lambda bi, j: (j, 0, 0))],
        out_specs=pl.BlockSpec((None, 2, None, n1, fw), lambda bi, j: (bi, 0, j, 0, 0)),
        out_shape=jax.ShapeDtypeStruct((b, 2, n2, n1, fw), BF16),
        compiler_params=_params(("parallel", "parallel")),
        name="dft_stage1",
    )(pos.l1, z, z, pos.twr, pos.twi)
    y = y.reshape(b, 2, n2, n1 * fw)
    tn = _pick(n1 * fw, 4096, LANE)
    out = pl.pallas_call(
        functools.partial(_dft_stage_kernel, twiddle=False),
        grid=(b, n1 * fw // tn),
        in_specs=[pl.BlockSpec((n2, 2 * n2), lambda bi, j: (0, 0)),
                  pl.BlockSpec((None, None, n2, tn), lambda bi, j: (bi, 0, 0, j)),
                  pl.BlockSpec((None, None, n2, tn), lambda bi, j: (bi, 1, 0, j))],
        out_specs=pl.BlockSpec((None, n2, tn), lambda bi, j: (bi, 0, j)),
        out_shape=jax.ShapeDtypeStruct((b, n2, n1 * fw), BF16),
        compiler_params=_params(("parallel", "parallel")),
        name="dft_stage2",
    )(pos.l3, y, y)
    return out.reshape(b, n, fw)


def depthwise_conv(u, w):
    n = u.shape[1]
    kk = w.shape[0]
    pad = kk // 2
    up = jnp.pad(u, ((0, 0), (pad, pad), (0, 0)))
    out = up[:, 0:n] * w[0]
    for i in range(1, kk):
        out = out + up[:, i:i + n] * w[i]
    return out


def kernel(x, c, ctx, c_ctx, ada_a, ada_b, ada_bias, norm1_w, norm2_w, w_in, mu_rkv, mu_lr, decay_w0, decay_w1, decay_w2, iclr_a0, iclr_a1, iclr_a2, ogate_g1, ogate_g2, k_k, k_a, r_k, gn_w, gn_b, vres_mu, vres_v0, vres_v1, vres_v2, conv_w, gate_w1, gate_w2, gate_b, proj_rwkv, proj_fourier, proj_conv, w_out, ffn_w1, ffn_w3, ffn_w2, router_w, router_b, moe_w1, moe_w3, moe_w2, final_norm_w):
    depth = w_in.shape[0]
    bsz, seq, d = x.shape
    ctx_len = ctx.shape[1]
    rw = mu_rkv.shape[-1]
    nh, hd = r_k.shape[1], r_k.shape[2]
    fw = proj_fourier.shape[1]
    cw = proj_conv.shape[1]
    four_off = 3 * rw
    conv_off = four_off + fw
    n_exp = router_w.shape[-1]
    d_exp = moe_w1.shape[-1]
    n_mod = ada_b.shape[-1] // d
    r_dec, r_icl, r_og, r_vr, r_gate = (decay_w1.shape[-1], iclr_a1.shape[-1], ogate_g1.shape[-1],
                                        vres_v1.shape[-1], gate_w1.shape[-1])

    gc = fw // FOURIER_GROUPS
    cc, cs = dft_tables(gc, gc, gc, gc ** -0.5)
    eye_g = jnp.eye(FOURIER_GROUPS, dtype=F32)
    chan_table = jnp.concatenate([jnp.kron(eye_g, cc), -jnp.kron(eye_g, cs)], axis=1).astype(BF16)
    pos_lat, pos_ctx = PosDft(seq), PosDft(ctx_len)

    w_in_b, w_out_b = w_in.astype(BF16), w_out.astype(BF16)
    proj_r_b, proj_f_b, proj_c_b = proj_rwkv.astype(BF16), proj_fourier.astype(BF16), proj_conv.astype(BF16)
    gate_w2_b, gate_b3 = gate_w2.astype(BF16), gate_b[:, None, :]
    ffn_w1_b, ffn_w3_b, ffn_w2_b = ffn_w1.astype(BF16), ffn_w3.astype(BF16), ffn_w2.astype(BF16)
    moe_w1_b = moe_w1.astype(BF16).reshape((-1,) + moe_w1.shape[2:])
    moe_w3_b = moe_w3.astype(BF16).reshape((-1,) + moe_w3.shape[2:])
    moe_w2_b = moe_w2.astype(BF16).reshape(moe_w2.shape[0], n_exp * d_exp, d)
    moe_w2_e = moe_w2_b.reshape(-1, d_exp, d)

    cond_lat = jax.nn.silu(c)
    cond_ctx = jax.nn.silu(c_ctx)[None, :]
    cond = jnp.concatenate([cond_lat, cond_ctx], axis=0)
    cond = jnp.pad(cond, ((0, 16 - cond.shape[0] % 16), (0, 0)))

    def tokens(xs, mod, l, mode, pos_tab, s0, v_first, last_ctx):
        b, t, _ = xs.shape
        has_vres = l > 0
        hd_cat = prep(xs, norm1_w[l], jnp.broadcast_to(mod[0], (b, 1, d)),
                      jnp.broadcast_to(mod[1], (b, 1, d)), mode)

        w_h = [decay_w1[l, 0], decay_w1[l, 1], iclr_a1[l, 0], iclr_a1[l, 1], ogate_g1[l], gate_w1[l]]
        mus = [mu_lr[l, 0], mu_lr[l, 0], mu_lr[l, 1], mu_lr[l, 1], mu_lr[l, 2], None]
        if has_vres:
            w_h.append(vres_v1[l - 1])
            mus.append(vres_mu[l - 1])
        w_dh = [jnp.zeros_like(w) if m is None else w * m[:, None] for w, m in zip(w_h, mus)]
        w_low = jnp.concatenate([jnp.concatenate(w_h, axis=1), jnp.concatenate(w_dh, axis=1)], axis=0)
        n_low = w_low.shape[1]
        w_low = jnp.pad(w_low, ((0, 0), (0, -n_low % LANE)))
        low = matmul(hd_cat, w_low.astype(BF16)).reshape(b, t, -1)
        offs = [0]
        for w in w_h:
            offs.append(offs[-1] + w.shape[1])
        low_g = low[..., offs[4]:offs[5]]
        low_gate = low[..., offs[5]:offs[6]]

        p = matmul(hd_cat, w_in_b, BF16, n_cols=four_off, layer=l).reshape(b, t, -1)
        p_fc = matmul(hd_cat, w_in_b, BF16, n_cols=w_in_b.shape[-1] - four_off,
                      col_off=four_off, layer=l).reshape(b, t, -1)

        r, v, kk, kd_f, kd_r, a_f, a_r, lw_f, lw_r = wkv_prep(
            p, low, v_first, mu_rkv[l], decay_w0[l], iclr_a0[l], k_k[l], k_a[l], decay_w2[l], iclr_a2[l],
            vres_v0[l - 1] if has_vres else None, vres_v2[l - 1] if has_vres else None, offs, hd, mode)

        if s0 is None:
            s0 = [jnp.zeros(wkv_state_shape(b, nh, hd), F32)] * 2
        ys, states = wkv(r, v, kk, [(lw_f, kd_f, a_f), (lw_r, kd_r, a_r)], s0, hd)
        if last_ctx:
            return None, states, v

        y_rwkv = wkv_post(ys[0], ys[1], r, kd_f, kd_r, v, low_g, r_k[l], gn_w[l], gn_b[l],
                          ogate_g2[l], hd)

        y_four = fourier_mix(p_fc[..., :fw], pos_tab, chan_table)
        gate_b_ = p_fc[..., fw:fw + cw].astype(F32)
        gate_c_ = p_fc[..., fw + cw:fw + 2 * cw].astype(F32)
        uu = p_fc[..., fw + 2 * cw:].astype(F32)
        y_conv = gate_b_ * depthwise_conv(gate_c_ * uu, conv_w[l])

        rows = b * t

        def flat(z):
            return z.reshape(rows, -1).astype(BF16)

        merged = gated_merge(flat(y_rwkv), flat(y_four), flat(y_conv), flat(low_gate),
                             proj_r_b, proj_f_b, proj_c_b, gate_w2_b, gate_b3, l)
        xs = resid_matmul(merged, w_out_b, xs.reshape(rows, d),
                          jnp.broadcast_to(mod[2], (b, 1, d)), t, l)

        j = l // 2
        dense = l % 2 == 0
        h2, logits = norm_mod(xs, norm2_w[l], jnp.broadcast_to(mod[3], (b, 1, d)),
                              jnp.broadcast_to(mod[4], (b, 1, d)), t,
                              None if dense else router_w[j], None if dense else router_b[j])
        if dense:
            hid = swiglu_up(h2, ffn_w1_b, ffn_w3_b, e0=j, n_e=1)
            w2 = ffn_w2_b
        else:
            top_val, top_idx = lax.top_k(logits, TOP_K)
            weights = jax.nn.softmax(top_val, axis=-1)
            if rows >= MOE_SPARSE_MIN_ROWS:
                src, row_gate, tile_e, dest = moe_dispatch(top_idx, weights, n_exp, MOE_TILE)
                y = grouped_swiglu(h2[src], tile_e + j * n_exp,
                                   jnp.broadcast_to(row_gate[:, None], (row_gate.shape[0], LANE)),
                                   moe_w1_b, moe_w3_b, moe_w2_e, MOE_TILE)
                f = jnp.sum(y[dest].astype(F32), axis=1).reshape(b, t, d)
                return xs.reshape(b, t, d) + mod[5] * f, states, v
            gate = jnp.sum(jax.nn.one_hot(top_idx, n_exp, dtype=F32) * weights[..., None], axis=-2)
            gate_rep = jnp.repeat(gate, LANE, axis=-1)
            hid = swiglu_up(h2, moe_w1_b, moe_w3_b, gate_rep, e0=j * n_exp, n_e=n_exp)
            w2 = moe_w2_b
        xs = resid_matmul(hid, w2, xs, jnp.broadcast_to(mod[5], (b, 1, d)), t, j)
        return xs.reshape(b, t, d), states, v

    xl, xc = x, ctx
    v_first_l = v_first_c = None
    for l in range(depth):
        last = l == depth - 1
        m = mm(mm(cond, ada_a[l]), ada_b[l]) + ada_bias[l]
        mod_l = [m[:bsz, None, i * d:(i + 1) * d] for i in range(n_mod)]
        mod_c = [m[bsz:bsz + 1, None, i * d:(i + 1) * d] for i in range(n_mod)]

        xc_new, ctx_states, vc = tokens(xc, mod_c, l, "seq", pos_ctx, None, v_first_c, last)
        xl, _, vl = tokens(xl, mod_l, l, "grid", pos_lat, ctx_states, v_first_l, False)
        if l == 0:
            v_first_c, v_first_l = vc, vl
        if not last:
            xc = xc_new
    return rmsnorm(xl, final_norm_w)
```

```python
import functools
import math

import jax
import jax.numpy as jnp
from jax import lax
from jax.experimental import pallas as pl
from jax.experimental.pallas import tpu as pltpu

F32 = jnp.float32
BF16 = jnp.bfloat16

GRID_W = 64
FOURIER_GROUPS = 8
TOP_K = 2
NORM_EPS = 1e-6
GN_EPS = 64e-5
DECAY_OFFSET = 0.5
WKV_CHUNK = 64
WKV_HEADS = 32
MOE_TILE = 512
MOE_SPARSE_MIN_ROWS = 4096
DFT_N1 = 128
PREP_ROWS = 256
POST_ROWS = 256
LANE = 128
VMEM_LIMIT = 56 * 1024 * 1024

HI = lax.Precision.HIGHEST


def _pick(dim, target, align):
    if dim <= target:
        return dim
    t = (target // align) * align
    while t >= align:
        if dim % t == 0:
            return t
        t -= align
    return dim


def _mm_kernel(a_ref, b_ref, o_ref, acc_ref, *, nk):
    k = pl.program_id(2)

    @pl.when(k == 0)
    def _():
        acc_ref[...] = jnp.zeros_like(acc_ref)

    acc_ref[...] += jnp.dot(a_ref[...], b_ref[...], preferred_element_type=F32)

    @pl.when(k == nk - 1)
    def _():
        o_ref[...] = acc_ref[...].astype(o_ref.dtype)


def matmul(a, b, out_dtype=F32, tm=1024, tn=1024, tk=2048, n_cols=None, col_off=0, planes=1, layer=None):
    m = a.shape[0]
    k = b.shape[-2]
    n = b.shape[-1] if n_cols is None else n_cols
    tm = _pick(m, tm, 16)
    tn = _pick(math.gcd(n // planes, col_off) if col_off else n // planes, tn, LANE)
    tk = _pick(k, tk, LANE)
    nk = k // tk
    joff = col_off // tn
    if planes == 1:
        out_spec = pl.BlockSpec((tm, tn), lambda i, j, kk: (i, j))
        out_shape = jax.ShapeDtypeStruct((m, n), out_dtype)
    else:
        per = n // planes // tn
        out_spec = pl.BlockSpec((None, tm, tn), lambda i, j, kk: (j // per, i, j % per))
        out_shape = jax.ShapeDtypeStruct((planes, m, n // planes), out_dtype)
    if layer is None:
        b_spec = pl.BlockSpec((tk, tn), lambda i, j, kk: (kk, j + joff))
    else:
        b_spec = pl.BlockSpec((None, tk, tn), lambda i, j, kk: (layer, kk, j + joff))
    return pl.pallas_call(
        functools.partial(_mm_kernel, nk=nk),
        grid=(m // tm, n // tn, nk),
        in_specs=[pl.BlockSpec((tm, tk), lambda i, j, kk: (i, kk)), b_spec],
        out_specs=out_spec,
        out_shape=out_shape,
        scratch_shapes=[pltpu.VMEM((tm, tn), F32)],
        compiler_params=pltpu.CompilerParams(
            dimension_semantics=("parallel", "parallel", "arbitrary"),
            vmem_limit_bytes=VMEM_LIMIT),
        name="matmul",
    )(a, b)


def mm(a, b, out_dtype=F32, **kw):
    lead = a.shape[:-1]
    out = matmul(a.reshape(-1, a.shape[-1]).astype(BF16), b.astype(BF16), out_dtype, **kw)
    return out.reshape(lead + (b.shape[-1],))


def _params(sem):
    return pltpu.CompilerParams(dimension_semantics=sem, vmem_limit_bytes=VMEM_LIMIT)


def _swiglu_up_kernel(a_ref, w1_ref, w3_ref, *rest, nk, gated):
    if gated:
        g_ref, o_ref, acc1_ref, acc3_ref = rest
    else:
        o_ref, acc1_ref, acc3_ref = rest
    k = pl.program_id(2)

    @pl.when(k == 0)
    def _():
        acc1_ref[...] = jnp.zeros_like(acc1_ref)
        acc3_ref[...] = jnp.zeros_like(acc3_ref)

    a = a_ref[...]
    acc1_ref[...] += jnp.dot(a, w1_ref[...], preferred_element_type=F32)
    acc3_ref[...] += jnp.dot(a, w3_ref[...], preferred_element_type=F32)

    @pl.when(k == nk - 1)
    def _():
        h1 = acc1_ref[...]
        hid = h1 * jax.nn.sigmoid(h1) * acc3_ref[...]
        if gated:
            hid = hid * jnp.tile(g_ref[...], (1, hid.shape[1] // LANE))
        o_ref[...] = hid.astype(o_ref.dtype)


def swiglu_up(a, w1, w3, gate_rep=None, e0=0, n_e=None):
    m, k = a.shape
    f = w1.shape[-1]
    n_e = w1.shape[0] if n_e is None else n_e
    n = n_e * f
    tm = _pick(m, 1024, 16)
    tk = _pick(k, 2048, LANE)
    tn = _pick(f, 1024, LANE)
    gated = gate_rep is not None
    nk = k // tk
    per = f // tn
    w_spec = pl.BlockSpec((None, tk, tn), lambda i, j, kk: (e0 + j // per, kk, j % per))
    in_specs = [pl.BlockSpec((tm, tk), lambda i, j, kk: (i, kk)), w_spec, w_spec]
    args = [a, w1, w3]
    if gated:
        in_specs.append(pl.BlockSpec((tm, LANE), lambda i, j, kk: (i, j // per)))
        args.append(gate_rep)
    return pl.pallas_call(
        functools.partial(_swiglu_up_kernel, nk=nk, gated=gated),
        grid=(m // tm, n // tn, nk),
        in_specs=in_specs,
        out_specs=pl.BlockSpec((tm, tn), lambda i, j, kk: (i, j)),
        out_shape=jax.ShapeDtypeStruct((m, n), BF16),
        scratch_shapes=[pltpu.VMEM((tm, tn), F32), pltpu.VMEM((tm, tn), F32)],
        compiler_params=_params(("parallel", "parallel", "arbitrary")),
        name="swiglu_up",
    )(*args)


def _resid_kernel(a_ref, w_ref, x_ref, g_ref, o_ref, acc_ref, *, nk):
    k = pl.program_id(2)

    @pl.when(k == 0)
    def _():
        acc_ref[...] = jnp.zeros_like(acc_ref)

    acc_ref[...] += jnp.dot(a_ref[...], w_ref[...], preferred_element_type=F32)

    @pl.when(k == nk - 1)
    def _():
        o_ref[...] = x_ref[...] + g_ref[0] * acc_ref[...]


def resid_matmul(a, w, x, g, rows_per_batch, layer):
    m, k = a.shape
    n = w.shape[-1]
    tm = _pick(rows_per_batch, 1024, 16)
    tn = _pick(n, 1024, LANE)
    tk = _pick(k, 2048, LANE)
    nk = k // tk
    per = rows_per_batch // tm
    return pl.pallas_call(
        functools.partial(_resid_kernel, nk=nk),
        grid=(m // tm, n // tn, nk),
        in_specs=[pl.BlockSpec((tm, tk), lambda i, j, kk: (i, kk)),
                  pl.BlockSpec((None, tk, tn), lambda i, j, kk: (layer, kk, j)),
                  pl.BlockSpec((tm, tn), lambda i, j, kk: (i, j)),
                  pl.BlockSpec((1, 1, tn), lambda i, j, kk: (i // per, 0, j))],
        out_specs=pl.BlockSpec((tm, tn), lambda i, j, kk: (i, j)),
        out_shape=jax.ShapeDtypeStruct((m, n), F32),
        scratch_shapes=[pltpu.VMEM((tm, tn), F32)],
        compiler_params=_params(("parallel", "parallel", "arbitrary")),
        name="resid_matmul",
    )(a, w, x, g)


def _grouped_up_kernel(te_ref, a_ref, w1_ref, w3_ref, o_ref, acc1_ref, acc3_ref, *, nk):
    del te_ref
    k = pl.program_id(2)

    @pl.when(k == 0)
    def _():
        acc1_ref[...] = jnp.zeros_like(acc1_ref)
        acc3_ref[...] = jnp.zeros_like(acc3_ref)

    a = a_ref[...]
    acc1_ref[...] += jnp.dot(a, w1_ref[...], preferred_element_type=F32)
    acc3_ref[...] += jnp.dot(a, w3_ref[...], preferred_element_type=F32)

    @pl.when(k == nk - 1)
    def _():
        h1 = acc1_ref[...]
        o_ref[...] = (h1 * jax.nn.sigmoid(h1) * acc3_ref[...]).astype(o_ref.dtype)


def _grouped_down_kernel(te_ref, a_ref, w_ref, g_ref, o_ref):
    del te_ref
    y = jnp.dot(a_ref[...], w_ref[...], preferred_element_type=F32)
    o_ref[...] = (y * jnp.tile(g_ref[...], (1, y.shape[1] // LANE))).astype(o_ref.dtype)


def grouped_swiglu(x, tile_expert, row_gate, w1, w3, w2, tm):
    r, d = x.shape
    f = w1.shape[-1]
    tk = _pick(d, 2048, LANE)
    tn = _pick(f, 1024, LANE)
    nk = d // tk
    hid = pl.pallas_call(
        functools.partial(_grouped_up_kernel, nk=nk),
        grid_spec=pltpu.PrefetchScalarGridSpec(
            num_scalar_prefetch=1,
            grid=(r // tm, f // tn, nk),
            in_specs=[pl.BlockSpec((tm, tk), lambda i, j, kk, te: (i, kk)),
                      pl.BlockSpec((None, tk, tn), lambda i, j, kk, te: (te[i], kk, j)),
                      pl.BlockSpec((None, tk, tn), lambda i, j, kk, te: (te[i], kk, j))],
            out_specs=pl.BlockSpec((tm, tn), lambda i, j, kk, te: (i, j)),
            scratch_shapes=[pltpu.VMEM((tm, tn), F32), pltpu.VMEM((tm, tn), F32)]),
        out_shape=jax.ShapeDtypeStruct((r, f), BF16),
        compiler_params=_params(("parallel", "parallel", "arbitrary")),
        name="grouped_up",
    )(tile_expert, x, w1, w3)
    tn2 = _pick(d, 1024, LANE)
    return pl.pallas_call(
        _grouped_down_kernel,
        grid_spec=pltpu.PrefetchScalarGridSpec(
            num_scalar_prefetch=1,
            grid=(r // tm, d // tn2),
            in_specs=[pl.BlockSpec((tm, f), lambda i, j, te: (i, 0)),
                      pl.BlockSpec((None, f, tn2), lambda i, j, te: (te[i], 0, j)),
                      pl.BlockSpec((tm, LANE), lambda i, j, te: (i, 0))],
            out_specs=pl.BlockSpec((tm, tn2), lambda i, j, te: (i, j))),
        out_shape=jax.ShapeDtypeStruct((r, d), BF16),
        compiler_params=_params(("parallel", "parallel")),
        name="grouped_down",
    )(tile_expert, hid, w2, row_gate)


def moe_dispatch(top_idx, weights, n_exp, tm):
    m, k = top_idx.shape
    n = m * k
    e_flat = top_idx.reshape(n)
    order = jnp.argsort(e_flat, stable=True)
    e_sorted = e_flat[order]
    counts = jnp.sum(jax.nn.one_hot(e_flat, n_exp, dtype=jnp.int32), axis=0)
    padded = (counts + tm - 1) // tm * tm
    pad_end = jnp.cumsum(padded)
    pad_start = pad_end - padded
    start = jnp.cumsum(counts) - counts
    dest_sorted = pad_start[e_sorted] + jnp.arange(n, dtype=jnp.int32) - start[e_sorted]
    r = n + n_exp * tm

    def expert_of(rows):
        return jnp.minimum(jnp.sum(rows[:, None] >= pad_end[None, :], axis=1), n_exp - 1).astype(jnp.int32)

    rows = jnp.arange(r, dtype=jnp.int32)
    e_row = expert_of(rows)
    within = rows - pad_start[e_row]
    valid = within < counts[e_row]
    src_sorted = order[jnp.clip(start[e_row] + within, 0, n - 1)]
    src_token = jnp.where(valid, src_sorted // k, 0)
    row_gate = jnp.where(valid, weights.reshape(n)[src_sorted], 0.0)
    tile_expert = expert_of(jnp.arange(r // tm, dtype=jnp.int32) * tm)
    dest = dest_sorted[jnp.argsort(order)].reshape(m, k)
    return src_token, row_gate, tile_expert, dest


def _merge_kernel(yr_ref, yf_ref, yc_ref, lg_ref, pr_ref, pf_ref, pc_ref,
                  gwr_ref, gwf_ref, gwc_ref, gbr_ref, gbf_ref, gbc_ref, o_ref):
    lg = lg_ref[...]

    def branch(y_ref, p_ref, gw_ref, gb_ref):
        gate = jax.nn.sigmoid(jnp.dot(lg, gw_ref[...], preferred_element_type=F32) + gb_ref[...])
        return gate * jnp.dot(y_ref[...], p_ref[...], preferred_element_type=F32)

    out = (branch(yr_ref, pr_ref, gwr_ref, gbr_ref) + branch(yf_ref, pf_ref, gwf_ref, gbf_ref)
           + branch(yc_ref, pc_ref, gwc_ref, gbc_ref))
    o_ref[...] = out.astype(o_ref.dtype)


def gated_merge(yr, yf, yc, lg, pr, pf, pc, gw2, gb, layer):
    m = yr.shape[0]
    d = pr.shape[-1]
    tm = _pick(m, 1024, 16)
    tn = _pick(d, 512, LANE)
    nj = d // tn

    def rows(arr):
        return pl.BlockSpec((tm, arr.shape[1]), lambda i, j: (i, 0))

    def cols(arr, off):
        return pl.BlockSpec((None, arr.shape[1], tn), lambda i, j: (layer, 0, off * nj + j))

    return pl.pallas_call(
        _merge_kernel,
        grid=(m // tm, nj),
        in_specs=[rows(yr), rows(yf), rows(yc), rows(lg), cols(pr, 0), cols(pf, 0), cols(pc, 0),
                  cols(gw2, 0), cols(gw2, 1), cols(gw2, 2), cols(gb, 0), cols(gb, 1), cols(gb, 2)],
        out_specs=pl.BlockSpec((tm, tn), lambda i, j: (i, j)),
        out_shape=jax.ShapeDtypeStruct((m, d), BF16),
        compiler_params=_params(("parallel", "parallel")),
        name="gated_merge",
    )(yr, yf, yc, lg, pr, pf, pc, gw2, gw2, gw2, gb, gb, gb)


def _bf(x):
    return x.astype(BF16)


def _dot(a, b):
    return jnp.dot(_bf(a), _bf(b), preferred_element_type=F32)


def _dot_nt(a, b):
    return lax.dot_general(_bf(a), _bf(b), (((1,), (1,)), ((), ())), preferred_element_type=F32)


def _dot_tn(a, b):
    return lax.dot_general(_bf(a), _bf(b), (((0,), (0,)), ((), ())), preferred_element_type=F32)


def _each(f, *lists):
    return [f(*xs) for xs in zip(*lists)]


def _block_diag(x, width, pack):
    if pack == 1:
        return x
    lane_g = lax.broadcasted_iota(jnp.int32, x.shape, 1) // width
    zero = jnp.zeros_like(x)
    return jnp.concatenate([jnp.where(lane_g == g, x, zero) for g in range(pack)], axis=0)


def _unit_tri_inverse(a_tri, row, col, size, pack):
    def same_block(shift):
        return (row >> shift) == (col >> shift)

    def idot(a, b):
        return _dot(a, _block_diag(_bf(b), size, pack))

    eye = (row == col).astype(F32)
    a8 = _each(lambda a: jnp.where(same_block(3), a, 0.0), a_tri)
    a8_2 = _each(idot, a8, a8)
    a8_4 = _each(idot, a8_2, a8_2)
    x = _each(lambda a: eye - a, a8)
    x = _each(lambda xx, d: xx + d, x, _each(idot, x, a8_2))
    x = _each(lambda xx, d: xx + d, x, _each(idot, x, a8_4))
    shift = 3
    while (1 << shift) < size:
        off = same_block(shift + 1) & jnp.logical_not(same_block(shift))
        e = _each(lambda a: jnp.where(off, a, 0.0), a_tri)
        ex = _each(idot, e, x)
        x = _each(lambda xx, d: xx - d, x, _each(idot, x, ex))
        shift += 1
    return x


def _wkv_kernel(*refs, chunk, hd, heads, pack, dirs, nc):
    nd = len(dirs)
    ins = [refs[6 * d:6 * d + 6] for d in range(nd)]
    s0_refs = refs[6 * nd:7 * nd]
    y_refs = refs[7 * nd:8 * nd]
    sf_refs = refs[8 * nd:9 * nd]
    st_ref, al_ref, be_ref, rt_ref, kt_ref, vb_ref, pl_ref = refs[9 * nd:]
    c = pl.program_id(2)

    @pl.when(c == 0)
    def _():
        for d in range(nd):
            st_ref[d] = s0_refs[d][0]

    row = lax.broadcasted_iota(jnp.int32, (chunk, pack * chunk), 0)
    col = lax.broadcasted_iota(jnp.int32, (chunk, pack * chunk), 1) % chunk
    width = pack * hd
    groups = heads // pack
    sls = [slice(g * width, (g + 1) * width) for g in range(groups)]
    strict, incl = [], []
    for d, reverse in enumerate(dirs):
        st_d, in_d = (col > row, col >= row) if reverse else (col < row, col <= row)
        strict += [st_d] * groups
        incl += [in_d] * groups
        tri = jnp.where(in_d[:, :chunk], 1.0, 0.0).astype(BF16)
        last = 0 if reverse else chunk - 1
        r_ref, lw_ref, k_ref, v_ref, kk_ref, a_ref = ins[d]

        lw = lw_ref[0]
        lw_hi = _bf(lw)
        rem = lw - lw_hi.astype(F32)
        lw_mid = _bf(rem)
        lw_lo = _bf(rem - lw_mid.astype(F32))
        cum = (jnp.dot(tri, lw_hi, preferred_element_type=F32)
               + jnp.dot(tri, lw_mid, preferred_element_type=F32)
               + jnp.dot(tri, lw_lo, preferred_element_type=F32))
        p_in = jnp.exp(cum)
        p_inv = jnp.exp(-cum)
        kk = kk_ref[0].astype(F32)
        rt_ref[d] = _bf(r_ref[0].astype(F32) * p_in)
        kt_ref[d] = _bf(k_ref[0].astype(F32) * p_inv)
        be_ref[d] = _bf(a_ref[0].astype(F32) * kk * p_inv)
        al_ref[d] = _bf(kk * jnp.exp(cum - lw))
        vb_ref[d] = _bf(v_ref[0])
        pl_ref[d] = jnp.broadcast_to(p_in[last:last + 1, :], pl_ref.shape[1:])

    ent = [(d, g) for d in range(nd) for g in range(groups)]
    alpha = [al_ref[d, :, sls[g]] for d, g in ent]
    beta = [be_ref[d, :, sls[g]] for d, g in ent]
    rt = [rt_ref[d, :, sls[g]] for d, g in ent]
    kt = [kt_ref[d, :, sls[g]] for d, g in ent]
    v = [vb_ref[d, :, sls[g]] for d, g in ent]
    st0 = [st_ref[d, g] for d, g in ent]
    st0_b = _each(_bf, st0)

    def bd_keys(x):
        return _block_diag(x, hd, pack)

    def bdot(a, x):
        return _dot(a, _block_diag(_bf(x), hd, pack))

    ar = _each(lambda x, y: jnp.concatenate([x, y], axis=0), alpha, rt)
    x_b = _each(_dot_nt, ar, _each(bd_keys, beta))
    x_k = _each(_dot_nt, ar, _each(bd_keys, kt))
    a_ab = _each(lambda x, m: jnp.where(m, x[:chunk], 0.0), x_b, strict)
    a_rb = _each(lambda x, m: jnp.where(m, x[chunk:], 0.0), x_b, incl)
    a_ak = _each(lambda x, m: jnp.where(m, x[:chunk], 0.0), x_k, strict)
    a_rk = _each(lambda x, m: jnp.where(m, x[chunk:], 0.0), x_k, incl)
    t_inv = _unit_tri_inverse(a_ab, row, col, chunk, pack)

    w_t = _each(bdot, t_inv, alpha)
    u0 = _each(bdot, t_inv, _each(bdot, a_ak, v))
    y0 = _each(bdot, a_rk, v)
    ktv = _each(_dot_tn, kt, v)
    u = _each(lambda x, y: x + y, _each(_dot, w_t, st0_b), u0)
    y1 = _each(_dot, rt, st0_b)
    y2 = _each(bdot, a_rb, u)
    btu = _each(_dot_tn, beta, u)
    p_col = [jnp.transpose(pl_ref[d, :, sls[g]])[:, :1] for d, g in ent]
    own = (lax.broadcasted_iota(jnp.int32, (width, width), 0) // hd
           == lax.broadcasted_iota(jnp.int32, (width, width), 1) // hd)
    for n, (d, g) in enumerate(ent):
        y_refs[d][0, :, sls[g]] = y0[n] + y1[n] - y2[n]
        st_ref[d, g] = (st0[n] + jnp.where(own, ktv[n] - btu[n], 0.0)) * p_col[n]

    @pl.when(c == nc - 1)
    def _():
        for d in range(nd):
            sf_refs[d][0] = st_ref[d]


def wkv_pack(nh, hd):
    pack = LANE // hd if hd < LANE and LANE % hd == 0 and hd == WKV_CHUNK else 1
    return pack if nh % pack == 0 else 1


def wkv_state_shape(b, nh, hd):
    pack = wkv_pack(nh, hd)
    return (b, nh // pack, pack * hd, pack * hd)


def wkv(r, v, kk, per_dir, s0, hd, dirs=(False, True)):
    b, t, width = r.shape
    nh = width // hd
    pack = wkv_pack(nh, hd)
    heads = max(hh for hh in (1, 2, 4, 8, 16, 32) if nh % hh == 0 and hh <= WKV_HEADS and hh % pack == 0)
    chunk = WKV_CHUNK
    nc = t // chunk
    blk = heads * hd
    groups = heads // pack
    gw = pack * hd
    nd = len(dirs)

    def tok_spec(reverse):
        return pl.BlockSpec((1, chunk, blk), lambda bi, hi, ci: (bi, nc - 1 - ci if reverse else ci, hi))

    st_spec = pl.BlockSpec((1, groups, gw, gw), lambda bi, hi, ci: (bi, hi, 0, 0))
    in_specs, args = [], []
    for d, reverse in enumerate(dirs):
        lw, k, a = per_dir[d]
        in_specs += [tok_spec(reverse)] * 6
        args += [r, lw, k, v, kk, a]
    in_specs += [st_spec] * nd
    args += list(s0)
    outs = pl.pallas_call(
        functools.partial(_wkv_kernel, chunk=chunk, hd=hd, heads=heads, pack=pack, dirs=tuple(dirs), nc=nc),
        grid=(b, nh // heads, nc),
        in_specs=in_specs,
        out_specs=[tok_spec(reverse) for reverse in dirs] + [st_spec] * nd,
        out_shape=[jax.ShapeDtypeStruct((b, t, width), F32)] * nd
        + [jax.ShapeDtypeStruct(wkv_state_shape(b, nh, hd), F32)] * nd,
        scratch_shapes=[pltpu.VMEM((nd, groups, gw, gw), F32)]
        + [pltpu.VMEM((nd, chunk, blk), BF16)] * 5 + [pltpu.VMEM((nd, 8, blk), F32)],
        compiler_params=pltpu.CompilerParams(
            dimension_semantics=("parallel", "parallel", "arbitrary"),
            vmem_limit_bytes=VMEM_LIMIT),
        name="wkv",
    )(*args)
    return outs[:nd], outs[nd:]


def _from_prev(x):
    return pltpu.roll(x, 1, axis=0)


def _from_next(x):
    return pltpu.roll(x, x.shape[0] - 1, axis=0)


def _token_shift(x, up, dn, has_up, has_dn, mode):
    rows, ch = x.shape
    if mode == "seq":
        half = ch // 2
        t = lax.broadcasted_iota(jnp.int32, (rows, half), 0)
        s0 = jnp.where(t == 0, 0.0, _from_prev(x[:, :half]))
        s1 = jnp.where(t == rows - 1, 0.0, _from_next(x[:, half:]))
        return [s0, s1]
    q = ch // 4
    t = lax.broadcasted_iota(jnp.int32, (rows, q), 0) % GRID_W
    s0 = jnp.where(t == 0, 0.0, _from_prev(x[:, :q]))
    s1 = jnp.where(t == GRID_W - 1, 0.0, _from_next(x[:, q:2 * q]))
    up = jnp.where(has_up, up, 0.0)
    dn = jnp.where(has_dn, dn, 0.0)
    if rows == GRID_W:
        s2, s3 = up, dn
    else:
        s2 = jnp.concatenate([up, x[:rows - GRID_W, 2 * q:3 * q]], axis=0)
        s3 = jnp.concatenate([x[GRID_W:, 3 * q:], dn], axis=0)
    return [s0, s1, s2, s3]


def _prep_kernel(*refs, mode, nblk):
    if mode == "grid":
        x_ref, up_ref, dn_ref, w_ref, sh_ref, sc_ref, o_ref = refs
    else:
        x_ref, w_ref, sh_ref, sc_ref, o_ref = refs
    i = pl.program_id(1)
    w, sh, sc = w_ref[...], sh_ref[0], sc_ref[0]

    def modulated(x):
        y = x * lax.rsqrt(jnp.mean(x * x, axis=-1, keepdims=True) + NORM_EPS)
        return (y * w) * (1 + sc) + sh

    h = modulated(x_ref[0])
    d = h.shape[1]
    if mode == "grid":
        q = d // 4
        up = modulated(up_ref[0])[:, 2 * q:3 * q]
        dn = modulated(dn_ref[0])[:, 3 * q:]
        parts = _token_shift(h, up, dn, i > 0, i < nblk - 1, mode)
    else:
        parts = _token_shift(h, None, None, None, None, mode)
    o_ref[:, :d] = h.astype(o_ref.dtype)
    width = d // len(parts)
    for n, s in enumerate(parts):
        lo = n * width
        o_ref[:, d + lo:d + lo + width] = (s - h[:, lo:lo + width]).astype(o_ref.dtype)


def prep(x, w, shift, scale, mode):
    b, t, d = x.shape
    rows = _pick(t, PREP_ROWS, GRID_W) if mode == "grid" else t
    nblk = t // rows
    per = rows // GRID_W
    x_spec = pl.BlockSpec((1, rows, d), lambda bi, i: (bi, i, 0))
    vec_spec = pl.BlockSpec((1, d), lambda bi, i: (0, 0))
    mod_spec = pl.BlockSpec((1, 1, d), lambda bi, i: (bi, 0, 0))
    in_specs, args = [x_spec], [x]
    if mode == "grid":
        last = t // GRID_W - 1
        in_specs += [pl.BlockSpec((1, GRID_W, d), lambda bi, i: (bi, jnp.maximum(i * per - 1, 0), 0)),
                     pl.BlockSpec((1, GRID_W, d), lambda bi, i: (bi, jnp.minimum((i + 1) * per, last), 0))]
        args += [x, x]
    in_specs += [vec_spec, mod_spec, mod_spec]
    args += [w[None, :], shift, scale]
    return pl.pallas_call(
        functools.partial(_prep_kernel, mode=mode, nblk=nblk),
        grid=(b, nblk),
        in_specs=in_specs,
        out_specs=pl.BlockSpec((rows, 2 * d), lambda bi, i: (bi * nblk + i, 0)),
        out_shape=jax.ShapeDtypeStruct((b * t, 2 * d), BF16),
        compiler_params=_params(("parallel", "parallel")),
        name="prep_" + mode,
    )(*args)


def _norm_mod_kernel(x_ref, w_ref, sh_ref, sc_ref, *rest, routed):
    x = x_ref[...]
    y = x * lax.rsqrt(jnp.mean(x * x, axis=-1, keepdims=True) + NORM_EPS)
    h = (y * w_ref[...]) * (1 + sc_ref[0]) + sh_ref[0]
    if routed:
        rw_ref, rb_ref, o_ref, lg_ref = rest
        lg_ref[...] = jnp.dot(h, rw_ref[...], precision=HI, preferred_element_type=F32) + rb_ref[...]
    else:
        (o_ref,) = rest
    o_ref[...] = h.astype(o_ref.dtype)


def norm_mod(x, w, shift, scale, rows_per_batch, router_w=None, router_b=None):
    m, d = x.shape
    rows = _pick(rows_per_batch, PREP_ROWS, 8)
    nblk = rows_per_batch // rows
    routed = router_w is not None
    x_spec = pl.BlockSpec((rows, d), lambda bi, i: (bi * nblk + i, 0))
    mod_spec = pl.BlockSpec((1, 1, d), lambda bi, i: (bi, 0, 0))
    in_specs = [x_spec, pl.BlockSpec((1, d), lambda bi, i: (0, 0)), mod_spec, mod_spec]
    args = [x, w[None, :], shift, scale]
    out_specs, out_shape = [x_spec], [jax.ShapeDtypeStruct((m, d), BF16)]
    if routed:
        n_e = router_w.shape[1]
        pad = -n_e % LANE
        in_specs += [pl.BlockSpec((d, n_e + pad), lambda bi, i: (0, 0)),
                     pl.BlockSpec((1, n_e + pad), lambda bi, i: (0, 0))]
        args += [jnp.pad(router_w, ((0, 0), (0, pad))), jnp.pad(router_b, (0, pad))[None, :]]
        out_specs.append(pl.BlockSpec((rows, n_e + pad), lambda bi, i: (bi * nblk + i, 0)))
        out_shape.append(jax.ShapeDtypeStruct((m, n_e + pad), F32))
    out = pl.pallas_call(
        functools.partial(_norm_mod_kernel, routed=routed),
        grid=(m // rows_per_batch, nblk),
        in_specs=in_specs,
        out_specs=out_specs,
        out_shape=out_shape,
        compiler_params=_params(("parallel", "parallel")),
        name="norm_mod",
    )(*args)
    return (out[0], out[1][:, :router_w.shape[1]]) if routed else (out[0], None)


def _head_sums(x, hd):
    rows, width = x.shape
    nt = width // LANE
    stacked = jnp.concatenate([x[:, i * LANE:(i + 1) * LANE] for i in range(nt)], axis=0)
    li = lax.broadcasted_iota(jnp.int32, (LANE, LANE), 0) // hd
    lj = lax.broadcasted_iota(jnp.int32, (LANE, LANE), 1) // hd
    ones = jnp.where(li == lj, 1.0, 0.0).astype(BF16)
    hi = _bf(stacked)
    rem = stacked - hi.astype(F32)
    mid = _bf(rem)
    lo = _bf(rem - mid.astype(F32))
    s = (jnp.dot(hi, ones, preferred_element_type=F32) + jnp.dot(mid, ones, preferred_element_type=F32)
         + jnp.dot(lo, ones, preferred_element_type=F32))
    return jnp.concatenate([s[i * rows:(i + 1) * rows] for i in range(nt)], axis=1)


def _wkv_prep_kernel(*refs, mode, nblk, hd, offs, has_vres):
    refs = list(refs)
    main = [refs.pop(0) for _ in range(3)]
    halo = [(refs.pop(0), refs.pop(0)) for _ in range(3)] if mode == "grid" else [(None, None)] * 3
    low_ref = refs.pop(0)
    vf_ref = refs.pop(0) if has_vres else None
    mu_ref, w0_ref, a0_ref, kk_ref, ka_ref, w2_ref, a2_ref = [refs.pop(0) for _ in range(7)]
    if has_vres:
        v0_ref, v2_ref = refs.pop(0), refs.pop(0)
    r_o, v_o, kk_o, kdf_o, kdr_o, af_o, ar_o, lwf_o, lwr_o = refs
    c = pl.program_id(1)
    low = low_ref[0]

    def lerp(n):
        x = main[n][0].astype(F32)
        up, dn = ((halo[n][0][0].astype(F32), halo[n][1][0].astype(F32)) if mode == "grid"
                  else (None, None))
        sh = jnp.concatenate(_token_shift(x, up, dn, c > 0, c < nblk - 1, mode), axis=1)
        return x + (sh - x) * mu_ref[n:n + 1, :]

    def low_dot(lo, hi_, w, act=None):
        z = low[:, lo:hi_]
        if act is not None:
            z = act(z)
        return jnp.dot(_bf(z), w, preferred_element_type=F32)

    r = lerp(0)
    k = lerp(1)
    v = lerp(2)
    if has_vres:
        gate = jax.nn.sigmoid(v0_ref[...] + low_dot(offs[6], offs[7], v2_ref[...]))
        v = v + (vf_ref[0].astype(F32) - v) * gate
    kq = k * kk_ref[...]
    kk = kq * lax.rsqrt(jnp.maximum(_head_sums(kq * kq, hd), 1e-24))
    r_o[0] = r.astype(r_o.dtype)
    v_o[0] = v.astype(v_o.dtype)
    kk_o[0] = kk.astype(kk_o.dtype)
    for di, (kd_o, a_o, lw_o) in enumerate(((kdf_o, af_o, lwf_o), (kdr_o, ar_o, lwr_o))):
        wl = w0_ref[di:di + 1, :] + low_dot(offs[di], offs[di + 1], w2_ref[di], jnp.tanh)
        softplus = jnp.maximum(-wl, 0.0) + jnp.log1p(jnp.exp(-jnp.abs(wl)))
        lw_o[0] = -jnp.exp(-softplus - DECAY_OFFSET)
        a = jax.nn.sigmoid(a0_ref[di:di + 1, :] + low_dot(offs[2 + di], offs[3 + di], a2_ref[di]))
        a_o[0] = a.astype(a_o.dtype)
        kd_o[0] = (k * (1.0 + (a - 1.0) * ka_ref[...])).astype(kd_o.dtype)


def wkv_prep(p, low, v_first, mu, w0, a0, k_k, k_a, w2, a2, v0, v2, offs, hd, mode):
    b, t, rw3 = p.shape
    rw = rw3 // 3
    q = rw // 4
    rows = GRID_W if mode == "grid" else t
    nblk = t // rows
    has_vres = v_first is not None

    def tok(width, col):
        return pl.BlockSpec((1, rows, width), lambda bi, ci: (bi, ci, col))

    def full(arr):
        nd = arr.ndim
        return pl.BlockSpec(arr.shape, lambda bi, ci: (0,) * nd)

    in_specs = [tok(rw, n) for n in range(3)]
    args = [p, p, p]
    if mode == "grid":
        for n in range(3):
            in_specs += [pl.BlockSpec((1, rows, q), lambda bi, ci, n=n: (bi, jnp.maximum(ci - 1, 0), 4 * n + 2)),
                         pl.BlockSpec((1, rows, q), lambda bi, ci, n=n: (bi, jnp.minimum(ci + 1, nblk - 1), 4 * n + 3))]
            args += [p, p]
    in_specs.append(tok(low.shape[-1], 0))
    args.append(low)
    if has_vres:
        in_specs.append(tok(rw, 0))
        args.append(v_first)
    small = [mu, w0, a0, k_k[None, :], k_a[None, :], w2.astype(BF16), a2.astype(BF16)]
    if has_vres:
        small += [v0[None, :], v2.astype(BF16)]
    in_specs += [full(s) for s in small]
    args += small
    out_spec = tok(rw, 0)
    shapes = [jax.ShapeDtypeStruct((b, t, rw), BF16)] * 7 + [jax.ShapeDtypeStruct((b, t, rw), F32)] * 2
    return pl.pallas_call(
        functools.partial(_wkv_prep_kernel, mode=mode, nblk=nblk, hd=hd, offs=tuple(offs), has_vres=has_vres),
        grid=(b, nblk),
        in_specs=in_specs,
        out_specs=[out_spec] * 9,
        out_shape=shapes,
        compiler_params=_params(("parallel", "parallel")),
        name="wkv_prep_" + mode,
    )(*args)


def _wkv_post_kernel(yf_ref, yr_ref, r_ref, kdf_ref, kdr_ref, v_ref, lg_ref, rk_ref, gw_ref, gb_ref,
                     g2_ref, o_ref, *, hd):
    y = yf_ref[0] + yr_ref[0]
    mean = _head_sums(y, hd) * (1.0 / hd)
    yc = y - mean
    var = _head_sums(yc * yc, hd) * (1.0 / hd)
    y = yc * lax.rsqrt(var + GN_EPS) * gw_ref[...] + gb_ref[...]
    rk = r_ref[0].astype(F32) * (kdf_ref[0].astype(F32) + kdr_ref[0].astype(F32)) * rk_ref[...]
    y = y + _head_sums(rk, hd) * v_ref[0].astype(F32)
    g = jnp.dot(_bf(jax.nn.sigmoid(lg_ref[0])), g2_ref[...], preferred_element_type=F32)
    o_ref[0] = (y * g).astype(o_ref.dtype)


def wkv_post(y_f, y_r, r, kd_f, kd_r, v, low_g, r_k, gn_w, gn_b, g2, hd):
    b, t, rw = y_f.shape
    rows = _pick(t, POST_ROWS, 8)

    def tok(width):
        return pl.BlockSpec((1, rows, width), lambda bi, ci: (bi, ci, 0))

    def full(arr):
        return pl.BlockSpec(arr.shape, lambda bi, ci: (0, 0))

    small = [r_k.reshape(1, rw), gn_w[None, :], gn_b[None, :], g2.astype(BF16)]
    return pl.pallas_call(
        functools.partial(_wkv_post_kernel, hd=hd),
        grid=(b, t // rows),
        in_specs=[tok(rw)] * 6 + [tok(low_g.shape[-1])] + [full(s) for s in small],
        out_specs=tok(rw),
        out_shape=jax.ShapeDtypeStruct((b, t, rw), BF16),
        compiler_params=_params(("parallel", "parallel")),
        name="wkv_post",
    )(y_f, y_r, r, kd_f, kd_r, v, low_g, *small)


def rmsnorm(x, w):
    y = x * lax.rsqrt(jnp.mean(x * x, axis=-1, keepdims=True) + NORM_EPS)
    return y * w


def modulate(x, w, shift, scale):
    return rmsnorm(x, w) * (1 + scale) + shift


def dft_tables(rows, cols, n, scale):
    j = lax.broadcasted_iota(jnp.int32, (rows, cols), 0)
    k = lax.broadcasted_iota(jnp.int32, (rows, cols), 1)
    ang = ((j * k) % n).astype(F32) * (2.0 * math.pi / n)
    return jnp.cos(ang) * scale, jnp.sin(ang) * scale


def _dft_stage_kernel(l_ref, zr_ref, zi_ref, *rest, twiddle):
    z = jnp.concatenate([zr_ref[...], zi_ref[...]], axis=0)
    y = jnp.dot(l_ref[...], z, preferred_element_type=F32)
    if not twiddle:
        rest[0][...] = y.astype(rest[0].dtype)
        return
    twr_ref, twi_ref, o_ref = rest
    half = y.shape[0] // 2
    reps = y.shape[1] // LANE
    tr, ti = jnp.tile(twr_ref[...], (1, reps)), jnp.tile(twi_ref[...], (1, reps))
    yr, yi = y[:half], y[half:]
    o_ref[0] = (yr * tr - yi * ti).astype(o_ref.dtype)
    o_ref[1] = (yr * ti + yi * tr).astype(o_ref.dtype)


class PosDft:
    def __init__(self, n):
        self.n = n
        self.n1 = DFT_N1 if n % DFT_N1 == 0 and n // DFT_N1 >= 8 and (n // DFT_N1) % 8 == 0 else 0
        if not self.n1:
            c, s = dft_tables(n, n, n, n ** -0.5)
            self.direct = jnp.concatenate([c, s], axis=1).astype(BF16)
            return
        n1, n2 = self.n1, n // self.n1
        self.n2 = n2
        c1, s1 = dft_tables(n1, n1, n1, n1 ** -0.5)
        self.l1 = jnp.concatenate([jnp.concatenate([c1, s1], axis=1),
                                   jnp.concatenate([-s1, c1], axis=1)], axis=0).astype(BF16)
        tc, ts = dft_tables(n2, n1, n, 1.0)
        self.twr = jnp.broadcast_to(tc[:, :, None], (n2, n1, LANE))
        self.twi = jnp.broadcast_to(-ts[:, :, None], (n2, n1, LANE))
        c2, s2 = dft_tables(n2, n2, n2, n2 ** -0.5)
        self.l3 = jnp.concatenate([c2, s2], axis=1).astype(BF16)


def fourier_mix(u, pos, chan_table):
    b, n, fw = u.shape
    z = matmul(u.reshape(b * n, fw).astype(BF16), chan_table, BF16, planes=2)
    if not pos.n1:
        z = z.reshape(2, b, n, fw).transpose(1, 0, 2, 3).reshape(b, 2 * n, fw)
        return jnp.stack([matmul(pos.direct, z[i], BF16) for i in range(b)], axis=0)
    n1, n2 = pos.n1, pos.n2
    z = z.reshape(2, b, n1, n2 * fw)
    y = pl.pallas_call(
        functools.partial(_dft_stage_kernel, twiddle=True),
        grid=(b, n2),
        in_specs=[pl.BlockSpec((2 * n1, 2 * n1), lambda bi, j: (0, 0)),
                  pl.BlockSpec((None, None, n1, fw), lambda bi, j: (0, bi, 0, j)),
                  pl.BlockSpec((None, None, n1, fw), lambda bi, j: (1, bi, 0, j)),
                  pl.BlockSpec((None, n1, LANE), lambda bi, j: (j, 0, 0)),
                  pl.BlockSpec((None, n1, LANE), lambda bi, j: (j, 0, 0))],
        out_specs=pl.BlockSpec((None, 2, None, n1, fw), lambda bi, j: (bi, 0, j, 0, 0)),
        out_shape=jax.ShapeDtypeStruct((b, 2, n2, n1, fw), BF16),
        compiler_params=_params(("parallel", "parallel")),
        name="dft_stage1",
    )(pos.l1, z, z, pos.twr, pos.twi)
    y = y.reshape(b, 2, n2, n1 * fw)
    tn = _pick(n1 * fw, 4096, LANE)
    out = pl.pallas_call(
        functools.partial(_dft_stage_kernel, twiddle=False),
        grid=(b, n1 * fw // tn),
        in_specs=[pl.BlockSpec((n2, 2 * n2), lambda bi, j: (0, 0)),
                  pl.BlockSpec((None, None, n2, tn), lambda bi, j: (bi, 0, 0, j)),
                  pl.BlockSpec((None, None, n2, tn), lambda bi, j: (bi, 1, 0, j))],
        out_specs=pl.BlockSpec((None, n2, tn), lambda bi, j: (bi, 0, j)),
        out_shape=jax.ShapeDtypeStruct((b, n2, n1 * fw), BF16),
        compiler_params=_params(("parallel", "parallel")),
        name="dft_stage2",
    )(pos.l3, y, y)
    return out.reshape(b, n, fw)


def depthwise_conv(u, w):
    n = u.shape[1]
    kk = w.shape[0]
    pad = kk // 2
    up = jnp.pad(u, ((0, 0), (pad, pad), (0, 0)))
    out = up[:, 0:n] * w[0]
    for i in range(1, kk):
        out = out + up[:, i:i + n] * w[i]
    return out


def kernel(x, c, ctx, c_ctx, ada_a, ada_b, ada_bias, norm1_w, norm2_w, w_in, mu_rkv, mu_lr, decay_w0, decay_w1, decay_w2, iclr_a0, iclr_a1, iclr_a2, ogate_g1, ogate_g2, k_k, k_a, r_k, gn_w, gn_b, vres_mu, vres_v0, vres_v1, vres_v2, conv_w, gate_w1, gate_w2, gate_b, proj_rwkv, proj_fourier, proj_conv, w_out, ffn_w1, ffn_w3, ffn_w2, router_w, router_b, moe_w1, moe_w3, moe_w2, final_norm_w):
    depth = w_in.shape[0]
    bsz, seq, d = x.shape
    ctx_len = ctx.shape[1]
    rw = mu_rkv.shape[-1]
    nh, hd = r_k.shape[1], r_k.shape[2]
    fw = proj_fourier.shape[1]
    cw = proj_conv.shape[1]
    four_off = 3 * rw
    conv_off = four_off + fw
    n_exp = router_w.shape[-1]
    d_exp = moe_w1.shape[-1]
    n_mod = ada_b.shape[-1] // d
    r_dec, r_icl, r_og, r_vr, r_gate = (decay_w1.shape[-1], iclr_a1.shape[-1], ogate_g1.shape[-1],
                                        vres_v1.shape[-1], gate_w1.shape[-1])

    gc = fw // FOURIER_GROUPS
    cc, cs = dft_tables(gc, gc, gc, gc ** -0.5)
    eye_g = jnp.eye(FOURIER_GROUPS, dtype=F32)
    chan_table = jnp.concatenate([jnp.kron(eye_g, cc), -jnp.kron(eye_g, cs)], axis=1).astype(BF16)
    pos_lat, pos_ctx = PosDft(seq), PosDft(ctx_len)

    w_in_b, w_out_b = w_in.astype(BF16), w_out.astype(BF16)
    proj_r_b, proj_f_b, proj_c_b = proj_rwkv.astype(BF16), proj_fourier.astype(BF16), proj_conv.astype(BF16)
    gate_w2_b, gate_b3 = gate_w2.astype(BF16), gate_b[:, None, :]
    ffn_w1_b, ffn_w3_b, ffn_w2_b = ffn_w1.astype(BF16), ffn_w3.astype(BF16), ffn_w2.astype(BF16)
    moe_w1_b = moe_w1.astype(BF16).reshape((-1,) + moe_w1.shape[2:])
    moe_w3_b = moe_w3.astype(BF16).reshape((-1,) + moe_w3.shape[2:])
    moe_w2_b = moe_w2.astype(BF16).reshape(moe_w2.shape[0], n_exp * d_exp, d)
    moe_w2_e = moe_w2_b.reshape(-1, d_exp, d)

    cond_lat = jax.nn.silu(c)
    cond_ctx = jax.nn.silu(c_ctx)[None, :]
    cond = jnp.concatenate([cond_lat, cond_ctx], axis=0)
    cond = jnp.pad(cond, ((0, 16 - cond.shape[0] % 16), (0, 0)))

    def tokens(xs, mod, l, mode, pos_tab, s0, v_first, last_ctx):
        b, t, _ = xs.shape
        has_vres = l > 0
        hd_cat = prep(xs, norm1_w[l], jnp.broadcast_to(mod[0], (b, 1, d)),
                      jnp.broadcast_to(mod[1], (b, 1, d)), mode)

        w_h = [decay_w1[l, 0], decay_w1[l, 1], iclr_a1[l, 0], iclr_a1[l, 1], ogate_g1[l], gate_w1[l]]
        mus = [mu_lr[l, 0], mu_lr[l, 0], mu_lr[l, 1], mu_lr[l, 1], mu_lr[l, 2], None]
        if has_vres:
            w_h.append(vres_v1[l - 1])
            mus.append(vres_mu[l - 1])
        w_dh = [jnp.zeros_like(w) if m is None else w * m[:, None] for w, m in zip(w_h, mus)]
        w_low = jnp.concatenate([jnp.concatenate(w_h, axis=1), jnp.concatenate(w_dh, axis=1)], axis=0)
        n_low = w_low.shape[1]
        w_low = jnp.pad(w_low, ((0, 0), (0, -n_low % LANE)))
        low = matmul(hd_cat, w_low.astype(BF16)).reshape(b, t, -1)
        offs = [0]
        for w in w_h:
            offs.append(offs[-1] + w.shape[1])
        low_g = low[..., offs[4]:offs[5]]
        low_gate = low[..., offs[5]:offs[6]]

        p = matmul(hd_cat, w_in_b, BF16, n_cols=four_off, layer=l).reshape(b, t, -1)
        p_fc = matmul(hd_cat, w_in_b, BF16, n_cols=w_in_b.shape[-1] - four_off,
                      col_off=four_off, layer=l).reshape(b, t, -1)

        r, v, kk, kd_f, kd_r, a_f, a_r, lw_f, lw_r = wkv_prep(
            p, low, v_first, mu_rkv[l], decay_w0[l], iclr_a0[l], k_k[l], k_a[l], decay_w2[l], iclr_a2[l],
            vres_v0[l - 1] if has_vres else None, vres_v2[l - 1] if has_vres else None, offs, hd, mode)

        if s0 is None:
            s0 = [jnp.zeros(wkv_state_shape(b, nh, hd), F32)] * 2
        ys, states = wkv(r, v, kk, [(lw_f, kd_f, a_f), (lw_r, kd_r, a_r)], s0, hd)
        if last_ctx:
            return None, states, v

        y_rwkv = wkv_post(ys[0], ys[1], r, kd_f, kd_r, v, low_g, r_k[l], gn_w[l], gn_b[l],
                          ogate_g2[l], hd)

        y_four = fourier_mix(p_fc[..., :fw], pos_tab, chan_table)
        gate_b_ = p_fc[..., fw:fw + cw].astype(F32)
        gate_c_ = p_fc[..., fw + cw:fw + 2 * cw].astype(F32)
        uu = p_fc[..., fw + 2 * cw:].astype(F32)
        y_conv = gate_b_ * depthwise_conv(gate_c_ * uu, conv_w[l])

        rows = b * t

        def flat(z):
            return z.reshape(rows, -1).astype(BF16)

        merged = gated_merge(flat(y_rwkv), flat(y_four), flat(y_conv), flat(low_gate),
                             proj_r_b, proj_f_b, proj_c_b, gate_w2_b, gate_b3, l)
        xs = resid_matmul(merged, w_out_b, xs.reshape(rows, d),
                          jnp.broadcast_to(mod[2], (b, 1, d)), t, l)

        j = l // 2
        dense = l % 2 == 0
        h2, logits = norm_mod(xs, norm2_w[l], jnp.broadcast_to(mod[3], (b, 1, d)),
                              jnp.broadcast_to(mod[4], (b, 1, d)), t,
                              None if dense else router_w[j], None if dense else router_b[j])
        if dense:
            hid = swiglu_up(h2, ffn_w1_b, ffn_w3_b, e0=j, n_e=1)
            w2 = ffn_w2_b
        else:
            top_val, top_idx = lax.top_k(logits, TOP_K)
            weights = jax.nn.softmax(top_val, axis=-1)
            if rows >= MOE_SPARSE_MIN_ROWS:
                src, row_gate, tile_e, dest = moe_dispatch(top_idx, weights, n_exp, MOE_TILE)
                y = grouped_swiglu(h2[src], tile_e + j * n_exp,
                                   jnp.broadcast_to(row_gate[:, None], (row_gate.shape[0], LANE)),
                                   moe_w1_b, moe_w3_b, moe_w2_e, MOE_TILE)
                f = sum(y[dest[:, s]].astype(F32) for s in range(TOP_K)).reshape(b, t, d)
                return xs.reshape(b, t, d) + mod[5] * f, states, v
            gate = jnp.sum(jax.nn.one_hot(top_idx, n_exp, dtype=F32) * weights[..., None], axis=-2)
            gate_rep = jnp.repeat(gate, LANE, axis=-1)
            hid = swiglu_up(h2, moe_w1_b, moe_w3_b, gate_rep, e0=j * n_exp, n_e=n_exp)
            w2 = moe_w2_b
        xs = resid_matmul(hid, w2, xs, jnp.broadcast_to(mod[5], (b, 1, d)), t, j)
        return xs.reshape(b, t, d), states, v

    xl, xc = x, ctx
    v_first_l = v_first_c = None
    for l in range(depth):
        last = l == depth - 1
        m = mm(mm(cond, ada_a[l]), ada_b[l]) + ada_bias[l]
        mod_l = [m[:bsz, None, i * d:(i + 1) * d] for i in range(n_mod)]
        mod_c = [m[bsz:bsz + 1, None, i * d:(i + 1) * d] for i in range(n_mod)]

        xc_new, ctx_states, vc = tokens(xc, mod_c, l, "seq", pos_ctx, None, v_first_c, last)
        xl, _, vl = tokens(xl, mod_l, l, "grid", pos_lat, ctx_states, v_first_l, False)
        if l == 0:
            v_first_c, v_first_l = vc, vl
        if not last:
            xc = xc_new
    return rmsnorm(xl, final_norm_w)
```

```python
import functools
import math

import jax
import jax.numpy as jnp
from jax import lax
from jax.experimental import pallas as pl
from jax.experimental.pallas import tpu as pltpu

F32 = jnp.float32
BF16 = jnp.bfloat16

GRID_W = 64
FOURIER_GROUPS = 8
TOP_K = 2
NORM_EPS = 1e-6
GN_EPS = 64e-5
DECAY_OFFSET = 0.5
WKV_CHUNK = 64
WKV_HEADS = 32
MOE_TILE = 512
MOE_SPARSE_MIN_ROWS = 4096
DFT_N1 = 128
PREP_ROWS = 256
POST_ROWS = 256
LANE = 128
VMEM_LIMIT = 56 * 1024 * 1024

HI = lax.Precision.HIGHEST


def _sigmoid(x):
    return 0.5 * jnp.tanh(0.5 * x) + 0.5


def _pick(dim, target, align):
    if dim <= target:
        return dim
    t = (target // align) * align
    while t >= align:
        if dim % t == 0:
            return t
        t -= align
    return dim


def _mm_kernel(a_ref, b_ref, o_ref, acc_ref, *, nk):
    k = pl.program_id(2)

    @pl.when(k == 0)
    def _():
        acc_ref[...] = jnp.zeros_like(acc_ref)

    acc_ref[...] += jnp.dot(a_ref[...], b_ref[...], preferred_element_type=F32)

    @pl.when(k == nk - 1)
    def _():
        o_ref[...] = acc_ref[...].astype(o_ref.dtype)


def matmul(a, b, out_dtype=F32, tm=1024, tn=1024, tk=2048, n_cols=None, col_off=0, planes=1, layer=None):
    m = a.shape[0]
    k = b.shape[-2]
    n = b.shape[-1] if n_cols is None else n_cols
    tm = _pick(m, tm, 16)
    tn = _pick(math.gcd(n // planes, col_off) if col_off else n // planes, tn, LANE)
    tk = _pick(k, tk, LANE)
    nk = k // tk
    joff = col_off // tn
    if planes == 1:
        out_spec = pl.BlockSpec((tm, tn), lambda i, j, kk: (i, j))
        out_shape = jax.ShapeDtypeStruct((m, n), out_dtype)
    else:
        per = n // planes // tn
        out_spec = pl.BlockSpec((None, tm, tn), lambda i, j, kk: (j // per, i, j % per))
        out_shape = jax.ShapeDtypeStruct((planes, m, n // planes), out_dtype)
    if layer is None:
        b_spec = pl.BlockSpec((tk, tn), lambda i, j, kk: (kk, j + joff))
    else:
        b_spec = pl.BlockSpec((None, tk, tn), lambda i, j, kk: (layer, kk, j + joff))
    return pl.pallas_call(
        functools.partial(_mm_kernel, nk=nk),
        grid=(m // tm, n // tn, nk),
        in_specs=[pl.BlockSpec((tm, tk), lambda i, j, kk: (i, kk)), b_spec],
        out_specs=out_spec,
        out_shape=out_shape,
        scratch_shapes=[pltpu.VMEM((tm, tn), F32)],
        compiler_params=pltpu.CompilerParams(
            dimension_semantics=("parallel", "parallel", "arbitrary"),
            vmem_limit_bytes=VMEM_LIMIT),
        name="matmul",
    )(a, b)


def mm(a, b, out_dtype=F32, **kw):
    lead = a.shape[:-1]
    out = matmul(a.reshape(-1, a.shape[-1]).astype(BF16), b.astype(BF16), out_dtype, **kw)
    return out.reshape(lead + (b.shape[-1],))


def _params(sem):
    return pltpu.CompilerParams(dimension_semantics=sem, vmem_limit_bytes=VMEM_LIMIT)


def _swiglu_up_kernel(a_ref, w1_ref, w3_ref, *rest, nk, gated):
    if gated:
        g_ref, o_ref, acc1_ref, acc3_ref = rest
    else:
        o_ref, acc1_ref, acc3_ref = rest
    k = pl.program_id(2)

    @pl.when(k == 0)
    def _():
        acc1_ref[...] = jnp.zeros_like(acc1_ref)
        acc3_ref[...] = jnp.zeros_like(acc3_ref)

    a = a_ref[...]
    acc1_ref[...] += jnp.dot(a, w1_ref[...], preferred_element_type=F32)
    acc3_ref[...] += jnp.dot(a, w3_ref[...], preferred_element_type=F32)

    @pl.when(k == nk - 1)
    def _():
        h1 = acc1_ref[...]
        hid = h1 * _sigmoid(h1) * acc3_ref[...]
        if gated:
            hid = hid * jnp.tile(g_ref[...], (1, hid.shape[1] // LANE))
        o_ref[...] = hid.astype(o_ref.dtype)


def swiglu_up(a, w1, w3, gate_rep=None, e0=0, n_e=None):
    m, k = a.shape
    f = w1.shape[-1]
    n_e = w1.shape[0] if n_e is None else n_e
    n = n_e * f
    tm = _pick(m, 1024, 16)
    tk = _pick(k, 2048, LANE)
    tn = _pick(f, 1024, LANE)
    gated = gate_rep is not None
    nk = k // tk
    per = f // tn
    w_spec = pl.BlockSpec((None, tk, tn), lambda i, j, kk: (e0 + j // per, kk, j % per))
    in_specs = [pl.BlockSpec((tm, tk), lambda i, j, kk: (i, kk)), w_spec, w_spec]
    args = [a, w1, w3]
    if gated:
        in_specs.append(pl.BlockSpec((tm, LANE), lambda i, j, kk: (i, j // per)))
        args.append(gate_rep)
    return pl.pallas_call(
        functools.partial(_swiglu_up_kernel, nk=nk, gated=gated),
        grid=(m // tm, n // tn, nk),
        in_specs=in_specs,
        out_specs=pl.BlockSpec((tm, tn), lambda i, j, kk: (i, j)),
        out_shape=jax.ShapeDtypeStruct((m, n), BF16),
        scratch_shapes=[pltpu.VMEM((tm, tn), F32), pltpu.VMEM((tm, tn), F32)],
        compiler_params=_params(("parallel", "parallel", "arbitrary")),
        name="swiglu_up",
    )(*args)


def _resid_kernel(a_ref, w_ref, x_ref, g_ref, o_ref, acc_ref, *, nk):
    k = pl.program_id(2)

    @pl.when(k == 0)
    def _():
        acc_ref[...] = jnp.zeros_like(acc_ref)

    acc_ref[...] += jnp.dot(a_ref[...], w_ref[...], preferred_element_type=F32)

    @pl.when(k == nk - 1)
    def _():
        o_ref[...] = x_ref[...] + g_ref[0] * acc_ref[...]


def resid_matmul(a, w, x, g, rows_per_batch, layer):
    m, k = a.shape
    n = w.shape[-1]
    tm = _pick(rows_per_batch, 1024, 16)
    tn = _pick(n, 1024, LANE)
    tk = _pick(k, 2048, LANE)
    nk = k // tk
    per = rows_per_batch // tm
    return pl.pallas_call(
        functools.partial(_resid_kernel, nk=nk),
        grid=(m // tm, n // tn, nk),
        in_specs=[pl.BlockSpec((tm, tk), lambda i, j, kk: (i, kk)),
                  pl.BlockSpec((None, tk, tn), lambda i, j, kk: (layer, kk, j)),
                  pl.BlockSpec((tm, tn), lambda i, j, kk: (i, j)),
                  pl.BlockSpec((1, 1, tn), lambda i, j, kk: (i // per, 0, j))],
        out_specs=pl.BlockSpec((tm, tn), lambda i, j, kk: (i, j)),
        out_shape=jax.ShapeDtypeStruct((m, n), F32),
        scratch_shapes=[pltpu.VMEM((tm, tn), F32)],
        compiler_params=_params(("parallel", "parallel", "arbitrary")),
        name="resid_matmul",
    )(a, w, x, g)


def _grouped_up_kernel(te_ref, a_ref, w1_ref, w3_ref, o_ref, acc1_ref, acc3_ref, *, nk):
    del te_ref
    k = pl.program_id(2)

    @pl.when(k == 0)
    def _():
        acc1_ref[...] = jnp.zeros_like(acc1_ref)
        acc3_ref[...] = jnp.zeros_like(acc3_ref)

    a = a_ref[...]
    acc1_ref[...] += jnp.dot(a, w1_ref[...], preferred_element_type=F32)
    acc3_ref[...] += jnp.dot(a, w3_ref[...], preferred_element_type=F32)

    @pl.when(k == nk - 1)
    def _():
        h1 = acc1_ref[...]
        o_ref[...] = (h1 * _sigmoid(h1) * acc3_ref[...]).astype(o_ref.dtype)


def _grouped_down_kernel(te_ref, a_ref, w_ref, g_ref, o_ref):
    del te_ref
    y = jnp.dot(a_ref[...], w_ref[...], preferred_element_type=F32)
    o_ref[...] = (y * jnp.tile(g_ref[...], (1, y.shape[1] // LANE))).astype(o_ref.dtype)


def grouped_swiglu(x, tile_expert, row_gate, w1, w3, w2, tm):
    r, d = x.shape
    f = w1.shape[-1]
    tk = _pick(d, 2048, LANE)
    tn = _pick(f, 1024, LANE)
    nk = d // tk
    hid = pl.pallas_call(
        functools.partial(_grouped_up_kernel, nk=nk),
        grid_spec=pltpu.PrefetchScalarGridSpec(
            num_scalar_prefetch=1,
            grid=(r // tm, f // tn, nk),
            in_specs=[pl.BlockSpec((tm, tk), lambda i, j, kk, te: (i, kk)),
                      pl.BlockSpec((None, tk, tn), lambda i, j, kk, te: (te[i], kk, j)),
                      pl.BlockSpec((None, tk, tn), lambda i, j, kk, te: (te[i], kk, j))],
            out_specs=pl.BlockSpec((tm, tn), lambda i, j, kk, te: (i, j)),
            scratch_shapes=[pltpu.VMEM((tm, tn), F32), pltpu.VMEM((tm, tn), F32)]),
        out_shape=jax.ShapeDtypeStruct((r, f), BF16),
        compiler_params=_params(("parallel", "parallel", "arbitrary")),
        name="grouped_up",
    )(tile_expert, x, w1, w3)
    tn2 = _pick(d, 1024, LANE)
    return pl.pallas_call(
        _grouped_down_kernel,
        grid_spec=pltpu.PrefetchScalarGridSpec(
            num_scalar_prefetch=1,
            grid=(r // tm, d // tn2),
            in_specs=[pl.BlockSpec((tm, f), lambda i, j, te: (i, 0)),
                      pl.BlockSpec((None, f, tn2), lambda i, j, te: (te[i], 0, j)),
                      pl.BlockSpec((tm, LANE), lambda i, j, te: (i, 0))],
            out_specs=pl.BlockSpec((tm, tn2), lambda i, j, te: (i, j))),
        out_shape=jax.ShapeDtypeStruct((r, d), BF16),
        compiler_params=_params(("parallel", "parallel")),
        name="grouped_down",
    )(tile_expert, hid, w2, row_gate)


def moe_dispatch(top_idx, weights, n_exp, tm):
    m, k = top_idx.shape
    n = m * k
    e_flat = top_idx.reshape(n)
    order = jnp.argsort(e_flat, stable=True)
    e_sorted = e_flat[order]
    counts = jnp.sum(jax.nn.one_hot(e_flat, n_exp, dtype=jnp.int32), axis=0)
    padded = (counts + tm - 1) // tm * tm
    pad_end = jnp.cumsum(padded)
    pad_start = pad_end - padded
    start = jnp.cumsum(counts) - counts
    dest_sorted = pad_start[e_sorted] + jnp.arange(n, dtype=jnp.int32) - start[e_sorted]
    r = n + n_exp * tm

    def expert_of(rows):
        return jnp.minimum(jnp.sum(rows[:, None] >= pad_end[None, :], axis=1), n_exp - 1).astype(jnp.int32)

    rows = jnp.arange(r, dtype=jnp.int32)
    e_row = expert_of(rows)
    within = rows - pad_start[e_row]
    valid = within < counts[e_row]
    src_sorted = order[jnp.clip(start[e_row] + within, 0, n - 1)]
    src_token = jnp.where(valid, src_sorted // k, 0)
    row_gate = jnp.where(valid, weights.reshape(n)[src_sorted], 0.0)
    tile_expert = expert_of(jnp.arange(r // tm, dtype=jnp.int32) * tm)
    dest = dest_sorted[jnp.argsort(order)].reshape(m, k)
    return src_token, row_gate, tile_expert, dest


def _merge_kernel(yr_ref, yf_ref, yc_ref, lg_ref, pr_ref, pf_ref, pc_ref,
                  gwr_ref, gwf_ref, gwc_ref, gbr_ref, gbf_ref, gbc_ref, o_ref):
    lg = lg_ref[...]

    def branch(y_ref, p_ref, gw_ref, gb_ref):
        gate = _sigmoid(jnp.dot(lg, gw_ref[...], preferred_element_type=F32) + gb_ref[...])
        return gate * jnp.dot(y_ref[...], p_ref[...], preferred_element_type=F32)

    out = (branch(yr_ref, pr_ref, gwr_ref, gbr_ref) + branch(yf_ref, pf_ref, gwf_ref, gbf_ref)
           + branch(yc_ref, pc_ref, gwc_ref, gbc_ref))
    o_ref[...] = out.astype(o_ref.dtype)


def gated_merge(yr, yf, yc, lg, pr, pf, pc, gw2, gb, layer):
    m = yr.shape[0]
    d = pr.shape[-1]
    tm = _pick(m, 1024, 16)
    tn = _pick(d, 512, LANE)
    nj = d // tn

    def rows(arr):
        return pl.BlockSpec((tm, arr.shape[1]), lambda i, j: (i, 0))

    def cols(arr, off):
        return pl.BlockSpec((None, arr.shape[1], tn), lambda i, j: (layer, 0, off * nj + j))

    return pl.pallas_call(
        _merge_kernel,
        grid=(m // tm, nj),
        in_specs=[rows(yr), rows(yf), rows(yc), rows(lg), cols(pr, 0), cols(pf, 0), cols(pc, 0),
                  cols(gw2, 0), cols(gw2, 1), cols(gw2, 2), cols(gb, 0), cols(gb, 1), cols(gb, 2)],
        out_specs=pl.BlockSpec((tm, tn), lambda i, j: (i, j)),
        out_shape=jax.ShapeDtypeStruct((m, d), BF16),
        compiler_params=_params(("parallel", "parallel")),
        name="gated_merge",
    )(yr, yf, yc, lg, pr, pf, pc, gw2, gw2, gw2, gb, gb, gb)


def _bf(x):
    return x.astype(BF16)


def _dot(a, b):
    return jnp.dot(_bf(a), _bf(b), preferred_element_type=F32)


def _dot_nt(a, b):
    return lax.dot_general(_bf(a), _bf(b), (((1,), (1,)), ((), ())), preferred_element_type=F32)


def _dot_tn(a, b):
    return lax.dot_general(_bf(a), _bf(b), (((0,), (0,)), ((), ())), preferred_element_type=F32)


def _each(f, *lists):
    return [f(*xs) for xs in zip(*lists)]


def _block_diag(x, width, pack):
    if pack == 1:
        return x
    lane_g = lax.broadcasted_iota(jnp.int32, x.shape, 1) // width
    zero = jnp.zeros_like(x)
    return jnp.concatenate([jnp.where(lane_g == g, x, zero) for g in range(pack)], axis=0)


def _unit_tri_inverse(a_tri, row, col, size, pack):
    def same_block(shift):
        return (row >> shift) == (col >> shift)

    def idot(a, b):
        return _dot(a, _block_diag(_bf(b), size, pack))

    eye = (row == col).astype(F32)
    a8 = _each(lambda a: jnp.where(same_block(3), a, 0.0), a_tri)
    a8_2 = _each(idot, a8, a8)
    a8_4 = _each(idot, a8_2, a8_2)
    x = _each(lambda a: eye - a, a8)
    x = _each(lambda xx, d: xx + d, x, _each(idot, x, a8_2))
    x = _each(lambda xx, d: xx + d, x, _each(idot, x, a8_4))
    shift = 3
    while (1 << shift) < size:
        off = same_block(shift + 1) & jnp.logical_not(same_block(shift))
        e = _each(lambda a: jnp.where(off, a, 0.0), a_tri)
        ex = _each(idot, e, x)
        x = _each(lambda xx, d: xx - d, x, _each(idot, x, ex))
        shift += 1
    return x


def _wkv_kernel(*refs, chunk, hd, heads, pack, dirs, nc):
    nd = len(dirs)
    ins = [refs[6 * d:6 * d + 6] for d in range(nd)]
    s0_refs = refs[6 * nd:7 * nd]
    y_refs = refs[7 * nd:8 * nd]
    sf_refs = refs[8 * nd:9 * nd]
    st_ref, al_ref, be_ref, rt_ref, kt_ref, vb_ref, pl_ref = refs[9 * nd:]
    c = pl.program_id(2)

    @pl.when(c == 0)
    def _():
        for d in range(nd):
            st_ref[d] = s0_refs[d][0]

    row = lax.broadcasted_iota(jnp.int32, (chunk, pack * chunk), 0)
    col = lax.broadcasted_iota(jnp.int32, (chunk, pack * chunk), 1) % chunk
    width = pack * hd
    groups = heads // pack
    sls = [slice(g * width, (g + 1) * width) for g in range(groups)]
    strict, incl = [], []
    for d, reverse in enumerate(dirs):
        st_d, in_d = (col > row, col >= row) if reverse else (col < row, col <= row)
        strict += [st_d] * groups
        incl += [in_d] * groups
        tri = jnp.where(in_d[:, :chunk], 1.0, 0.0).astype(BF16)
        last = 0 if reverse else chunk - 1
        r_ref, lw_ref, k_ref, v_ref, kk_ref, a_ref = ins[d]

        lw = lw_ref[0]
        lw_hi = _bf(lw)
        rem = lw - lw_hi.astype(F32)
        lw_mid = _bf(rem)
        lw_lo = _bf(rem - lw_mid.astype(F32))
        cum = (jnp.dot(tri, lw_hi, preferred_element_type=F32)
               + jnp.dot(tri, lw_mid, preferred_element_type=F32)
               + jnp.dot(tri, lw_lo, preferred_element_type=F32))
        p_in = jnp.exp(cum)
        p_inv = jnp.exp(-cum)
        kk = kk_ref[0].astype(F32)
        rt_ref[d] = _bf(r_ref[0].astype(F32) * p_in)
        kt_ref[d] = _bf(k_ref[0].astype(F32) * p_inv)
        be_ref[d] = _bf(a_ref[0].astype(F32) * kk * p_inv)
        al_ref[d] = _bf(kk * jnp.exp(cum - lw))
        vb_ref[d] = _bf(v_ref[0])
        pl_ref[d] = jnp.broadcast_to(p_in[last:last + 1, :], pl_ref.shape[1:])

    ent = [(d, g) for d in range(nd) for g in range(groups)]
    alpha = [al_ref[d, :, sls[g]] for d, g in ent]
    beta = [be_ref[d, :, sls[g]] for d, g in ent]
    rt = [rt_ref[d, :, sls[g]] for d, g in ent]
    kt = [kt_ref[d, :, sls[g]] for d, g in ent]
    v = [vb_ref[d, :, sls[g]] for d, g in ent]
    st0 = [st_ref[d, g] for d, g in ent]
    st0_b = _each(_bf, st0)

    def bd_keys(x):
        return _block_diag(x, hd, pack)

    def bdot(a, x):
        return _dot(a, _block_diag(_bf(x), hd, pack))

    ar = _each(lambda x, y: jnp.concatenate([x, y], axis=0), alpha, rt)
    x_b = _each(_dot_nt, ar, _each(bd_keys, beta))
    x_k = _each(_dot_nt, ar, _each(bd_keys, kt))
    a_ab = _each(lambda x, m: jnp.where(m, x[:chunk], 0.0), x_b, strict)
    a_rb = _each(lambda x, m: jnp.where(m, x[chunk:], 0.0), x_b, incl)
    a_ak = _each(lambda x, m: jnp.where(m, x[:chunk], 0.0), x_k, strict)
    a_rk = _each(lambda x, m: jnp.where(m, x[chunk:], 0.0), x_k, incl)
    t_inv = _unit_tri_inverse(a_ab, row, col, chunk, pack)

    w_t = _each(bdot, t_inv, alpha)
    u0 = _each(bdot, t_inv, _each(bdot, a_ak, v))
    y0 = _each(bdot, a_rk, v)
    ktv = _each(_dot_tn, kt, v)
    u = _each(lambda x, y: x + y, _each(_dot, w_t, st0_b), u0)
    y1 = _each(_dot, rt, st0_b)
    y2 = _each(bdot, a_rb, u)
    btu = _each(_dot_tn, beta, u)
    p_col = [jnp.transpose(pl_ref[d, :, sls[g]])[:, :1] for d, g in ent]
    own = (lax.broadcasted_iota(jnp.int32, (width, width), 0) // hd
           == lax.broadcasted_iota(jnp.int32, (width, width), 1) // hd)
    for n, (d, g) in enumerate(ent):
        y_refs[d][0, :, sls[g]] = y0[n] + y1[n] - y2[n]
        st_ref[d, g] = (st0[n] + jnp.where(own, ktv[n] - btu[n], 0.0)) * p_col[n]

    @pl.when(c == nc - 1)
    def _():
        for d in range(nd):
            sf_refs[d][0] = st_ref[d]


def wkv_pack(nh, hd):
    pack = LANE // hd if hd < LANE and LANE % hd == 0 and hd == WKV_CHUNK else 1
    return pack if nh % pack == 0 else 1


def wkv_state_shape(b, nh, hd):
    pack = wkv_pack(nh, hd)
    return (b, nh // pack, pack * hd, pack * hd)


def wkv(r, v, kk, per_dir, s0, hd, dirs=(False, True)):
    b, t, width = r.shape
    nh = width // hd
    pack = wkv_pack(nh, hd)
    heads = max(hh for hh in (1, 2, 4, 8, 16, 32) if nh % hh == 0 and hh <= WKV_HEADS and hh % pack == 0)
    chunk = WKV_CHUNK
    nc = t // chunk
    blk = heads * hd
    groups = heads // pack
    gw = pack * hd
    nd = len(dirs)

    def tok_spec(reverse):
        return pl.BlockSpec((1, chunk, blk), lambda bi, hi, ci: (bi, nc - 1 - ci if reverse else ci, hi))

    st_spec = pl.BlockSpec((1, groups, gw, gw), lambda bi, hi, ci: (bi, hi, 0, 0))
    in_specs, args = [], []
    for d, reverse in enumerate(dirs):
        lw, k, a = per_dir[d]
        in_specs += [tok_spec(reverse)] * 6
        args += [r, lw, k, v, kk, a]
    in_specs += [st_spec] * nd
    args += list(s0)
    outs = pl.pallas_call(
        functools.partial(_wkv_kernel, chunk=chunk, hd=hd, heads=heads, pack=pack, dirs=tuple(dirs), nc=nc),
        grid=(b, nh // heads, nc),
        in_specs=in_specs,
        out_specs=[tok_spec(reverse) for reverse in dirs] + [st_spec] * nd,
        out_shape=[jax.ShapeDtypeStruct((b, t, width), F32)] * nd
        + [jax.ShapeDtypeStruct(wkv_state_shape(b, nh, hd), F32)] * nd,
        scratch_shapes=[pltpu.VMEM((nd, groups, gw, gw), F32)]
        + [pltpu.VMEM((nd, chunk, blk), BF16)] * 5 + [pltpu.VMEM((nd, 8, blk), F32)],
        compiler_params=pltpu.CompilerParams(
            dimension_semantics=("parallel", "parallel", "arbitrary"),
            vmem_limit_bytes=VMEM_LIMIT),
        name="wkv",
    )(*args)
    return outs[:nd], outs[nd:]


def _from_prev(x):
    return pltpu.roll(x, 1, axis=0)


def _from_next(x):
    return pltpu.roll(x, x.shape[0] - 1, axis=0)


def _token_shift(x, up, dn, has_up, has_dn, mode):
    rows, ch = x.shape
    if mode == "seq":
        half = ch // 2
        t = lax.broadcasted_iota(jnp.int32, (rows, half), 0)
        s0 = jnp.where(t == 0, 0.0, _from_prev(x[:, :half]))
        s1 = jnp.where(t == rows - 1, 0.0, _from_next(x[:, half:]))
        return [s0, s1]
    q = ch // 4
    t = lax.broadcasted_iota(jnp.int32, (rows, q), 0) % GRID_W
    s0 = jnp.where(t == 0, 0.0, _from_prev(x[:, :q]))
    s1 = jnp.where(t == GRID_W - 1, 0.0, _from_next(x[:, q:2 * q]))
    up = jnp.where(has_up, up, 0.0)
    dn = jnp.where(has_dn, dn, 0.0)
    if rows == GRID_W:
        s2, s3 = up, dn
    else:
        s2 = jnp.concatenate([up, x[:rows - GRID_W, 2 * q:3 * q]], axis=0)
        s3 = jnp.concatenate([x[GRID_W:, 3 * q:], dn], axis=0)
    return [s0, s1, s2, s3]


def _prep_kernel(*refs, mode, nblk):
    if mode == "grid":
        x_ref, up_ref, dn_ref, w_ref, sh_ref, sc_ref, o_ref = refs
    else:
        x_ref, w_ref, sh_ref, sc_ref, o_ref = refs
    i = pl.program_id(1)
    w, sh, sc = w_ref[...], sh_ref[0], sc_ref[0]

    def modulated(x):
        y = x * lax.rsqrt(jnp.mean(x * x, axis=-1, keepdims=True) + NORM_EPS)
        return (y * w) * (1 + sc) + sh

    h = modulated(x_ref[0])
    d = h.shape[1]
    if mode == "grid":
        q = d // 4
        up = modulated(up_ref[0])[:, 2 * q:3 * q]
        dn = modulated(dn_ref[0])[:, 3 * q:]
        parts = _token_shift(h, up, dn, i > 0, i < nblk - 1, mode)
    else:
        parts = _token_shift(h, None, None, None, None, mode)
    o_ref[:, :d] = h.astype(o_ref.dtype)
    width = d // len(parts)
    for n, s in enumerate(parts):
        lo = n * width
        o_ref[:, d + lo:d + lo + width] = (s - h[:, lo:lo + width]).astype(o_ref.dtype)


def prep(x, w, shift, scale, mode):
    b, t, d = x.shape
    rows = _pick(t, PREP_ROWS, GRID_W) if mode == "grid" else t
    nblk = t // rows
    per = rows // GRID_W
    x_spec = pl.BlockSpec((1, rows, d), lambda bi, i: (bi, i, 0))
    vec_spec = pl.BlockSpec((1, d), lambda bi, i: (0, 0))
    mod_spec = pl.BlockSpec((1, 1, d), lambda bi, i: (bi, 0, 0))
    in_specs, args = [x_spec], [x]
    if mode == "grid":
        last = t // GRID_W - 1
        in_specs += [pl.BlockSpec((1, GRID_W, d), lambda bi, i: (bi, jnp.maximum(i * per - 1, 0), 0)),
                     pl.BlockSpec((1, GRID_W, d), lambda bi, i: (bi, jnp.minimum((i + 1) * per, last), 0))]
        args += [x, x]
    in_specs += [vec_spec, mod_spec, mod_spec]
    args += [w[None, :], shift, scale]
    return pl.pallas_call(
        functools.partial(_prep_kernel, mode=mode, nblk=nblk),
        grid=(b, nblk),
        in_specs=in_specs,
        out_specs=pl.BlockSpec((rows, 2 * d), lambda bi, i: (bi * nblk + i, 0)),
        out_shape=jax.ShapeDtypeStruct((b * t, 2 * d), BF16),
        compiler_params=_params(("parallel", "parallel")),
        name="prep_" + mode,
    )(*args)


def _norm_mod_kernel(x_ref, w_ref, sh_ref, sc_ref, *rest, routed):
    x = x_ref[...]
    y = x * lax.rsqrt(jnp.mean(x * x, axis=-1, keepdims=True) + NORM_EPS)
    h = (y * w_ref[...]) * (1 + sc_ref[0]) + sh_ref[0]
    if routed:
        rw_ref, rb_ref, o_ref, lg_ref = rest
        lg_ref[...] = jnp.dot(h, rw_ref[...], precision=HI, preferred_element_type=F32) + rb_ref[...]
    else:
        (o_ref,) = rest
    o_ref[...] = h.astype(o_ref.dtype)


def norm_mod(x, w, shift, scale, rows_per_batch, router_w=None, router_b=None):
    m, d = x.shape
    rows = _pick(rows_per_batch, PREP_ROWS, 8)
    nblk = rows_per_batch // rows
    routed = router_w is not None
    x_spec = pl.BlockSpec((rows, d), lambda bi, i: (bi * nblk + i, 0))
    mod_spec = pl.BlockSpec((1, 1, d), lambda bi, i: (bi, 0, 0))
    in_specs = [x_spec, pl.BlockSpec((1, d), lambda bi, i: (0, 0)), mod_spec, mod_spec]
    args = [x, w[None, :], shift, scale]
    out_specs, out_shape = [x_spec], [jax.ShapeDtypeStruct((m, d), BF16)]
    if routed:
        n_e = router_w.shape[1]
        pad = -n_e % LANE
        in_specs += [pl.BlockSpec((d, n_e + pad), lambda bi, i: (0, 0)),
                     pl.BlockSpec((1, n_e + pad), lambda bi, i: (0, 0))]
        args += [jnp.pad(router_w, ((0, 0), (0, pad))), jnp.pad(router_b, (0, pad))[None, :]]
        out_specs.append(pl.BlockSpec((rows, n_e + pad), lambda bi, i: (bi * nblk + i, 0)))
        out_shape.append(jax.ShapeDtypeStruct((m, n_e + pad), F32))
    out = pl.pallas_call(
        functools.partial(_norm_mod_kernel, routed=routed),
        grid=(m // rows_per_batch, nblk),
        in_specs=in_specs,
        out_specs=out_specs,
        out_shape=out_shape,
        compiler_params=_params(("parallel", "parallel")),
        name="norm_mod",
    )(*args)
    return (out[0], out[1][:, :router_w.shape[1]]) if routed else (out[0], None)


def _head_sums(x, hd):
    rows, width = x.shape
    nt = width // LANE
    stacked = jnp.concatenate([x[:, i * LANE:(i + 1) * LANE] for i in range(nt)], axis=0)
    li = lax.broadcasted_iota(jnp.int32, (LANE, LANE), 0) // hd
    lj = lax.broadcasted_iota(jnp.int32, (LANE, LANE), 1) // hd
    ones = jnp.where(li == lj, 1.0, 0.0).astype(BF16)
    hi = _bf(stacked)
    rem = stacked - hi.astype(F32)
    mid = _bf(rem)
    lo = _bf(rem - mid.astype(F32))
    s = (jnp.dot(hi, ones, preferred_element_type=F32) + jnp.dot(mid, ones, preferred_element_type=F32)
         + jnp.dot(lo, ones, preferred_element_type=F32))
    return jnp.concatenate([s[i * rows:(i + 1) * rows] for i in range(nt)], axis=1)


def _wkv_prep_kernel(*refs, mode, nblk, hd, offs, has_vres):
    refs = list(refs)
    main = [refs.pop(0) for _ in range(3)]
    halo = [(refs.pop(0), refs.pop(0)) for _ in range(3)] if mode == "grid" else [(None, None)] * 3
    low_ref = refs.pop(0)
    vf_ref = refs.pop(0) if has_vres else None
    mu_ref, w0_ref, a0_ref, kk_ref, ka_ref, w2_ref, a2_ref = [refs.pop(0) for _ in range(7)]
    if has_vres:
        v0_ref, v2_ref = refs.pop(0), refs.pop(0)
    r_o, v_o, kk_o, kdf_o, kdr_o, af_o, ar_o, lwf_o, lwr_o = refs
    c = pl.program_id(1)
    low = low_ref[0]

    def lerp(n):
        x = main[n][0].astype(F32)
        up, dn = ((halo[n][0][0].astype(F32), halo[n][1][0].astype(F32)) if mode == "grid"
                  else (None, None))
        sh = jnp.concatenate(_token_shift(x, up, dn, c > 0, c < nblk - 1, mode), axis=1)
        return x + (sh - x) * mu_ref[n:n + 1, :]

    def low_dot(lo, hi_, w, act=None):
        z = low[:, lo:hi_]
        if act is not None:
            z = act(z)
        return jnp.dot(_bf(z), w, preferred_element_type=F32)

    r = lerp(0)
    k = lerp(1)
    v = lerp(2)
    if has_vres:
        gate = _sigmoid(v0_ref[...] + low_dot(offs[6], offs[7], v2_ref[...]))
        v = v + (vf_ref[0].astype(F32) - v) * gate
    kq = k * kk_ref[...]
    kk = kq * lax.rsqrt(jnp.maximum(_head_sums(kq * kq, hd), 1e-24))
    r_o[0] = r.astype(r_o.dtype)
    v_o[0] = v.astype(v_o.dtype)
    kk_o[0] = kk.astype(kk_o.dtype)
    for di, (kd_o, a_o, lw_o) in enumerate(((kdf_o, af_o, lwf_o), (kdr_o, ar_o, lwr_o))):
        wl = w0_ref[di:di + 1, :] + low_dot(offs[di], offs[di + 1], w2_ref[di], jnp.tanh)
        lw_o[0] = _sigmoid(wl) * (-math.exp(-DECAY_OFFSET))
        a = _sigmoid(a0_ref[di:di + 1, :] + low_dot(offs[2 + di], offs[3 + di], a2_ref[di]))
        a_o[0] = a.astype(a_o.dtype)
        kd_o[0] = (k * (1.0 + (a - 1.0) * ka_ref[...])).astype(kd_o.dtype)


def wkv_prep(p, low, v_first, mu, w0, a0, k_k, k_a, w2, a2, v0, v2, offs, hd, mode):
    b, t, rw3 = p.shape
    rw = rw3 // 3
    q = rw // 4
    rows = GRID_W if mode == "grid" else t
    nblk = t // rows
    has_vres = v_first is not None

    def tok(width, col):
        return pl.BlockSpec((1, rows, width), lambda bi, ci: (bi, ci, col))

    def full(arr):
        nd = arr.ndim
        return pl.BlockSpec(arr.shape, lambda bi, ci: (0,) * nd)

    in_specs = [tok(rw, n) for n in range(3)]
    args = [p, p, p]
    if mode == "grid":
        for n in range(3):
            in_specs += [pl.BlockSpec((1, rows, q), lambda bi, ci, n=n: (bi, jnp.maximum(ci - 1, 0), 4 * n + 2)),
                         pl.BlockSpec((1, rows, q), lambda bi, ci, n=n: (bi, jnp.minimum(ci + 1, nblk - 1), 4 * n + 3))]
            args += [p, p]
    in_specs.append(tok(low.shape[-1], 0))
    args.append(low)
    if has_vres:
        in_specs.append(tok(rw, 0))
        args.append(v_first)
    small = [mu, w0, a0, k_k[None, :], k_a[None, :], w2.astype(BF16), a2.astype(BF16)]
    if has_vres:
        small += [v0[None, :], v2.astype(BF16)]
    in_specs += [full(s) for s in small]
    args += small
    out_spec = tok(rw, 0)
    shapes = [jax.ShapeDtypeStruct((b, t, rw), BF16)] * 7 + [jax.ShapeDtypeStruct((b, t, rw), F32)] * 2
    return pl.pallas_call(
        functools.partial(_wkv_prep_kernel, mode=mode, nblk=nblk, hd=hd, offs=tuple(offs), has_vres=has_vres),
        grid=(b, nblk),
        in_specs=in_specs,
        out_specs=[out_spec] * 9,
        out_shape=shapes,
        compiler_params=_params(("parallel", "parallel")),
        name="wkv_prep_" + mode,
    )(*args)


def _wkv_post_kernel(yf_ref, yr_ref, r_ref, kdf_ref, kdr_ref, v_ref, lg_ref, rk_ref, gw_ref, gb_ref,
                     g2_ref, o_ref, *, hd):
    y = yf_ref[0] + yr_ref[0]
    mean = _head_sums(y, hd) * (1.0 / hd)
    yc = y - mean
    var = _head_sums(yc * yc, hd) * (1.0 / hd)
    y = yc * lax.rsqrt(var + GN_EPS) * gw_ref[...] + gb_ref[...]
    rk = r_ref[0].astype(F32) * (kdf_ref[0].astype(F32) + kdr_ref[0].astype(F32)) * rk_ref[...]
    y = y + _head_sums(rk, hd) * v_ref[0].astype(F32)
    g = jnp.dot(_bf(_sigmoid(lg_ref[0])), g2_ref[...], preferred_element_type=F32)
    o_ref[0] = (y * g).astype(o_ref.dtype)


def wkv_post(y_f, y_r, r, kd_f, kd_r, v, low_g, r_k, gn_w, gn_b, g2, hd):
    b, t, rw = y_f.shape
    rows = _pick(t, POST_ROWS, 8)

    def tok(width):
        return pl.BlockSpec((1, rows, width), lambda bi, ci: (bi, ci, 0))

    def full(arr):
        return pl.BlockSpec(arr.shape, lambda bi, ci: (0, 0))

    small = [r_k.reshape(1, rw), gn_w[None, :], gn_b[None, :], g2.astype(BF16)]
    return pl.pallas_call(
        functools.partial(_wkv_post_kernel, hd=hd),
        grid=(b, t // rows),
        in_specs=[tok(rw)] * 6 + [tok(low_g.shape[-1])] + [full(s) for s in small],
        out_specs=tok(rw),
        out_shape=jax.ShapeDtypeStruct((b, t, rw), BF16),
        compiler_params=_params(("parallel", "parallel")),
        name="wkv_post",
    )(y_f, y_r, r, kd_f, kd_r, v, low_g, *small)


def rmsnorm(x, w):
    y = x * lax.rsqrt(jnp.mean(x * x, axis=-1, keepdims=True) + NORM_EPS)
    return y * w


def modulate(x, w, shift, scale):
    return rmsnorm(x, w) * (1 + scale) + shift


def dft_tables(rows, cols, n, scale):
    j = lax.broadcasted_iota(jnp.int32, (rows, cols), 0)
    k = lax.broadcasted_iota(jnp.int32, (rows, cols), 1)
    ang = ((j * k) % n).astype(F32) * (2.0 * math.pi / n)
    return jnp.cos(ang) * scale, jnp.sin(ang) * scale


def _dft_stage_kernel(l_ref, zr_ref, zi_ref, *rest, twiddle):
    z = jnp.concatenate([zr_ref[...], zi_ref[...]], axis=0)
    y = jnp.dot(l_ref[...], z, preferred_element_type=F32)
    if not twiddle:
        rest[0][...] = y.astype(rest[0].dtype)
        return
    twr_ref, twi_ref, o_ref = rest
    half = y.shape[0] // 2
    reps = y.shape[1] // LANE
    tr, ti = jnp.tile(twr_ref[...], (1, reps)), jnp.tile(twi_ref[...], (1, reps))
    yr, yi = y[:half], y[half:]
    o_ref[0] = (yr * tr - yi * ti).astype(o_ref.dtype)
    o_ref[1] = (yr * ti + yi * tr).astype(o_ref.dtype)


class PosDft:
    def __init__(self, n):
        self.n = n
        self.n1 = DFT_N1 if n % DFT_N1 == 0 and n // DFT_N1 >= 8 and (n // DFT_N1) % 8 == 0 else 0
        if not self.n1:
            c, s = dft_tables(n, n, n, n ** -0.5)
            self.direct = jnp.concatenate([c, s], axis=1).astype(BF16)
            return
        n1, n2 = self.n1, n // self.n1
        self.n2 = n2
        c1, s1 = dft_tables(n1, n1, n1, n1 ** -0.5)
        self.l1 = jnp.concatenate([jnp.concatenate([c1, s1], axis=1),
                                   jnp.concatenate([-s1, c1], axis=1)], axis=0).astype(BF16)
        tc, ts = dft_tables(n2, n1, n, 1.0)
        self.twr = jnp.broadcast_to(tc[:, :, None], (n2, n1, LANE))
        self.twi = jnp.broadcast_to(-ts[:, :, None], (n2, n1, LANE))
        c2, s2 = dft_tables(n2, n2, n2, n2 ** -0.5)
        self.l3 = jnp.concatenate([c2, s2], axis=1).astype(BF16)


def fourier_mix(u, pos, chan_table):
    b, n, fw = u.shape
    z = matmul(u.reshape(b * n, fw).astype(BF16), chan_table, BF16, planes=2)
    if not pos.n1:
        z = z.reshape(2, b, n, fw).transpose(1, 0, 2, 3).reshape(b, 2 * n, fw)
        return jnp.stack([matmul(pos.direct, z[i], BF16) for i in range(b)], axis=0)
    n1, n2 = pos.n1, pos.n2
    z = z.reshape(2, b, n1, n2 * fw)
    y = pl.pallas_call(
        functools.partial(_dft_stage_kernel, twiddle=True),
        grid=(b, n2),
        in_specs=[pl.BlockSpec((2 * n1, 2 * n1), lambda bi, j: (0, 0)),
                  pl.BlockSpec((None, None, n1, fw), lambda bi, j: (0, bi, 0, j)),
                  pl.BlockSpec((None, None, n1, fw), lambda bi, j: (1, bi, 0, j)),
                  pl.BlockSpec((None, n1, LANE), lambda bi, j: (j, 0, 0)),
                  pl.BlockSpec((None, n1, LANE), lambda bi, j: (j, 0, 0))],
        out_specs=pl.BlockSpec((None, 2, None, n1, fw), lambda bi, j: (bi, 0, j, 0, 0)),
        out_shape=jax.ShapeDtypeStruct((b, 2, n2, n1, fw), BF16),
        compiler_params=_params(("parallel", "parallel")),
        name="dft_stage1",
    )(pos.l1, z, z, pos.twr, pos.twi)
    y = y.reshape(b, 2, n2, n1 * fw)
    tn = _pick(n1 * fw, 4096, LANE)
    out = pl.pallas_call(
        functools.partial(_dft_stage_kernel, twiddle=False),
        grid=(b, n1 * fw // tn),
        in_specs=[pl.BlockSpec((n2, 2 * n2), lambda bi, j: (0, 0)),
                  pl.BlockSpec((None, None, n2, tn), lambda bi, j: (bi, 0, 0, j)),
                  pl.BlockSpec((None, None, n2, tn), lambda bi, j: (bi, 1, 0, j))],
        out_specs=pl.BlockSpec((None, n2, tn), lambda bi, j: (bi, 0, j)),
        out_shape=jax.ShapeDtypeStruct((b, n2, n1 * fw), BF16),
        compiler_params=_params(("parallel", "parallel")),
        name="dft_stage2",
    )(pos.l3, y, y)
    return out.reshape(b, n, fw)


def depthwise_conv(u, w):
    n = u.shape[1]
    kk = w.shape[0]
    pad = kk // 2
    up = jnp.pad(u, ((0, 0), (pad, pad), (0, 0)))
    out = up[:, 0:n] * w[0]
    for i in range(1, kk):
        out = out + up[:, i:i + n] * w[i]
    return out


def kernel(x, c, ctx, c_ctx, ada_a, ada_b, ada_bias, norm1_w, norm2_w, w_in, mu_rkv, mu_lr, decay_w0, decay_w1, decay_w2, iclr_a0, iclr_a1, iclr_a2, ogate_g1, ogate_g2, k_k, k_a, r_k, gn_w, gn_b, vres_mu, vres_v0, vres_v1, vres_v2, conv_w, gate_w1, gate_w2, gate_b, proj_rwkv, proj_fourier, proj_conv, w_out, ffn_w1, ffn_w3, ffn_w2, router_w, router_b, moe_w1, moe_w3, moe_w2, final_norm_w):
    depth = w_in.shape[0]
    bsz, seq, d = x.shape
    ctx_len = ctx.shape[1]
    rw = mu_rkv.shape[-1]
    nh, hd = r_k.shape[1], r_k.shape[2]
    fw = proj_fourier.shape[1]
    cw = proj_conv.shape[1]
    four_off = 3 * rw
    conv_off = four_off + fw
    n_exp = router_w.shape[-1]
    d_exp = moe_w1.shape[-1]
    n_mod = ada_b.shape[-1] // d
    r_dec, r_icl, r_og, r_vr, r_gate = (decay_w1.shape[-1], iclr_a1.shape[-1], ogate_g1.shape[-1],
                                        vres_v1.shape[-1], gate_w1.shape[-1])

    gc = fw // FOURIER_GROUPS
    cc, cs = dft_tables(gc, gc, gc, gc ** -0.5)
    eye_g = jnp.eye(FOURIER_GROUPS, dtype=F32)
    chan_table = jnp.concatenate([jnp.kron(eye_g, cc), -jnp.kron(eye_g, cs)], axis=1).astype(BF16)
    pos_lat, pos_ctx = PosDft(seq), PosDft(ctx_len)

    w_in_b, w_out_b = w_in.astype(BF16), w_out.astype(BF16)
    proj_r_b, proj_f_b, proj_c_b = proj_rwkv.astype(BF16), proj_fourier.astype(BF16), proj_conv.astype(BF16)
    gate_w2_b, gate_b3 = gate_w2.astype(BF16), gate_b[:, None, :]
    ffn_w1_b, ffn_w3_b, ffn_w2_b = ffn_w1.astype(BF16), ffn_w3.astype(BF16), ffn_w2.astype(BF16)
    moe_w1_b = moe_w1.astype(BF16).reshape((-1,) + moe_w1.shape[2:])
    moe_w3_b = moe_w3.astype(BF16).reshape((-1,) + moe_w3.shape[2:])
    moe_w2_b = moe_w2.astype(BF16).reshape(moe_w2.shape[0], n_exp * d_exp, d)
    moe_w2_e = moe_w2_b.reshape(-1, d_exp, d)

    cond_lat = jax.nn.silu(c)
    cond_ctx = jax.nn.silu(c_ctx)[None, :]
    cond = jnp.concatenate([cond_lat, cond_ctx], axis=0)
    cond = jnp.pad(cond, ((0, 16 - cond.shape[0] % 16), (0, 0)))

    def tokens(xs, mod, l, mode, pos_tab, s0, v_first, last_ctx):
        b, t, _ = xs.shape
        has_vres = l > 0
        hd_cat = prep(xs, norm1_w[l], jnp.broadcast_to(mod[0], (b, 1, d)),
                      jnp.broadcast_to(mod[1], (b, 1, d)), mode)

        w_h = [decay_w1[l, 0], decay_w1[l, 1], iclr_a1[l, 0], iclr_a1[l, 1], ogate_g1[l], gate_w1[l]]
        mus = [mu_lr[l, 0], mu_lr[l, 0], mu_lr[l, 1], mu_lr[l, 1], mu_lr[l, 2], None]
        if has_vres:
            w_h.append(vres_v1[l - 1])
            mus.append(vres_mu[l - 1])
        w_dh = [jnp.zeros_like(w) if m is None else w * m[:, None] for w, m in zip(w_h, mus)]
        w_low = jnp.concatenate([jnp.concatenate(w_h, axis=1), jnp.concatenate(w_dh, axis=1)], axis=0)
        n_low = w_low.shape[1]
        w_low = jnp.pad(w_low, ((0, 0), (0, -n_low % LANE)))
        low = matmul(hd_cat, w_low.astype(BF16)).reshape(b, t, -1)
        offs = [0]
        for w in w_h:
            offs.append(offs[-1] + w.shape[1])
        low_g = low[..., offs[4]:offs[5]]
        low_gate = low[..., offs[5]:offs[6]]

        p = matmul(hd_cat, w_in_b, BF16, n_cols=four_off, layer=l).reshape(b, t, -1)
        p_fc = matmul(hd_cat, w_in_b, BF16, n_cols=w_in_b.shape[-1] - four_off,
                      col_off=four_off, layer=l).reshape(b, t, -1)

        r, v, kk, kd_f, kd_r, a_f, a_r, lw_f, lw_r = wkv_prep(
            p, low, v_first, mu_rkv[l], decay_w0[l], iclr_a0[l], k_k[l], k_a[l], decay_w2[l], iclr_a2[l],
            vres_v0[l - 1] if has_vres else None, vres_v2[l - 1] if has_vres else None, offs, hd, mode)

        if s0 is None:
            s0 = [jnp.zeros(wkv_state_shape(b, nh, hd), F32)] * 2
        ys, states = wkv(r, v, kk, [(lw_f, kd_f, a_f), (lw_r, kd_r, a_r)], s0, hd)
        if last_ctx:
            return None, states, v

        y_rwkv = wkv_post(ys[0], ys[1], r, kd_f, kd_r, v, low_g, r_k[l], gn_w[l], gn_b[l],
                          ogate_g2[l], hd)

        y_four = fourier_mix(p_fc[..., :fw], pos_tab, chan_table)
        gate_b_ = p_fc[..., fw:fw + cw].astype(F32)
        gate_c_ = p_fc[..., fw + cw:fw + 2 * cw].astype(F32)
        uu = p_fc[..., fw + 2 * cw:].astype(F32)
        y_conv = gate_b_ * depthwise_conv(gate_c_ * uu, conv_w[l])

        rows = b * t

        def flat(z):
            return z.reshape(rows, -1).astype(BF16)

        merged = gated_merge(flat(y_rwkv), flat(y_four), flat(y_conv), flat(low_gate),
                             proj_r_b, proj_f_b, proj_c_b, gate_w2_b, gate_b3, l)
        xs = resid_matmul(merged, w_out_b, xs.reshape(rows, d),
                          jnp.broadcast_to(mod[2], (b, 1, d)), t, l)

        j = l // 2
        dense = l % 2 == 0
        h2, logits = norm_mod(xs, norm2_w[l], jnp.broadcast_to(mod[3], (b, 1, d)),
                              jnp.broadcast_to(mod[4], (b, 1, d)), t,
                              None if dense else router_w[j], None if dense else router_b[j])
        if dense:
            hid = swiglu_up(h2, ffn_w1_b, ffn_w3_b, e0=j, n_e=1)
            w2 = ffn_w2_b
        else:
            top_val, top_idx = lax.top_k(logits, TOP_K)
            weights = jax.nn.softmax(top_val, axis=-1)
            if rows >= MOE_SPARSE_MIN_ROWS:
                src, row_gate, tile_e, dest = moe_dispatch(top_idx, weights, n_exp, MOE_TILE)
                y = grouped_swiglu(h2[src], tile_e + j * n_exp,
                                   jnp.broadcast_to(row_gate[:, None], (row_gate.shape[0], LANE)),
                                   moe_w1_b, moe_w3_b, moe_w2_e, MOE_TILE)
                f = sum(y[dest[:, s]].astype(F32) for s in range(TOP_K)).reshape(b, t, d)
                return xs.reshape(b, t, d) + mod[5] * f, states, v
            gate = jnp.sum(jax.nn.one_hot(top_idx, n_exp, dtype=F32) * weights[..., None], axis=-2)
            gate_rep = jnp.repeat(gate, LANE, axis=-1)
            hid = swiglu_up(h2, moe_w1_b, moe_w3_b, gate_rep, e0=j * n_exp, n_e=n_exp)
            w2 = moe_w2_b
        xs = resid_matmul(hid, w2, xs, jnp.broadcast_to(mod[5], (b, 1, d)), t, j)
        return xs.reshape(b, t, d), states, v

    xl, xc = x, ctx
    v_first_l = v_first_c = None
    for l in range(depth):
        last = l == depth - 1
        m = mm(mm(cond, ada_a[l]), ada_b[l]) + ada_bias[l]
        mod_l = [m[:bsz, None, i * d:(i + 1) * d] for i in range(n_mod)]
        mod_c = [m[bsz:bsz + 1, None, i * d:(i + 1) * d] for i in range(n_mod)]

        xc_new, ctx_states, vc = tokens(xc, mod_c, l, "seq", pos_ctx, None, v_first_c, last)
        xl, _, vl = tokens(xl, mod_l, l, "grid", pos_lat, ctx_states, v_first_l, False)
        if l == 0:
            v_first_c, v_first_l = vc, vl
        if not last:
            xc = xc_new
    return rmsnorm(xl, final_norm_w)
```

```python
import functools
import math

import jax
import jax.numpy as jnp
from jax import lax
from jax.experimental import pallas as pl
from jax.experimental.pallas import tpu as pltpu

F32 = jnp.float32
BF16 = jnp.bfloat16

GRID_W = 64
FOURIER_GROUPS = 8
TOP_K = 2
NORM_EPS = 1e-6
GN_EPS = 64e-5
DECAY_OFFSET = 0.5
WKV_CHUNK = 64
WKV_HEADS = 32
MOE_TILE = 512
MOE_SPARSE_MIN_ROWS = 4096
DFT_N1 = 128
PREP_ROWS = 256
POST_ROWS = 256
LANE = 128
VMEM_LIMIT = 56 * 1024 * 1024

HI = lax.Precision.HIGHEST


def _sigmoid(x):
    return 0.5 * jnp.tanh(0.5 * x) + 0.5


def _pick(dim, target, align):
    if dim <= target:
        return dim
    t = (target // align) * align
    while t >= align:
        if dim % t == 0:
            return t
        t -= align
    return dim


def _mm_kernel(a_ref, b_ref, o_ref, acc_ref, *, nk):
    k = pl.program_id(2)

    @pl.when(k == 0)
    def _():
        acc_ref[...] = jnp.zeros_like(acc_ref)

    acc_ref[...] += jnp.dot(a_ref[...], b_ref[...], preferred_element_type=F32)

    @pl.when(k == nk - 1)
    def _():
        o_ref[...] = acc_ref[...].astype(o_ref.dtype)


def matmul(a, b, out_dtype=F32, tm=1024, tn=1024, tk=2048, n_cols=None, col_off=0, planes=1, layer=None):
    m = a.shape[0]
    k = b.shape[-2]
    n = b.shape[-1] if n_cols is None else n_cols
    tm = _pick(m, tm, 16)
    tn = _pick(math.gcd(n // planes, col_off) if col_off else n // planes, tn, LANE)
    tk = _pick(k, tk, LANE)
    nk = k // tk
    joff = col_off // tn
    if planes == 1:
        out_spec = pl.BlockSpec((tm, tn), lambda i, j, kk: (i, j))
        out_shape = jax.ShapeDtypeStruct((m, n), out_dtype)
    else:
        per = n // planes // tn
        out_spec = pl.BlockSpec((None, tm, tn), lambda i, j, kk: (j // per, i, j % per))
        out_shape = jax.ShapeDtypeStruct((planes, m, n // planes), out_dtype)
    if layer is None:
        b_spec = pl.BlockSpec((tk, tn), lambda i, j, kk: (kk, j + joff))
    else:
        b_spec = pl.BlockSpec((None, tk, tn), lambda i, j, kk: (layer, kk, j + joff))
    return pl.pallas_call(
        functools.partial(_mm_kernel, nk=nk),
        grid=(m // tm, n // tn, nk),
        in_specs=[pl.BlockSpec((tm, tk), lambda i, j, kk: (i, kk)), b_spec],
        out_specs=out_spec,
        out_shape=out_shape,
        scratch_shapes=[pltpu.VMEM((tm, tn), F32)],
        compiler_params=pltpu.CompilerParams(
            dimension_semantics=("parallel", "parallel", "arbitrary"),
            vmem_limit_bytes=VMEM_LIMIT),
        name="matmul",
    )(a, b)


def mm(a, b, out_dtype=F32, **kw):
    lead = a.shape[:-1]
    out = matmul(a.reshape(-1, a.shape[-1]).astype(BF16), b.astype(BF16), out_dtype, **kw)
    return out.reshape(lead + (b.shape[-1],))


def _params(sem):
    return pltpu.CompilerParams(dimension_semantics=sem, vmem_limit_bytes=VMEM_LIMIT)


def _swiglu_up_kernel(a_ref, w1_ref, w3_ref, *rest, nk, gated):
    if gated:
        g_ref, o_ref, acc1_ref, acc3_ref = rest
    else:
        o_ref, acc1_ref, acc3_ref = rest
    k = pl.program_id(2)

    @pl.when(k == 0)
    def _():
        acc1_ref[...] = jnp.zeros_like(acc1_ref)
        acc3_ref[...] = jnp.zeros_like(acc3_ref)

    a = a_ref[...]
    acc1_ref[...] += jnp.dot(a, w1_ref[...], preferred_element_type=F32)
    acc3_ref[...] += jnp.dot(a, w3_ref[...], preferred_element_type=F32)

    @pl.when(k == nk - 1)
    def _():
        h1 = acc1_ref[...]
        hid = h1 * _sigmoid(h1) * acc3_ref[...]
        if gated:
            hid = hid * jnp.tile(g_ref[...], (1, hid.shape[1] // LANE))
        o_ref[...] = hid.astype(o_ref.dtype)


def swiglu_up(a, w1, w3, gate_rep=None, e0=0, n_e=None):
    m, k = a.shape
    f = w1.shape[-1]
    n_e = w1.shape[0] if n_e is None else n_e
    n = n_e * f
    tm = _pick(m, 1024, 16)
    tk = _pick(k, 2048, LANE)
    tn = _pick(f, 1024, LANE)
    gated = gate_rep is not None
    nk = k // tk
    per = f // tn
    w_spec = pl.BlockSpec((None, tk, tn), lambda i, j, kk: (e0 + j // per, kk, j % per))
    in_specs = [pl.BlockSpec((tm, tk), lambda i, j, kk: (i, kk)), w_spec, w_spec]
    args = [a, w1, w3]
    if gated:
        in_specs.append(pl.BlockSpec((tm, LANE), lambda i, j, kk: (i, j // per)))
        args.append(gate_rep)
    return pl.pallas_call(
        functools.partial(_swiglu_up_kernel, nk=nk, gated=gated),
        grid=(m // tm, n // tn, nk),
        in_specs=in_specs,
        out_specs=pl.BlockSpec((tm, tn), lambda i, j, kk: (i, j)),
        out_shape=jax.ShapeDtypeStruct((m, n), BF16),
        scratch_shapes=[pltpu.VMEM((tm, tn), F32), pltpu.VMEM((tm, tn), F32)],
        compiler_params=_params(("parallel", "parallel", "arbitrary")),
        name="swiglu_up",
    )(*args)


def _resid_kernel(a_ref, w_ref, x_ref, g_ref, o_ref, acc_ref, *, nk):
    k = pl.program_id(2)

    @pl.when(k == 0)
    def _():
        acc_ref[...] = jnp.zeros_like(acc_ref)

    acc_ref[...] += jnp.dot(a_ref[...], w_ref[...], preferred_element_type=F32)

    @pl.when(k == nk - 1)
    def _():
        o_ref[...] = x_ref[...] + g_ref[0] * acc_ref[...]


def resid_matmul(a, w, x, g, rows_per_batch, layer):
    m, k = a.shape
    n = w.shape[-1]
    tm = _pick(rows_per_batch, 1024, 16)
    tn = _pick(n, 1024, LANE)
    tk = _pick(k, 2048, LANE)
    nk = k // tk
    per = rows_per_batch // tm
    return pl.pallas_call(
        functools.partial(_resid_kernel, nk=nk),
        grid=(m // tm, n // tn, nk),
        in_specs=[pl.BlockSpec((tm, tk), lambda i, j, kk: (i, kk)),
                  pl.BlockSpec((None, tk, tn), lambda i, j, kk: (layer, kk, j)),
                  pl.BlockSpec((tm, tn), lambda i, j, kk: (i, j)),
                  pl.BlockSpec((1, 1, tn), lambda i, j, kk: (i // per, 0, j))],
        out_specs=pl.BlockSpec((tm, tn), lambda i, j, kk: (i, j)),
        out_shape=jax.ShapeDtypeStruct((m, n), F32),
        scratch_shapes=[pltpu.VMEM((tm, tn), F32)],
        compiler_params=_params(("parallel", "parallel", "arbitrary")),
        name="resid_matmul",
    )(a, w, x, g)


def _grouped_up_kernel(te_ref, a_ref, w1_ref, w3_ref, o_ref, acc1_ref, acc3_ref, *, nk):
    del te_ref
    k = pl.program_id(2)

    @pl.when(k == 0)
    def _():
        acc1_ref[...] = jnp.zeros_like(acc1_ref)
        acc3_ref[...] = jnp.zeros_like(acc3_ref)

    a = a_ref[...]
    acc1_ref[...] += jnp.dot(a, w1_ref[...], preferred_element_type=F32)
    acc3_ref[...] += jnp.dot(a, w3_ref[...], preferred_element_type=F32)

    @pl.when(k == nk - 1)
    def _():
        h1 = acc1_ref[...]
        o_ref[...] = (h1 * _sigmoid(h1) * acc3_ref[...]).astype(o_ref.dtype)


def _grouped_down_kernel(te_ref, a_ref, w_ref, g_ref, o_ref):
    del te_ref
    y = jnp.dot(a_ref[...], w_ref[...], preferred_element_type=F32)
    o_ref[...] = (y * jnp.tile(g_ref[...], (1, y.shape[1] // LANE))).astype(o_ref.dtype)


def grouped_swiglu(x, tile_expert, row_gate, w1, w3, w2, tm):
    r, d = x.shape
    f = w1.shape[-1]
    tk = _pick(d, 2048, LANE)
    tn = _pick(f, 1024, LANE)
    nk = d // tk
    hid = pl.pallas_call(
        functools.partial(_grouped_up_kernel, nk=nk),
        grid_spec=pltpu.PrefetchScalarGridSpec(
            num_scalar_prefetch=1,
            grid=(r // tm, f // tn, nk),
            in_specs=[pl.BlockSpec((tm, tk), lambda i, j, kk, te: (i, kk)),
                      pl.BlockSpec((None, tk, tn), lambda i, j, kk, te: (te[i], kk, j)),
                      pl.BlockSpec((None, tk, tn), lambda i, j, kk, te: (te[i], kk, j))],
            out_specs=pl.BlockSpec((tm, tn), lambda i, j, kk, te: (i, j)),
            scratch_shapes=[pltpu.VMEM((tm, tn), F32), pltpu.VMEM((tm, tn), F32)]),
        out_shape=jax.ShapeDtypeStruct((r, f), BF16),
        compiler_params=_params(("parallel", "parallel", "arbitrary")),
        name="grouped_up",
    )(tile_expert, x, w1, w3)
    tn2 = _pick(d, 1024, LANE)
    return pl.pallas_call(
        _grouped_down_kernel,
        grid_spec=pltpu.PrefetchScalarGridSpec(
            num_scalar_prefetch=1,
            grid=(r // tm, d // tn2),
            in_specs=[pl.BlockSpec((tm, f), lambda i, j, te: (i, 0)),
                      pl.BlockSpec((None, f, tn2), lambda i, j, te: (te[i], 0, j)),
                      pl.BlockSpec((tm, LANE), lambda i, j, te: (i, 0))],
            out_specs=pl.BlockSpec((tm, tn2), lambda i, j, te: (i, j))),
        out_shape=jax.ShapeDtypeStruct((r, d), BF16),
        compiler_params=_params(("parallel", "parallel")),
        name="grouped_down",
    )(tile_expert, hid, w2, row_gate)


def moe_dispatch(top_idx, weights, n_exp, tm):
    m, k = top_idx.shape
    n = m * k
    e_flat = top_idx.reshape(n)
    order = jnp.argsort(e_flat, stable=True)
    e_sorted = e_flat[order]
    counts = jnp.sum(jax.nn.one_hot(e_flat, n_exp, dtype=jnp.int32), axis=0)
    padded = (counts + tm - 1) // tm * tm
    pad_end = jnp.cumsum(padded)
    pad_start = pad_end - padded
    start = jnp.cumsum(counts) - counts
    dest_sorted = pad_start[e_sorted] + jnp.arange(n, dtype=jnp.int32) - start[e_sorted]
    r = n + n_exp * tm

    def expert_of(rows):
        return jnp.minimum(jnp.sum(rows[:, None] >= pad_end[None, :], axis=1), n_exp - 1).astype(jnp.int32)

    rows = jnp.arange(r, dtype=jnp.int32)
    e_row = expert_of(rows)
    within = rows - pad_start[e_row]
    valid = within < counts[e_row]
    src_sorted = order[jnp.clip(start[e_row] + within, 0, n - 1)]
    src_token = jnp.where(valid, src_sorted // k, 0)
    row_gate = jnp.where(valid, weights.reshape(n)[src_sorted], 0.0)
    tile_expert = expert_of(jnp.arange(r // tm, dtype=jnp.int32) * tm)
    dest = dest_sorted[jnp.argsort(order)].reshape(m, k)
    return src_token, row_gate, tile_expert, dest


def _merge_kernel(yr_ref, yf_ref, yc_ref, lg_ref, pr_ref, pf_ref, pc_ref,
                  gwr_ref, gwf_ref, gwc_ref, gbr_ref, gbf_ref, gbc_ref, o_ref):
    lg = lg_ref[...]

    def branch(y_ref, p_ref, gw_ref, gb_ref):
        gate = _sigmoid(jnp.dot(lg, gw_ref[...], preferred_element_type=F32) + gb_ref[...])
        return gate * jnp.dot(y_ref[...], p_ref[...], preferred_element_type=F32)

    out = (branch(yr_ref, pr_ref, gwr_ref, gbr_ref) + branch(yf_ref, pf_ref, gwf_ref, gbf_ref)
           + branch(yc_ref, pc_ref, gwc_ref, gbc_ref))
    o_ref[...] = out.astype(o_ref.dtype)


def gated_merge(yr, yf, yc, lg, pr, pf, pc, gw2, gb, layer):
    m = yr.shape[0]
    d = pr.shape[-1]
    tm = _pick(m, 1024, 16)
    tn = _pick(d, 512, LANE)
    nj = d // tn

    def rows(arr):
        return pl.BlockSpec((tm, arr.shape[1]), lambda i, j: (i, 0))

    def cols(arr, off):
        return pl.BlockSpec((None, arr.shape[1], tn), lambda i, j: (layer, 0, off * nj + j))

    return pl.pallas_call(
        _merge_kernel,
        grid=(m // tm, nj),
        in_specs=[rows(yr), rows(yf), rows(yc), rows(lg), cols(pr, 0), cols(pf, 0), cols(pc, 0),
                  cols(gw2, 0), cols(gw2, 1), cols(gw2, 2), cols(gb, 0), cols(gb, 1), cols(gb, 2)],
        out_specs=pl.BlockSpec((tm, tn), lambda i, j: (i, j)),
        out_shape=jax.ShapeDtypeStruct((m, d), BF16),
        compiler_params=_params(("parallel", "parallel")),
        name="gated_merge",
    )(yr, yf, yc, lg, pr, pf, pc, gw2, gw2, gw2, gb, gb, gb)


def _bf(x):
    return x.astype(BF16)


def _dot(a, b):
    return jnp.dot(_bf(a), _bf(b), preferred_element_type=F32)


def _dot_nt(a, b):
    return lax.dot_general(_bf(a), _bf(b), (((1,), (1,)), ((), ())), preferred_element_type=F32)


def _dot_tn(a, b):
    return lax.dot_general(_bf(a), _bf(b), (((0,), (0,)), ((), ())), preferred_element_type=F32)


def _each(f, *lists):
    return [f(*xs) for xs in zip(*lists)]


def _block_diag(x, width, pack):
    if pack == 1:
        return x
    lane_g = lax.broadcasted_iota(jnp.int32, x.shape, 1) // width
    zero = jnp.zeros_like(x)
    return jnp.concatenate([jnp.where(lane_g == g, x, zero) for g in range(pack)], axis=0)


def _unit_tri_inverse(a_tri, row, col, size, pack):
    def same_block(shift):
        return (row >> shift) == (col >> shift)

    def idot(a, b):
        return _dot(a, _block_diag(_bf(b), size, pack))

    eye = (row == col).astype(F32)
    a8 = _each(lambda a: jnp.where(same_block(3), a, 0.0), a_tri)
    a8_2 = _each(idot, a8, a8)
    a8_4 = _each(idot, a8_2, a8_2)
    x = _each(lambda a: eye - a, a8)
    x = _each(lambda xx, d: xx + d, x, _each(idot, x, a8_2))
    x = _each(lambda xx, d: xx + d, x, _each(idot, x, a8_4))
    shift = 3
    while (1 << shift) < size:
        off = same_block(shift + 1) & jnp.logical_not(same_block(shift))
        e = _each(lambda a: jnp.where(off, a, 0.0), a_tri)
        ex = _each(idot, e, x)
        x = _each(lambda xx, d: xx - d, x, _each(idot, x, ex))
        shift += 1
    return x


def _wkv_kernel(*refs, chunk, hd, heads, pack, dirs, nc):
    nd = len(dirs)
    ins = [refs[6 * d:6 * d + 6] for d in range(nd)]
    s0_refs = refs[6 * nd:7 * nd]
    y_refs = refs[7 * nd:8 * nd]
    sf_refs = refs[8 * nd:9 * nd]
    st_ref, al_ref, be_ref, rt_ref, kt_ref, vb_ref, pl_ref = refs[9 * nd:]
    c = pl.program_id(2)

    @pl.when(c == 0)
    def _():
        for d in range(nd):
            st_ref[d] = s0_refs[d][0]

    row = lax.broadcasted_iota(jnp.int32, (chunk, pack * chunk), 0)
    col = lax.broadcasted_iota(jnp.int32, (chunk, pack * chunk), 1) % chunk
    width = pack * hd
    groups = heads // pack
    sls = [slice(g * width, (g + 1) * width) for g in range(groups)]
    strict, incl = [], []
    for d, reverse in enumerate(dirs):
        st_d, in_d = (col > row, col >= row) if reverse else (col < row, col <= row)
        strict += [st_d] * groups
        incl += [in_d] * groups
        tri = jnp.where(in_d[:, :chunk], 1.0, 0.0).astype(BF16)
        last = 0 if reverse else chunk - 1
        r_ref, lw_ref, k_ref, v_ref, kk_ref, a_ref = ins[d]

        lw = lw_ref[0]
        lw_hi = _bf(lw)
        rem = lw - lw_hi.astype(F32)
        lw_mid = _bf(rem)
        lw_lo = _bf(rem - lw_mid.astype(F32))
        cum = (jnp.dot(tri, lw_hi, preferred_element_type=F32)
               + jnp.dot(tri, lw_mid, preferred_element_type=F32)
               + jnp.dot(tri, lw_lo, preferred_element_type=F32))
        p_in = jnp.exp(cum)
        p_inv = jnp.exp(-cum)
        kk = kk_ref[0].astype(F32)
        rt_ref[d] = _bf(r_ref[0].astype(F32) * p_in)
        kt_ref[d] = _bf(k_ref[0].astype(F32) * p_inv)
        be_ref[d] = _bf(a_ref[0].astype(F32) * kk * p_inv)
        al_ref[d] = _bf(kk * jnp.exp(cum - lw))
        vb_ref[d] = _bf(v_ref[0])
        pl_ref[d] = jnp.broadcast_to(p_in[last:last + 1, :], pl_ref.shape[1:])

    ent = [(d, g) for d in range(nd) for g in range(groups)]
    alpha = [al_ref[d, :, sls[g]] for d, g in ent]
    beta = [be_ref[d, :, sls[g]] for d, g in ent]
    rt = [rt_ref[d, :, sls[g]] for d, g in ent]
    kt = [kt_ref[d, :, sls[g]] for d, g in ent]
    v = [vb_ref[d, :, sls[g]] for d, g in ent]
    st0 = [st_ref[d, g] for d, g in ent]
    st0_b = _each(_bf, st0)

    def bd_keys(x):
        return _block_diag(x, hd, pack)

    def bdot(a, x):
        return _dot(a, _block_diag(_bf(x), hd, pack))

    ar = _each(lambda x, y: jnp.concatenate([x, y], axis=0), alpha, rt)
    x_b = _each(_dot_nt, ar, _each(bd_keys, beta))
    x_k = _each(_dot_nt, ar, _each(bd_keys, kt))
    a_ab = _each(lambda x, m: jnp.where(m, x[:chunk], 0.0), x_b, strict)
    a_rb = _each(lambda x, m: jnp.where(m, x[chunk:], 0.0), x_b, incl)
    a_ak = _each(lambda x, m: jnp.where(m, x[:chunk], 0.0), x_k, strict)
    a_rk = _each(lambda x, m: jnp.where(m, x[chunk:], 0.0), x_k, incl)
    t_inv = _unit_tri_inverse(a_ab, row, col, chunk, pack)

    w_t = _each(bdot, t_inv, alpha)
    u0 = _each(bdot, t_inv, _each(bdot, a_ak, v))
    y0 = _each(bdot, a_rk, v)
    ktv = _each(_dot_tn, kt, v)
    u = _each(lambda x, y: x + y, _each(_dot, w_t, st0_b), u0)
    y1 = _each(_dot, rt, st0_b)
    y2 = _each(bdot, a_rb, u)
    btu = _each(_dot_tn, beta, u)
    p_col = [jnp.transpose(pl_ref[d, :, sls[g]])[:, :1] for d, g in ent]
    own = (lax.broadcasted_iota(jnp.int32, (width, width), 0) // hd
           == lax.broadcasted_iota(jnp.int32, (width, width), 1) // hd)
    for n, (d, g) in enumerate(ent):
        y_refs[d][0, :, sls[g]] = y0[n] + y1[n] - y2[n]
        st_ref[d, g] = (st0[n] + jnp.where(own, ktv[n] - btu[n], 0.0)) * p_col[n]

    @pl.when(c == nc - 1)
    def _():
        for d in range(nd):
            sf_refs[d][0] = st_ref[d]


def wkv_pack(nh, hd):
    pack = LANE // hd if hd < LANE and LANE % hd == 0 and hd == WKV_CHUNK else 1
    return pack if nh % pack == 0 else 1


def wkv_state_shape(b, nh, hd):
    pack = wkv_pack(nh, hd)
    return (b, nh // pack, pack * hd, pack * hd)


def wkv(r, v, kk, per_dir, s0, hd, dirs=(False, True)):
    b, t, width = r.shape
    nh = width // hd
    pack = wkv_pack(nh, hd)
    heads = max(hh for hh in (1, 2, 4, 8, 16, 32) if nh % hh == 0 and hh <= WKV_HEADS and hh % pack == 0)
    chunk = WKV_CHUNK
    nc = t // chunk
    blk = heads * hd
    groups = heads // pack
    gw = pack * hd
    nd = len(dirs)

    def tok_spec(reverse):
        return pl.BlockSpec((1, chunk, blk), lambda bi, hi, ci: (bi, nc - 1 - ci if reverse else ci, hi))

    st_spec = pl.BlockSpec((1, groups, gw, gw), lambda bi, hi, ci: (bi, hi, 0, 0))
    in_specs, args = [], []
    for d, reverse in enumerate(dirs):
        lw, k, a = per_dir[d]
        in_specs += [tok_spec(reverse)] * 6
        args += [r, lw, k, v, kk, a]
    in_specs += [st_spec] * nd
    args += list(s0)
    outs = pl.pallas_call(
        functools.partial(_wkv_kernel, chunk=chunk, hd=hd, heads=heads, pack=pack, dirs=tuple(dirs), nc=nc),
        grid=(b, nh // heads, nc),
        in_specs=in_specs,
        out_specs=[tok_spec(reverse) for reverse in dirs] + [st_spec] * nd,
        out_shape=[jax.ShapeDtypeStruct((b, t, width), F32)] * nd
        + [jax.ShapeDtypeStruct(wkv_state_shape(b, nh, hd), F32)] * nd,
        scratch_shapes=[pltpu.VMEM((nd, groups, gw, gw), F32)]
        + [pltpu.VMEM((nd, chunk, blk), BF16)] * 5 + [pltpu.VMEM((nd, 8, blk), F32)],
        compiler_params=pltpu.CompilerParams(
            dimension_semantics=("parallel", "parallel", "arbitrary"),
            vmem_limit_bytes=VMEM_LIMIT),
        name="wkv",
    )(*args)
    return outs[:nd], outs[nd:]


def _from_prev(x):
    return pltpu.roll(x, 1, axis=0)


def _from_next(x):
    return pltpu.roll(x, x.shape[0] - 1, axis=0)


def _token_shift(x, up, dn, has_up, has_dn, mode):
    rows, ch = x.shape
    if mode == "seq":
        half = ch // 2
        t = lax.broadcasted_iota(jnp.int32, (rows, half), 0)
        s0 = jnp.where(t == 0, 0.0, _from_prev(x[:, :half]))
        s1 = jnp.where(t == rows - 1, 0.0, _from_next(x[:, half:]))
        return [s0, s1]
    q = ch // 4
    t = lax.broadcasted_iota(jnp.int32, (rows, q), 0) % GRID_W
    s0 = jnp.where(t == 0, 0.0, _from_prev(x[:, :q]))
    s1 = jnp.where(t == GRID_W - 1, 0.0, _from_next(x[:, q:2 * q]))
    up = jnp.where(has_up, up, 0.0)
    dn = jnp.where(has_dn, dn, 0.0)
    if rows == GRID_W:
        s2, s3 = up, dn
    else:
        s2 = jnp.concatenate([up, x[:rows - GRID_W, 2 * q:3 * q]], axis=0)
        s3 = jnp.concatenate([x[GRID_W:, 3 * q:], dn], axis=0)
    return [s0, s1, s2, s3]


def _prep_kernel(*refs, mode, nblk):
    if mode == "grid":
        x_ref, up_ref, dn_ref, w_ref, sh_ref, sc_ref, o_ref = refs
    else:
        x_ref, w_ref, sh_ref, sc_ref, o_ref = refs
    i = pl.program_id(1)
    w, sh, sc = w_ref[...], sh_ref[0], sc_ref[0]

    def modulated(x):
        y = x * lax.rsqrt(jnp.mean(x * x, axis=-1, keepdims=True) + NORM_EPS)
        return (y * w) * (1 + sc) + sh

    h = modulated(x_ref[0])
    d = h.shape[1]
    if mode == "grid":
        q = d // 4
        up = modulated(up_ref[0])[:, 2 * q:3 * q]
        dn = modulated(dn_ref[0])[:, 3 * q:]
        parts = _token_shift(h, up, dn, i > 0, i < nblk - 1, mode)
    else:
        parts = _token_shift(h, None, None, None, None, mode)
    o_ref[:, :d] = h.astype(o_ref.dtype)
    width = d // len(parts)
    for n, s in enumerate(parts):
        lo = n * width
        o_ref[:, d + lo:d + lo + width] = (s - h[:, lo:lo + width]).astype(o_ref.dtype)


def prep(x, w, shift, scale, mode):
    b, t, d = x.shape
    rows = _pick(t, PREP_ROWS, GRID_W) if mode == "grid" else t
    nblk = t // rows
    per = rows // GRID_W
    x_spec = pl.BlockSpec((1, rows, d), lambda bi, i: (bi, i, 0))
    vec_spec = pl.BlockSpec((1, d), lambda bi, i: (0, 0))
    mod_spec = pl.BlockSpec((1, 1, d), lambda bi, i: (bi, 0, 0))
    in_specs, args = [x_spec], [x]
    if mode == "grid":
        last = t // GRID_W - 1
        in_specs += [pl.BlockSpec((1, GRID_W, d), lambda bi, i: (bi, jnp.maximum(i * per - 1, 0), 0)),
                     pl.BlockSpec((1, GRID_W, d), lambda bi, i: (bi, jnp.minimum((i + 1) * per, last), 0))]
        args += [x, x]
    in_specs += [vec_spec, mod_spec, mod_spec]
    args += [w[None, :], shift, scale]
    return pl.pallas_call(
        functools.partial(_prep_kernel, mode=mode, nblk=nblk),
        grid=(b, nblk),
        in_specs=in_specs,
        out_specs=pl.BlockSpec((rows, 2 * d), lambda bi, i: (bi * nblk + i, 0)),
        out_shape=jax.ShapeDtypeStruct((b * t, 2 * d), BF16),
        compiler_params=_params(("parallel", "parallel")),
        name="prep_" + mode,
    )(*args)


def _norm_mod_kernel(x_ref, w_ref, sh_ref, sc_ref, *rest, routed):
    x = x_ref[...]
    y = x * lax.rsqrt(jnp.mean(x * x, axis=-1, keepdims=True) + NORM_EPS)
    h = (y * w_ref[...]) * (1 + sc_ref[0]) + sh_ref[0]
    if routed:
        rw_ref, rb_ref, o_ref, lg_ref = rest
        lg_ref[...] = jnp.dot(h, rw_ref[...], precision=HI, preferred_element_type=F32) + rb_ref[...]
    else:
        (o_ref,) = rest
    o_ref[...] = h.astype(o_ref.dtype)


def norm_mod(x, w, shift, scale, rows_per_batch, router_w=None, router_b=None):
    m, d = x.shape
    rows = _pick(rows_per_batch, PREP_ROWS, 8)
    nblk = rows_per_batch // rows
    routed = router_w is not None
    x_spec = pl.BlockSpec((rows, d), lambda bi, i: (bi * nblk + i, 0))
    mod_spec = pl.BlockSpec((1, 1, d), lambda bi, i: (bi, 0, 0))
    in_specs = [x_spec, pl.BlockSpec((1, d), lambda bi, i: (0, 0)), mod_spec, mod_spec]
    args = [x, w[None, :], shift, scale]
    out_specs, out_shape = [x_spec], [jax.ShapeDtypeStruct((m, d), BF16)]
    if routed:
        n_e = router_w.shape[1]
        pad = -n_e % LANE
        in_specs += [pl.BlockSpec((d, n_e + pad), lambda bi, i: (0, 0)),
                     pl.BlockSpec((1, n_e + pad), lambda bi, i: (0, 0))]
        args += [jnp.pad(router_w, ((0, 0), (0, pad))), jnp.pad(router_b, (0, pad))[None, :]]
        out_specs.append(pl.BlockSpec((rows, n_e + pad), lambda bi, i: (bi * nblk + i, 0)))
        out_shape.append(jax.ShapeDtypeStruct((m, n_e + pad), F32))
    out = pl.pallas_call(
        functools.partial(_norm_mod_kernel, routed=routed),
        grid=(m // rows_per_batch, nblk),
        in_specs=in_specs,
        out_specs=out_specs,
        out_shape=out_shape,
        compiler_params=_params(("parallel", "parallel")),
        name="norm_mod",
    )(*args)
    return (out[0], out[1][:, :router_w.shape[1]]) if routed else (out[0], None)


def _head_sums(x, hd):
    rows, width = x.shape
    nt = width // LANE
    stacked = jnp.concatenate([x[:, i * LANE:(i + 1) * LANE] for i in range(nt)], axis=0)
    li = lax.broadcasted_iota(jnp.int32, (LANE, LANE), 0) // hd
    lj = lax.broadcasted_iota(jnp.int32, (LANE, LANE), 1) // hd
    ones = jnp.where(li == lj, 1.0, 0.0).astype(BF16)
    hi = _bf(stacked)
    lo = _bf(stacked - hi.astype(F32))
    s = jnp.dot(hi, ones, preferred_element_type=F32) + jnp.dot(lo, ones, preferred_element_type=F32)
    return jnp.concatenate([s[i * rows:(i + 1) * rows] for i in range(nt)], axis=1)


def _wkv_prep_kernel(*refs, mode, nblk, hd, offs, has_vres):
    refs = list(refs)
    main = [refs.pop(0) for _ in range(3)]
    halo = [(refs.pop(0), refs.pop(0)) for _ in range(3)] if mode == "grid" else [(None, None)] * 3
    low_ref = refs.pop(0)
    vf_ref = refs.pop(0) if has_vres else None
    mu_ref, w0_ref, a0_ref, kk_ref, ka_ref, w2_ref, a2_ref = [refs.pop(0) for _ in range(7)]
    if has_vres:
        v0_ref, v2_ref = refs.pop(0), refs.pop(0)
    r_o, v_o, kk_o, kdf_o, kdr_o, af_o, ar_o, lwf_o, lwr_o = refs
    c = pl.program_id(1)
    low = low_ref[0]

    def lerp(n):
        x = main[n][0].astype(F32)
        up, dn = ((halo[n][0][0].astype(F32), halo[n][1][0].astype(F32)) if mode == "grid"
                  else (None, None))
        sh = jnp.concatenate(_token_shift(x, up, dn, c > 0, c < nblk - 1, mode), axis=1)
        return x + (sh - x) * mu_ref[n:n + 1, :]

    def low_dot(lo, hi_, w, act=None):
        z = low[:, lo:hi_]
        if act is not None:
            z = act(z)
        return jnp.dot(_bf(z), w, preferred_element_type=F32)

    r = lerp(0)
    k = lerp(1)
    v = lerp(2)
    if has_vres:
        gate = _sigmoid(v0_ref[...] + low_dot(offs[6], offs[7], v2_ref[...]))
        v = v + (vf_ref[0].astype(F32) - v) * gate
    kq = k * kk_ref[...]
    kk = kq * lax.rsqrt(jnp.maximum(_head_sums(kq * kq, hd), 1e-24))
    r_o[0] = r.astype(r_o.dtype)
    v_o[0] = v.astype(v_o.dtype)
    kk_o[0] = kk.astype(kk_o.dtype)
    for di, (kd_o, a_o, lw_o) in enumerate(((kdf_o, af_o, lwf_o), (kdr_o, ar_o, lwr_o))):
        wl = w0_ref[di:di + 1, :] + low_dot(offs[di], offs[di + 1], w2_ref[di], jnp.tanh)
        lw_o[0] = _sigmoid(wl) * (-math.exp(-DECAY_OFFSET))
        a = _sigmoid(a0_ref[di:di + 1, :] + low_dot(offs[2 + di], offs[3 + di], a2_ref[di]))
        a_o[0] = a.astype(a_o.dtype)
        kd_o[0] = (k * (1.0 + (a - 1.0) * ka_ref[...])).astype(kd_o.dtype)


def wkv_prep(p, low, v_first, mu, w0, a0, k_k, k_a, w2, a2, v0, v2, offs, hd, mode):
    b, t, rw3 = p.shape
    rw = rw3 // 3
    q = rw // 4
    rows = GRID_W if mode == "grid" else t
    nblk = t // rows
    has_vres = v_first is not None

    def tok(width, col):
        return pl.BlockSpec((1, rows, width), lambda bi, ci: (bi, ci, col))

    def full(arr):
        nd = arr.ndim
        return pl.BlockSpec(arr.shape, lambda bi, ci: (0,) * nd)

    in_specs = [tok(rw, n) for n in range(3)]
    args = [p, p, p]
    if mode == "grid":
        for n in range(3):
            in_specs += [pl.BlockSpec((1, rows, q), lambda bi, ci, n=n: (bi, jnp.maximum(ci - 1, 0), 4 * n + 2)),
                         pl.BlockSpec((1, rows, q), lambda bi, ci, n=n: (bi, jnp.minimum(ci + 1, nblk - 1), 4 * n + 3))]
            args += [p, p]
    in_specs.append(tok(low.shape[-1], 0))
    args.append(low)
    if has_vres:
        in_specs.append(tok(rw, 0))
        args.append(v_first)
    small = [mu, w0, a0, k_k[None, :], k_a[None, :], w2.astype(BF16), a2.astype(BF16)]
    if has_vres:
        small += [v0[None, :], v2.astype(BF16)]
    in_specs += [full(s) for s in small]
    args += small
    out_spec = tok(rw, 0)
    shapes = [jax.ShapeDtypeStruct((b, t, rw), BF16)] * 7 + [jax.ShapeDtypeStruct((b, t, rw), F32)] * 2
    return pl.pallas_call(
        functools.partial(_wkv_prep_kernel, mode=mode, nblk=nblk, hd=hd, offs=tuple(offs), has_vres=has_vres),
        grid=(b, nblk),
        in_specs=in_specs,
        out_specs=[out_spec] * 9,
        out_shape=shapes,
        compiler_params=_params(("parallel", "parallel")),
        name="wkv_prep_" + mode,
    )(*args)


def _wkv_post_kernel(yf_ref, yr_ref, r_ref, kdf_ref, kdr_ref, v_ref, lg_ref, rk_ref, gw_ref, gb_ref,
                     g2_ref, o_ref, *, hd):
    y = yf_ref[0] + yr_ref[0]
    mean = _head_sums(y, hd) * (1.0 / hd)
    yc = y - mean
    var = _head_sums(yc * yc, hd) * (1.0 / hd)
    y = yc * lax.rsqrt(var + GN_EPS) * gw_ref[...] + gb_ref[...]
    rk = r_ref[0].astype(F32) * (kdf_ref[0].astype(F32) + kdr_ref[0].astype(F32)) * rk_ref[...]
    y = y + _head_sums(rk, hd) * v_ref[0].astype(F32)
    g = jnp.dot(_bf(_sigmoid(lg_ref[0])), g2_ref[...], preferred_element_type=F32)
    o_ref[0] = (y * g).astype(o_ref.dtype)


def wkv_post(y_f, y_r, r, kd_f, kd_r, v, low_g, r_k, gn_w, gn_b, g2, hd):
    b, t, rw = y_f.shape
    rows = _pick(t, POST_ROWS, 8)

    def tok(width):
        return pl.BlockSpec((1, rows, width), lambda bi, ci: (bi, ci, 0))

    def full(arr):
        return pl.BlockSpec(arr.shape, lambda bi, ci: (0, 0))

    small = [r_k.reshape(1, rw), gn_w[None, :], gn_b[None, :], g2.astype(BF16)]
    return pl.pallas_call(
        functools.partial(_wkv_post_kernel, hd=hd),
        grid=(b, t // rows),
        in_specs=[tok(rw)] * 6 + [tok(low_g.shape[-1])] + [full(s) for s in small],
        out_specs=tok(rw),
        out_shape=jax.ShapeDtypeStruct((b, t, rw), BF16),
        compiler_params=_params(("parallel", "parallel")),
        name="wkv_post",
    )(y_f, y_r, r, kd_f, kd_r, v, low_g, *small)


def rmsnorm(x, w):
    y = x * lax.rsqrt(jnp.mean(x * x, axis=-1, keepdims=True) + NORM_EPS)
    return y * w


def modulate(x, w, shift, scale):
    return rmsnorm(x, w) * (1 + scale) + shift


def dft_tables(rows, cols, n, scale):
    j = lax.broadcasted_iota(jnp.int32, (rows, cols), 0)
    k = lax.broadcasted_iota(jnp.int32, (rows, cols), 1)
    ang = ((j * k) % n).astype(F32) * (2.0 * math.pi / n)
    return jnp.cos(ang) * scale, jnp.sin(ang) * scale


def _dft_stage_kernel(l_ref, zr_ref, zi_ref, *rest, twiddle):
    z = jnp.concatenate([zr_ref[...], zi_ref[...]], axis=0)
    y = jnp.dot(l_ref[...], z, preferred_element_type=F32)
    if not twiddle:
        rest[0][...] = y.astype(rest[0].dtype)
        return
    twr_ref, twi_ref, o_ref = rest
    half = y.shape[0] // 2
    reps = y.shape[1] // LANE
    tr, ti = jnp.tile(twr_ref[...], (1, reps)), jnp.tile(twi_ref[...], (1, reps))
    yr, yi = y[:half], y[half:]
    o_ref[0] = (yr * tr - yi * ti).astype(o_ref.dtype)
    o_ref[1] = (yr * ti + yi * tr).astype(o_ref.dtype)


class PosDft:
    def __init__(self, n):
        self.n = n
        self.n1 = DFT_N1 if n % DFT_N1 == 0 and n // DFT_N1 >= 8 and (n // DFT_N1) % 8 == 0 else 0
        if not self.n1:
            c, s = dft_tables(n, n, n, n ** -0.5)
            self.direct = jnp.concatenate([c, s], axis=1).astype(BF16)
            return
        n1, n2 = self.n1, n // self.n1
        self.n2 = n2
        c1, s1 = dft_tables(n1, n1, n1, n1 ** -0.5)
        self.l1 = jnp.concatenate([jnp.concatenate([c1, s1], axis=1),
                                   jnp.concatenate([-s1, c1], axis=1)], axis=0).astype(BF16)
        tc, ts = dft_tables(n2, n1, n, 1.0)
        self.twr = jnp.broadcast_to(tc[:, :, None], (n2, n1, LANE))
        self.twi = jnp.broadcast_to(-ts[:, :, None], (n2, n1, LANE))
        c2, s2 = dft_tables(n2, n2, n2, n2 ** -0.5)
        self.l3 = jnp.concatenate([c2, s2], axis=1).astype(BF16)


def fourier_mix(u, pos, chan_table):
    b, n, fw = u.shape
    z = matmul(u.reshape(b * n, fw).astype(BF16), chan_table, BF16, planes=2)
    if not pos.n1:
        z = z.reshape(2, b, n, fw).transpose(1, 0, 2, 3).reshape(b, 2 * n, fw)
        return jnp.stack([matmul(pos.direct, z[i], BF16) for i in range(b)], axis=0)
    n1, n2 = pos.n1, pos.n2
    z = z.reshape(2, b, n1, n2 * fw)
    y = pl.pallas_call(
        functools.partial(_dft_stage_kernel, twiddle=True),
        grid=(b, n2),
        in_specs=[pl.BlockSpec((2 * n1, 2 * n1), lambda bi, j: (0, 0)),
                  pl.BlockSpec((None, None, n1, fw), lambda bi, j: (0, bi, 0, j)),
                  pl.BlockSpec((None, None, n1, fw), lambda bi, j: (1, bi, 0, j)),
                  pl.BlockSpec((None, n1, LANE), lambda bi, j: (j, 0, 0)),
                  pl.BlockSpec((None, n1, LANE), lambda bi, j: (j, 0, 0))],
        out_specs=pl.BlockSpec((None, 2, None, n1, fw), lambda bi, j: (bi, 0, j, 0, 0)),
        out_shape=jax.ShapeDtypeStruct((b, 2, n2, n1, fw), BF16),
        compiler_params=_params(("parallel", "parallel")),
        name="dft_stage1",
    )(pos.l1, z, z, pos.twr, pos.twi)
    y = y.reshape(b, 2, n2, n1 * fw)
    tn = _pick(n1 * fw, 4096, LANE)
    out = pl.pallas_call(
        functools.partial(_dft_stage_kernel, twiddle=False),
        grid=(b, n1 * fw // tn),
        in_specs=[pl.BlockSpec((n2, 2 * n2), lambda bi, j: (0, 0)),
                  pl.BlockSpec((None, None, n2, tn), lambda bi, j: (bi, 0, 0, j)),
                  pl.BlockSpec((None, None, n2, tn), lambda bi, j: (bi, 1, 0, j))],
        out_specs=pl.BlockSpec((None, n2, tn), lambda bi, j: (bi, 0, j)),
        out_shape=jax.ShapeDtypeStruct((b, n2, n1 * fw), BF16),
        compiler_params=_params(("parallel", "parallel")),
        name="dft_stage2",
    )(pos.l3, y, y)
    return out.reshape(b, n, fw)


def depthwise_conv(u, w):
    n = u.shape[1]
    kk = w.shape[0]
    pad = kk // 2
    up = jnp.pad(u, ((0, 0), (pad, pad), (0, 0)))
    out = up[:, 0:n] * w[0]
    for i in range(1, kk):
        out = out + up[:, i:i + n] * w[i]
    return out


def kernel(x, c, ctx, c_ctx, ada_a, ada_b, ada_bias, norm1_w, norm2_w, w_in, mu_rkv, mu_lr, decay_w0, decay_w1, decay_w2, iclr_a0, iclr_a1, iclr_a2, ogate_g1, ogate_g2, k_k, k_a, r_k, gn_w, gn_b, vres_mu, vres_v0, vres_v1, vres_v2, conv_w, gate_w1, gate_w2, gate_b, proj_rwkv, proj_fourier, proj_conv, w_out, ffn_w1, ffn_w3, ffn_w2, router_w, router_b, moe_w1, moe_w3, moe_w2, final_norm_w):
    depth = w_in.shape[0]
    bsz, seq, d = x.shape
    ctx_len = ctx.shape[1]
    rw = mu_rkv.shape[-1]
    nh, hd = r_k.shape[1], r_k.shape[2]
    fw = proj_fourier.shape[1]
    cw = proj_conv.shape[1]
    four_off = 3 * rw
    conv_off = four_off + fw
    n_exp = router_w.shape[-1]
    d_exp = moe_w1.shape[-1]
    n_mod = ada_b.shape[-1] // d
    r_dec, r_icl, r_og, r_vr, r_gate = (decay_w1.shape[-1], iclr_a1.shape[-1], ogate_g1.shape[-1],
                                        vres_v1.shape[-1], gate_w1.shape[-1])

    gc = fw // FOURIER_GROUPS
    cc, cs = dft_tables(gc, gc, gc, gc ** -0.5)
    eye_g = jnp.eye(FOURIER_GROUPS, dtype=F32)
    chan_table = jnp.concatenate([jnp.kron(eye_g, cc), -jnp.kron(eye_g, cs)], axis=1).astype(BF16)
    pos_lat, pos_ctx = PosDft(seq), PosDft(ctx_len)

    w_in_b, w_out_b = w_in.astype(BF16), w_out.astype(BF16)
    proj_r_b, proj_f_b, proj_c_b = proj_rwkv.astype(BF16), proj_fourier.astype(BF16), proj_conv.astype(BF16)
    gate_w2_b, gate_b3 = gate_w2.astype(BF16), gate_b[:, None, :]
    ffn_w1_b, ffn_w3_b, ffn_w2_b = ffn_w1.astype(BF16), ffn_w3.astype(BF16), ffn_w2.astype(BF16)
    moe_w1_b = moe_w1.astype(BF16).reshape((-1,) + moe_w1.shape[2:])
    moe_w3_b = moe_w3.astype(BF16).reshape((-1,) + moe_w3.shape[2:])
    moe_w2_b = moe_w2.astype(BF16).reshape(moe_w2.shape[0], n_exp * d_exp, d)
    moe_w2_e = moe_w2_b.reshape(-1, d_exp, d)

    cond_lat = jax.nn.silu(c)
    cond_ctx = jax.nn.silu(c_ctx)[None, :]
    cond = jnp.concatenate([cond_lat, cond_ctx], axis=0)
    cond = jnp.pad(cond, ((0, 16 - cond.shape[0] % 16), (0, 0)))

    def tokens(xs, mod, l, mode, pos_tab, s0, v_first, last_ctx):
        b, t, _ = xs.shape
        has_vres = l > 0
        hd_cat = prep(xs, norm1_w[l], jnp.broadcast_to(mod[0], (b, 1, d)),
                      jnp.broadcast_to(mod[1], (b, 1, d)), mode)

        w_h = [decay_w1[l, 0], decay_w1[l, 1], iclr_a1[l, 0], iclr_a1[l, 1], ogate_g1[l], gate_w1[l]]
        mus = [mu_lr[l, 0], mu_lr[l, 0], mu_lr[l, 1], mu_lr[l, 1], mu_lr[l, 2], None]
        if has_vres:
            w_h.append(vres_v1[l - 1])
            mus.append(vres_mu[l - 1])
        w_dh = [jnp.zeros_like(w) if m is None else w * m[:, None] for w, m in zip(w_h, mus)]
        w_low = jnp.concatenate([jnp.concatenate(w_h, axis=1), jnp.concatenate(w_dh, axis=1)], axis=0)
        n_low = w_low.shape[1]
        w_low = jnp.pad(w_low, ((0, 0), (0, -n_low % LANE)))
        low = matmul(hd_cat, w_low.astype(BF16)).reshape(b, t, -1)
        offs = [0]
        for w in w_h:
            offs.append(offs[-1] + w.shape[1])
        low_g = low[..., offs[4]:offs[5]]
        low_gate = low[..., offs[5]:offs[6]]

        p = matmul(hd_cat, w_in_b, BF16, n_cols=four_off, layer=l).reshape(b, t, -1)
        p_fc = matmul(hd_cat, w_in_b, BF16, n_cols=w_in_b.shape[-1] - four_off,
                      col_off=four_off, layer=l).reshape(b, t, -1)

        r, v, kk, kd_f, kd_r, a_f, a_r, lw_f, lw_r = wkv_prep(
            p, low, v_first, mu_rkv[l], decay_w0[l], iclr_a0[l], k_k[l], k_a[l], decay_w2[l], iclr_a2[l],
            vres_v0[l - 1] if has_vres else None, vres_v2[l - 1] if has_vres else None, offs, hd, mode)

        if s0 is None:
            s0 = [jnp.zeros(wkv_state_shape(b, nh, hd), F32)] * 2
        ys, states = wkv(r, v, kk, [(lw_f, kd_f, a_f), (lw_r, kd_r, a_r)], s0, hd)
        if last_ctx:
            return None, states, v

        y_rwkv = wkv_post(ys[0], ys[1], r, kd_f, kd_r, v, low_g, r_k[l], gn_w[l], gn_b[l],
                          ogate_g2[l], hd)

        y_four = fourier_mix(p_fc[..., :fw], pos_tab, chan_table)
        gate_b_ = p_fc[..., fw:fw + cw].astype(F32)
        gate_c_ = p_fc[..., fw + cw:fw + 2 * cw].astype(F32)
        uu = p_fc[..., fw + 2 * cw:].astype(F32)
        y_conv = gate_b_ * depthwise_conv(gate_c_ * uu, conv_w[l])

        rows = b * t

        def flat(z):
            return z.reshape(rows, -1).astype(BF16)

        merged = gated_merge(flat(y_rwkv), flat(y_four), flat(y_conv), flat(low_gate),
                             proj_r_b, proj_f_b, proj_c_b, gate_w2_b, gate_b3, l)
        xs = resid_matmul(merged, w_out_b, xs.reshape(rows, d),
                          jnp.broadcast_to(mod[2], (b, 1, d)), t, l)

        j = l // 2
        dense = l % 2 == 0
        h2, logits = norm_mod(xs, norm2_w[l], jnp.broadcast_to(mod[3], (b, 1, d)),
                              jnp.broadcast_to(mod[4], (b, 1, d)), t,
                              None if dense else router_w[j], None if dense else router_b[j])
        if dense:
            hid = swiglu_up(h2, ffn_w1_b, ffn_w3_b, e0=j, n_e=1)
            w2 = ffn_w2_b
        else:
            top_val, top_idx = lax.top_k(logits, TOP_K)
            weights = jax.nn.softmax(top_val, axis=-1)
            if rows >= MOE_SPARSE_MIN_ROWS:
                src, row_gate, tile_e, dest = moe_dispatch(top_idx, weights, n_exp, MOE_TILE)
                y = grouped_swiglu(h2[src], tile_e + j * n_exp,
                                   jnp.broadcast_to(row_gate[:, None], (row_gate.shape[0], LANE)),
                                   moe_w1_b, moe_w3_b, moe_w2_e, MOE_TILE)
                f = sum(y[dest[:, s]].astype(F32) for s in range(TOP_K)).reshape(b, t, d)
                return xs.reshape(b, t, d) + mod[5] * f, states, v
            gate = jnp.sum(jax.nn.one_hot(top_idx, n_exp, dtype=F32) * weights[..., None], axis=-2)
            gate_rep = jnp.repeat(gate, LANE, axis=-1)
            hid = swiglu_up(h2, moe_w1_b, moe_w3_b, gate_rep, e0=j * n_exp, n_e=n_exp)
            w2 = moe_w2_b
        xs = resid_matmul(hid, w2, xs, jnp.broadcast_to(mod[5], (b, 1, d)), t, j)
        return xs.reshape(b, t, d), states, v

    xl, xc = x, ctx
    v_first_l = v_first_c = None
    for l in range(depth):
        last = l == depth - 1
        m = mm(mm(cond, ada_a[l]), ada_b[l]) + ada_bias[l]
        mod_l = [m[:bsz, None, i * d:(i + 1) * d] for i in range(n_mod)]
        mod_c = [m[bsz:bsz + 1, None, i * d:(i + 1) * d] for i in range(n_mod)]

        xc_new, ctx_states, vc = tokens(xc, mod_c, l, "seq", pos_ctx, None, v_first_c, last)
        xl, _, vl = tokens(xl, mod_l, l, "grid", pos_lat, ctx_states, v_first_l, False)
        if l == 0:
            v_first_c, v_first_l = vc, vl
        if not last:
            xc = xc_new
    return rmsnorm(xl, final_norm_w)
```

```python
import functools
import math

import jax
import jax.numpy as jnp
from jax import lax
from jax.experimental import pallas as pl
from jax.experimental.pallas import tpu as pltpu

F32 = jnp.float32
BF16 = jnp.bfloat16

GRID_W = 64
FOURIER_GROUPS = 8
TOP_K = 2
NORM_EPS = 1e-6
GN_EPS = 64e-5
DECAY_OFFSET = 0.5
WKV_CHUNK = 64
WKV_HEADS = 32
MOE_TILE = 512
MOE_SPARSE_MIN_ROWS = 4096
DFT_N1 = 128
WKV_PREP_ROWS = 128
PREP_ROWS = 256
POST_ROWS = 256
LANE = 128
VMEM_LIMIT = 56 * 1024 * 1024

HI = lax.Precision.HIGHEST


def _sigmoid(x):
    return 0.5 * jnp.tanh(0.5 * x) + 0.5


def _pick(dim, target, align):
    if dim <= target:
        return dim
    t = (target // align) * align
    while t >= align:
        if dim % t == 0:
            return t
        t -= align
    return dim


def _mm_kernel(a_ref, b_ref, o_ref, acc_ref, *, nk):
    k = pl.program_id(2)

    @pl.when(k == 0)
    def _():
        acc_ref[...] = jnp.zeros_like(acc_ref)

    acc_ref[...] += jnp.dot(a_ref[...], b_ref[...], preferred_element_type=F32)

    @pl.when(k == nk - 1)
    def _():
        o_ref[...] = acc_ref[...].astype(o_ref.dtype)


def matmul(a, b, out_dtype=F32, tm=1024, tn=1024, tk=2048, n_cols=None, col_off=0, planes=1, layer=None):
    m = a.shape[0]
    k = b.shape[-2]
    n = b.shape[-1] if n_cols is None else n_cols
    tm = _pick(m, tm, 16)
    tn = _pick(math.gcd(n // planes, col_off) if col_off else n // planes, tn, LANE)
    tk = _pick(k, tk, LANE)
    nk = k // tk
    joff = col_off // tn
    if planes == 1:
        out_spec = pl.BlockSpec((tm, tn), lambda i, j, kk: (i, j))
        out_shape = jax.ShapeDtypeStruct((m, n), out_dtype)
    else:
        per = n // planes // tn
        out_spec = pl.BlockSpec((None, tm, tn), lambda i, j, kk: (j // per, i, j % per))
        out_shape = jax.ShapeDtypeStruct((planes, m, n // planes), out_dtype)
    if layer is None:
        b_spec = pl.BlockSpec((tk, tn), lambda i, j, kk: (kk, j + joff))
    else:
        b_spec = pl.BlockSpec((None, tk, tn), lambda i, j, kk: (layer, kk, j + joff))
    return pl.pallas_call(
        functools.partial(_mm_kernel, nk=nk),
        grid=(m // tm, n // tn, nk),
        in_specs=[pl.BlockSpec((tm, tk), lambda i, j, kk: (i, kk)), b_spec],
        out_specs=out_spec,
        out_shape=out_shape,
        scratch_shapes=[pltpu.VMEM((tm, tn), F32)],
        compiler_params=pltpu.CompilerParams(
            dimension_semantics=("parallel", "parallel", "arbitrary"),
            vmem_limit_bytes=VMEM_LIMIT),
        name="matmul",
    )(a, b)


def mm(a, b, out_dtype=F32, **kw):
    lead = a.shape[:-1]
    out = matmul(a.reshape(-1, a.shape[-1]).astype(BF16), b.astype(BF16), out_dtype, **kw)
    return out.reshape(lead + (b.shape[-1],))


def _params(sem):
    return pltpu.CompilerParams(dimension_semantics=sem, vmem_limit_bytes=VMEM_LIMIT)


def _swiglu_up_kernel(a_ref, w1_ref, w3_ref, *rest, nk, gated):
    if gated:
        g_ref, o_ref, acc1_ref, acc3_ref = rest
    else:
        o_ref, acc1_ref, acc3_ref = rest
    k = pl.program_id(2)

    @pl.when(k == 0)
    def _():
        acc1_ref[...] = jnp.zeros_like(acc1_ref)
        acc3_ref[...] = jnp.zeros_like(acc3_ref)

    a = a_ref[...]
    acc1_ref[...] += jnp.dot(a, w1_ref[...], preferred_element_type=F32)
    acc3_ref[...] += jnp.dot(a, w3_ref[...], preferred_element_type=F32)

    @pl.when(k == nk - 1)
    def _():
        h1 = acc1_ref[...]
        hid = h1 * _sigmoid(h1) * acc3_ref[...]
        if gated:
            hid = hid * jnp.tile(g_ref[...], (1, hid.shape[1] // LANE))
        o_ref[...] = hid.astype(o_ref.dtype)


def swiglu_up(a, w1, w3, gate_rep=None, e0=0, n_e=None):
    m, k = a.shape
    f = w1.shape[-1]
    n_e = w1.shape[0] if n_e is None else n_e
    n = n_e * f
    tm = _pick(m, 1024, 16)
    tk = _pick(k, 2048, LANE)
    tn = _pick(f, 1024, LANE)
    gated = gate_rep is not None
    nk = k // tk
    per = f // tn
    w_spec = pl.BlockSpec((None, tk, tn), lambda i, j, kk: (e0 + j // per, kk, j % per))
    in_specs = [pl.BlockSpec((tm, tk), lambda i, j, kk: (i, kk)), w_spec, w_spec]
    args = [a, w1, w3]
    if gated:
        in_specs.append(pl.BlockSpec((tm, LANE), lambda i, j, kk: (i, j // per)))
        args.append(gate_rep)
    return pl.pallas_call(
        functools.partial(_swiglu_up_kernel, nk=nk, gated=gated),
        grid=(m // tm, n // tn, nk),
        in_specs=in_specs,
        out_specs=pl.BlockSpec((tm, tn), lambda i, j, kk: (i, j)),
        out_shape=jax.ShapeDtypeStruct((m, n), BF16),
        scratch_shapes=[pltpu.VMEM((tm, tn), F32), pltpu.VMEM((tm, tn), F32)],
        compiler_params=_params(("parallel", "parallel", "arbitrary")),
        name="swiglu_up",
    )(*args)


def _resid_kernel(a_ref, w_ref, x_ref, g_ref, o_ref, acc_ref, *, nk):
    k = pl.program_id(2)

    @pl.when(k == 0)
    def _():
        acc_ref[...] = jnp.zeros_like(acc_ref)

    acc_ref[...] += jnp.dot(a_ref[...], w_ref[...], preferred_element_type=F32)

    @pl.when(k == nk - 1)
    def _():
        o_ref[...] = x_ref[...] + g_ref[0] * acc_ref[...]


def resid_matmul(a, w, x, g, rows_per_batch, layer):
    m, k = a.shape
    n = w.shape[-1]
    tm = _pick(rows_per_batch, 1024, 16)
    tn = _pick(n, 1024, LANE)
    tk = _pick(k, 2048, LANE)
    nk = k // tk
    per = rows_per_batch // tm
    return pl.pallas_call(
        functools.partial(_resid_kernel, nk=nk),
        grid=(m // tm, n // tn, nk),
        in_specs=[pl.BlockSpec((tm, tk), lambda i, j, kk: (i, kk)),
                  pl.BlockSpec((None, tk, tn), lambda i, j, kk: (layer, kk, j)),
                  pl.BlockSpec((tm, tn), lambda i, j, kk: (i, j)),
                  pl.BlockSpec((1, 1, tn), lambda i, j, kk: (i // per, 0, j))],
        out_specs=pl.BlockSpec((tm, tn), lambda i, j, kk: (i, j)),
        out_shape=jax.ShapeDtypeStruct((m, n), F32),
        scratch_shapes=[pltpu.VMEM((tm, tn), F32)],
        compiler_params=_params(("parallel", "parallel", "arbitrary")),
        name="resid_matmul",
    )(a, w, x, g)


def _grouped_up_kernel(te_ref, a_ref, w1_ref, w3_ref, o_ref, acc1_ref, acc3_ref, *, nk):
    del te_ref
    k = pl.program_id(2)

    @pl.when(k == 0)
    def _():
        acc1_ref[...] = jnp.zeros_like(acc1_ref)
        acc3_ref[...] = jnp.zeros_like(acc3_ref)

    a = a_ref[...]
    acc1_ref[...] += jnp.dot(a, w1_ref[...], preferred_element_type=F32)
    acc3_ref[...] += jnp.dot(a, w3_ref[...], preferred_element_type=F32)

    @pl.when(k == nk - 1)
    def _():
        h1 = acc1_ref[...]
        o_ref[...] = (h1 * _sigmoid(h1) * acc3_ref[...]).astype(o_ref.dtype)


def _grouped_down_kernel(te_ref, a_ref, w_ref, g_ref, o_ref):
    del te_ref
    y = jnp.dot(a_ref[...], w_ref[...], preferred_element_type=F32)
    o_ref[...] = (y * jnp.tile(g_ref[...], (1, y.shape[1] // LANE))).astype(o_ref.dtype)


def grouped_swiglu(x, tile_expert, row_gate, w1, w3, w2, tm):
    r, d = x.shape
    f = w1.shape[-1]
    tk = _pick(d, 2048, LANE)
    tn = _pick(f, 1024, LANE)
    nk = d // tk
    hid = pl.pallas_call(
        functools.partial(_grouped_up_kernel, nk=nk),
        grid_spec=pltpu.PrefetchScalarGridSpec(
            num_scalar_prefetch=1,
            grid=(r // tm, f // tn, nk),
            in_specs=[pl.BlockSpec((tm, tk), lambda i, j, kk, te: (i, kk)),
                      pl.BlockSpec((None, tk, tn), lambda i, j, kk, te: (te[i], kk, j)),
                      pl.BlockSpec((None, tk, tn), lambda i, j, kk, te: (te[i], kk, j))],
            out_specs=pl.BlockSpec((tm, tn), lambda i, j, kk, te: (i, j)),
            scratch_shapes=[pltpu.VMEM((tm, tn), F32), pltpu.VMEM((tm, tn), F32)]),
        out_shape=jax.ShapeDtypeStruct((r, f), BF16),
        compiler_params=_params(("parallel", "parallel", "arbitrary")),
        name="grouped_up",
    )(tile_expert, x, w1, w3)
    tn2 = _pick(d, 1024, LANE)
    return pl.pallas_call(
        _grouped_down_kernel,
        grid_spec=pltpu.PrefetchScalarGridSpec(
            num_scalar_prefetch=1,
            grid=(r // tm, d // tn2),
            in_specs=[pl.BlockSpec((tm, f), lambda i, j, te: (i, 0)),
                      pl.BlockSpec((None, f, tn2), lambda i, j, te: (te[i], 0, j)),
                      pl.BlockSpec((tm, LANE), lambda i, j, te: (i, 0))],
            out_specs=pl.BlockSpec((tm, tn2), lambda i, j, te: (i, j))),
        out_shape=jax.ShapeDtypeStruct((r, d), BF16),
        compiler_params=_params(("parallel", "parallel")),
        name="grouped_down",
    )(tile_expert, hid, w2, row_gate)


def moe_dispatch(top_idx, weights, n_exp, tm):
    m, k = top_idx.shape
    n = m * k
    e_flat = top_idx.reshape(n)
    order = jnp.argsort(e_flat, stable=True)
    e_sorted = e_flat[order]
    counts = jnp.sum(jax.nn.one_hot(e_flat, n_exp, dtype=jnp.int32), axis=0)
    padded = (counts + tm - 1) // tm * tm
    pad_end = jnp.cumsum(padded)
    pad_start = pad_end - padded
    start = jnp.cumsum(counts) - counts
    dest_sorted = pad_start[e_sorted] + jnp.arange(n, dtype=jnp.int32) - start[e_sorted]
    r = n + n_exp * tm

    def expert_of(rows):
        return jnp.minimum(jnp.sum(rows[:, None] >= pad_end[None, :], axis=1), n_exp - 1).astype(jnp.int32)

    rows = jnp.arange(r, dtype=jnp.int32)
    e_row = expert_of(rows)
    within = rows - pad_start[e_row]
    valid = within < counts[e_row]
    src_sorted = order[jnp.clip(start[e_row] + within, 0, n - 1)]
    src_token = jnp.where(valid, src_sorted // k, 0)
    row_gate = jnp.where(valid, weights.reshape(n)[src_sorted], 0.0)
    tile_expert = expert_of(jnp.arange(r // tm, dtype=jnp.int32) * tm)
    dest = dest_sorted[jnp.argsort(order)].reshape(m, k)
    return src_token, row_gate, tile_expert, dest


def _merge_kernel(yr_ref, yf_ref, yc_ref, lg_ref, pr_ref, pf_ref, pc_ref,
                  gwr_ref, gwf_ref, gwc_ref, gbr_ref, gbf_ref, gbc_ref, o_ref):
    lg = lg_ref[...]

    def branch(y_ref, p_ref, gw_ref, gb_ref):
        gate = _sigmoid(jnp.dot(lg, gw_ref[...], preferred_element_type=F32) + gb_ref[...])
        return gate * jnp.dot(y_ref[...], p_ref[...], preferred_element_type=F32)

    out = (branch(yr_ref, pr_ref, gwr_ref, gbr_ref) + branch(yf_ref, pf_ref, gwf_ref, gbf_ref)
           + branch(yc_ref, pc_ref, gwc_ref, gbc_ref))
    o_ref[...] = out.astype(o_ref.dtype)


def gated_merge(yr, yf, yc, lg, pr, pf, pc, gw2, gb, layer):
    m = yr.shape[0]
    d = pr.shape[-1]
    tm = _pick(m, 1024, 16)
    tn = _pick(d, 512, LANE)
    nj = d // tn

    def rows(arr):
        return pl.BlockSpec((tm, arr.shape[1]), lambda i, j: (i, 0))

    def cols(arr, off):
        return pl.BlockSpec((None, arr.shape[1], tn), lambda i, j: (layer, 0, off * nj + j))

    return pl.pallas_call(
        _merge_kernel,
        grid=(m // tm, nj),
        in_specs=[rows(yr), rows(yf), rows(yc), rows(lg), cols(pr, 0), cols(pf, 0), cols(pc, 0),
                  cols(gw2, 0), cols(gw2, 1), cols(gw2, 2), cols(gb, 0), cols(gb, 1), cols(gb, 2)],
        out_specs=pl.BlockSpec((tm, tn), lambda i, j: (i, j)),
        out_shape=jax.ShapeDtypeStruct((m, d), BF16),
        compiler_params=_params(("parallel", "parallel")),
        name="gated_merge",
    )(yr, yf, yc, lg, pr, pf, pc, gw2, gw2, gw2, gb, gb, gb)


def _bf(x):
    return x.astype(BF16)


def _dot(a, b):
    return jnp.dot(_bf(a), _bf(b), preferred_element_type=F32)


def _dot_nt(a, b):
    return lax.dot_general(_bf(a), _bf(b), (((1,), (1,)), ((), ())), preferred_element_type=F32)


def _dot_tn(a, b):
    return lax.dot_general(_bf(a), _bf(b), (((0,), (0,)), ((), ())), preferred_element_type=F32)


def _each(f, *lists):
    return [f(*xs) for xs in zip(*lists)]


def _block_diag(x, width, pack):
    if pack == 1:
        return x
    lane_g = lax.broadcasted_iota(jnp.int32, x.shape, 1) // width
    zero = jnp.zeros_like(x)
    return jnp.concatenate([jnp.where(lane_g == g, x, zero) for g in range(pack)], axis=0)


def _unit_tri_inverse(a_tri, row, col, size, pack):
    def same_block(shift):
        return (row >> shift) == (col >> shift)

    def idot(a, b):
        return _dot(a, _block_diag(_bf(b), size, pack))

    eye = (row == col).astype(F32)
    a8 = _each(lambda a: jnp.where(same_block(3), a, 0.0), a_tri)
    a8_2 = _each(idot, a8, a8)
    a8_4 = _each(idot, a8_2, a8_2)
    x = _each(lambda a: eye - a, a8)
    x = _each(lambda xx, d: xx + d, x, _each(idot, x, a8_2))
    x = _each(lambda xx, d: xx + d, x, _each(idot, x, a8_4))
    shift = 3
    while (1 << shift) < size:
        off = same_block(shift + 1) & jnp.logical_not(same_block(shift))
        e = _each(lambda a: jnp.where(off, a, 0.0), a_tri)
        ex = _each(idot, e, x)
        x = _each(lambda xx, d: xx - d, x, _each(idot, x, ex))
        shift += 1
    return x


def _wkv_kernel(*refs, chunk, hd, heads, pack, dirs, nc):
    nd = len(dirs)
    ins = [refs[6 * d:6 * d + 6] for d in range(nd)]
    s0_refs = refs[6 * nd:7 * nd]
    y_refs = refs[7 * nd:8 * nd]
    sf_refs = refs[8 * nd:9 * nd]
    st_ref, al_ref, be_ref, rt_ref, kt_ref, vb_ref, pl_ref = refs[9 * nd:]
    c = pl.program_id(2)

    @pl.when(c == 0)
    def _():
        for d in range(nd):
            st_ref[d] = s0_refs[d][0]

    row = lax.broadcasted_iota(jnp.int32, (chunk, pack * chunk), 0)
    col = lax.broadcasted_iota(jnp.int32, (chunk, pack * chunk), 1) % chunk
    width = pack * hd
    groups = heads // pack
    sls = [slice(g * width, (g + 1) * width) for g in range(groups)]
    strict, incl = [], []
    for d, reverse in enumerate(dirs):
        st_d, in_d = (col > row, col >= row) if reverse else (col < row, col <= row)
        strict += [st_d] * groups
        incl += [in_d] * groups
        tri = jnp.where(in_d[:, :chunk], 1.0, 0.0).astype(BF16)
        last = 0 if reverse else chunk - 1
        r_ref, lw_ref, k_ref, v_ref, kk_ref, a_ref = ins[d]

        lw = lw_ref[0]
        lw_hi = _bf(lw)
        rem = lw - lw_hi.astype(F32)
        lw_mid = _bf(rem)
        lw_lo = _bf(rem - lw_mid.astype(F32))
        cum = (jnp.dot(tri, lw_hi, preferred_element_type=F32)
               + jnp.dot(tri, lw_mid, preferred_element_type=F32)
               + jnp.dot(tri, lw_lo, preferred_element_type=F32))
        p_in = jnp.exp(cum)
        p_inv = jnp.exp(-cum)
        kk = kk_ref[0].astype(F32)
        rt_ref[d] = _bf(r_ref[0].astype(F32) * p_in)
        kt_ref[d] = _bf(k_ref[0].astype(F32) * p_inv)
        be_ref[d] = _bf(a_ref[0].astype(F32) * kk * p_inv)
        al_ref[d] = _bf(kk * jnp.exp(cum - lw))
        vb_ref[d] = _bf(v_ref[0])
        pl_ref[d] = jnp.broadcast_to(p_in[last:last + 1, :], pl_ref.shape[1:])

    ent = [(d, g) for d in range(nd) for g in range(groups)]
    alpha = [al_ref[d, :, sls[g]] for d, g in ent]
    beta = [be_ref[d, :, sls[g]] for d, g in ent]
    rt = [rt_ref[d, :, sls[g]] for d, g in ent]
    kt = [kt_ref[d, :, sls[g]] for d, g in ent]
    v = [vb_ref[d, :, sls[g]] for d, g in ent]
    st0 = [st_ref[d, g] for d, g in ent]
    st0_b = _each(_bf, st0)

    def bd_keys(x):
        return _block_diag(x, hd, pack)

    def bdot(a, x):
        return _dot(a, _block_diag(_bf(x), hd, pack))

    ar = _each(lambda x, y: jnp.concatenate([x, y], axis=0), alpha, rt)
    x_b = _each(_dot_nt, ar, _each(bd_keys, beta))
    x_k = _each(_dot_nt, ar, _each(bd_keys, kt))
    a_ab = _each(lambda x, m: jnp.where(m, x[:chunk], 0.0), x_b, strict)
    a_rb = _each(lambda x, m: jnp.where(m, x[chunk:], 0.0), x_b, incl)
    a_ak = _each(lambda x, m: jnp.where(m, x[:chunk], 0.0), x_k, strict)
    a_rk = _each(lambda x, m: jnp.where(m, x[chunk:], 0.0), x_k, incl)
    t_inv = _unit_tri_inverse(a_ab, row, col, chunk, pack)

    w_t = _each(bdot, t_inv, alpha)
    u0 = _each(bdot, t_inv, _each(bdot, a_ak, v))
    y0 = _each(bdot, a_rk, v)
    ktv = _each(_dot_tn, kt, v)
    u = _each(lambda x, y: x + y, _each(_dot, w_t, st0_b), u0)
    y1 = _each(_dot, rt, st0_b)
    y2 = _each(bdot, a_rb, u)
    btu = _each(_dot_tn, beta, u)
    p_col = [jnp.transpose(pl_ref[d, :, sls[g]])[:, :1] for d, g in ent]
    own = (lax.broadcasted_iota(jnp.int32, (width, width), 0) // hd
           == lax.broadcasted_iota(jnp.int32, (width, width), 1) // hd)
    for n, (d, g) in enumerate(ent):
        y_refs[d][0, :, sls[g]] = y0[n] + y1[n] - y2[n]
        st_ref[d, g] = (st0[n] + jnp.where(own, ktv[n] - btu[n], 0.0)) * p_col[n]

    @pl.when(c == nc - 1)
    def _():
        for d in range(nd):
            sf_refs[d][0] = st_ref[d]


def wkv_pack(nh, hd):
    pack = LANE // hd if hd < LANE and LANE % hd == 0 and hd == WKV_CHUNK else 1
    return pack if nh % pack == 0 else 1


def wkv_state_shape(b, nh, hd):
    pack = wkv_pack(nh, hd)
    return (b, nh // pack, pack * hd, pack * hd)


def wkv(r, v, kk, per_dir, s0, hd, dirs=(False, True)):
    b, t, width = r.shape
    nh = width // hd
    pack = wkv_pack(nh, hd)
    heads = max(hh for hh in (1, 2, 4, 8, 16, 32) if nh % hh == 0 and hh <= WKV_HEADS and hh % pack == 0)
    chunk = WKV_CHUNK
    nc = t // chunk
    blk = heads * hd
    groups = heads // pack
    gw = pack * hd
    nd = len(dirs)

    def tok_spec(reverse):
        return pl.BlockSpec((1, chunk, blk), lambda bi, hi, ci: (bi, nc - 1 - ci if reverse else ci, hi))

    st_spec = pl.BlockSpec((1, groups, gw, gw), lambda bi, hi, ci: (bi, hi, 0, 0))
    in_specs, args = [], []
    for d, reverse in enumerate(dirs):
        lw, k, a = per_dir[d]
        in_specs += [tok_spec(reverse)] * 6
        args += [r, lw, k, v, kk, a]
    in_specs += [st_spec] * nd
    args += list(s0)
    outs = pl.pallas_call(
        functools.partial(_wkv_kernel, chunk=chunk, hd=hd, heads=heads, pack=pack, dirs=tuple(dirs), nc=nc),
        grid=(b, nh // heads, nc),
        in_specs=in_specs,
        out_specs=[tok_spec(reverse) for reverse in dirs] + [st_spec] * nd,
        out_shape=[jax.ShapeDtypeStruct((b, t, width), F32)] * nd
        + [jax.ShapeDtypeStruct(wkv_state_shape(b, nh, hd), F32)] * nd,
        scratch_shapes=[pltpu.VMEM((nd, groups, gw, gw), F32)]
        + [pltpu.VMEM((nd, chunk, blk), BF16)] * 5 + [pltpu.VMEM((nd, 8, blk), F32)],
        compiler_params=pltpu.CompilerParams(
            dimension_semantics=("parallel", "parallel", "arbitrary"),
            vmem_limit_bytes=VMEM_LIMIT),
        name="wkv",
    )(*args)
    return outs[:nd], outs[nd:]


def _from_prev(x):
    return pltpu.roll(x, 1, axis=0)


def _from_next(x):
    return pltpu.roll(x, x.shape[0] - 1, axis=0)


def _token_shift(x, up, dn, has_up, has_dn, mode):
    rows, ch = x.shape
    if mode == "seq":
        half = ch // 2
        t = lax.broadcasted_iota(jnp.int32, (rows, half), 0)
        s0 = jnp.where(t == 0, 0.0, _from_prev(x[:, :half]))
        s1 = jnp.where(t == rows - 1, 0.0, _from_next(x[:, half:]))
        return [s0, s1]
    q = ch // 4
    t = lax.broadcasted_iota(jnp.int32, (rows, q), 0) % GRID_W
    s0 = jnp.where(t == 0, 0.0, _from_prev(x[:, :q]))
    s1 = jnp.where(t == GRID_W - 1, 0.0, _from_next(x[:, q:2 * q]))
    up = jnp.where(has_up, up, 0.0)
    dn = jnp.where(has_dn, dn, 0.0)
    if rows == GRID_W:
        s2, s3 = up, dn
    else:
        s2 = jnp.concatenate([up, x[:rows - GRID_W, 2 * q:3 * q]], axis=0)
        s3 = jnp.concatenate([x[GRID_W:, 3 * q:], dn], axis=0)
    return [s0, s1, s2, s3]


def _prep_kernel(*refs, mode, nblk):
    if mode == "grid":
        x_ref, up_ref, dn_ref, w_ref, sh_ref, sc_ref, o_ref = refs
    else:
        x_ref, w_ref, sh_ref, sc_ref, o_ref = refs
    i = pl.program_id(1)
    w, sh, sc = w_ref[...], sh_ref[0], sc_ref[0]

    def modulated(x):
        y = x * lax.rsqrt(jnp.mean(x * x, axis=-1, keepdims=True) + NORM_EPS)
        return (y * w) * (1 + sc) + sh

    h = modulated(x_ref[0])
    d = h.shape[1]
    if mode == "grid":
        q = d // 4
        up = modulated(up_ref[0])[:, 2 * q:3 * q]
        dn = modulated(dn_ref[0])[:, 3 * q:]
        parts = _token_shift(h, up, dn, i > 0, i < nblk - 1, mode)
    else:
        parts = _token_shift(h, None, None, None, None, mode)
    o_ref[:, :d] = h.astype(o_ref.dtype)
    width = d // len(parts)
    for n, s in enumerate(parts):
        lo = n * width
        o_ref[:, d + lo:d + lo + width] = (s - h[:, lo:lo + width]).astype(o_ref.dtype)


def prep(x, w, shift, scale, mode):
    b, t, d = x.shape
    rows = _pick(t, PREP_ROWS, GRID_W) if mode == "grid" else t
    nblk = t // rows
    per = rows // GRID_W
    x_spec = pl.BlockSpec((1, rows, d), lambda bi, i: (bi, i, 0))
    vec_spec = pl.BlockSpec((1, d), lambda bi, i: (0, 0))
    mod_spec = pl.BlockSpec((1, 1, d), lambda bi, i: (bi, 0, 0))
    in_specs, args = [x_spec], [x]
    if mode == "grid":
        last = t // GRID_W - 1
        in_specs += [pl.BlockSpec((1, GRID_W, d), lambda bi, i: (bi, jnp.maximum(i * per - 1, 0), 0)),
                     pl.BlockSpec((1, GRID_W, d), lambda bi, i: (bi, jnp.minimum((i + 1) * per, last), 0))]
        args += [x, x]
    in_specs += [vec_spec, mod_spec, mod_spec]
    args += [w[None, :], shift, scale]
    return pl.pallas_call(
        functools.partial(_prep_kernel, mode=mode, nblk=nblk),
        grid=(b, nblk),
        in_specs=in_specs,
        out_specs=pl.BlockSpec((rows, 2 * d), lambda bi, i: (bi * nblk + i, 0)),
        out_shape=jax.ShapeDtypeStruct((b * t, 2 * d), BF16),
        compiler_params=_params(("parallel", "parallel")),
        name="prep_" + mode,
    )(*args)


def _norm_mod_kernel(x_ref, w_ref, sh_ref, sc_ref, *rest, routed):
    x = x_ref[...]
    y = x * lax.rsqrt(jnp.mean(x * x, axis=-1, keepdims=True) + NORM_EPS)
    h = (y * w_ref[...]) * (1 + sc_ref[0]) + sh_ref[0]
    if routed:
        rw_ref, rb_ref, o_ref, lg_ref = rest
        lg_ref[...] = jnp.dot(h, rw_ref[...], precision=HI, preferred_element_type=F32) + rb_ref[...]
    else:
        (o_ref,) = rest
    o_ref[...] = h.astype(o_ref.dtype)


def norm_mod(x, w, shift, scale, rows_per_batch, router_w=None, router_b=None):
    m, d = x.shape
    rows = _pick(rows_per_batch, PREP_ROWS, 8)
    nblk = rows_per_batch // rows
    routed = router_w is not None
    x_spec = pl.BlockSpec((rows, d), lambda bi, i: (bi * nblk + i, 0))
    mod_spec = pl.BlockSpec((1, 1, d), lambda bi, i: (bi, 0, 0))
    in_specs = [x_spec, pl.BlockSpec((1, d), lambda bi, i: (0, 0)), mod_spec, mod_spec]
    args = [x, w[None, :], shift, scale]
    out_specs, out_shape = [x_spec], [jax.ShapeDtypeStruct((m, d), BF16)]
    if routed:
        n_e = router_w.shape[1]
        pad = -n_e % LANE
        in_specs += [pl.BlockSpec((d, n_e + pad), lambda bi, i: (0, 0)),
                     pl.BlockSpec((1, n_e + pad), lambda bi, i: (0, 0))]
        args += [jnp.pad(router_w, ((0, 0), (0, pad))), jnp.pad(router_b, (0, pad))[None, :]]
        out_specs.append(pl.BlockSpec((rows, n_e + pad), lambda bi, i: (bi * nblk + i, 0)))
        out_shape.append(jax.ShapeDtypeStruct((m, n_e + pad), F32))
    out = pl.pallas_call(
        functools.partial(_norm_mod_kernel, routed=routed),
        grid=(m // rows_per_batch, nblk),
        in_specs=in_specs,
        out_specs=out_specs,
        out_shape=out_shape,
        compiler_params=_params(("parallel", "parallel")),
        name="norm_mod",
    )(*args)
    return (out[0], out[1][:, :router_w.shape[1]]) if routed else (out[0], None)


def _head_sums(x, hd):
    rows, width = x.shape
    nt = width // LANE
    stacked = jnp.concatenate([x[:, i * LANE:(i + 1) * LANE] for i in range(nt)], axis=0)
    li = lax.broadcasted_iota(jnp.int32, (LANE, LANE), 0) // hd
    lj = lax.broadcasted_iota(jnp.int32, (LANE, LANE), 1) // hd
    ones = jnp.where(li == lj, 1.0, 0.0).astype(BF16)
    hi = _bf(stacked)
    lo = _bf(stacked - hi.astype(F32))
    s = jnp.dot(hi, ones, preferred_element_type=F32) + jnp.dot(lo, ones, preferred_element_type=F32)
    return jnp.concatenate([s[i * rows:(i + 1) * rows] for i in range(nt)], axis=1)


def _wkv_prep_kernel(*refs, mode, nblk, hd, offs, has_vres):
    refs = list(refs)
    main = [refs.pop(0) for _ in range(3)]
    halo = [(refs.pop(0), refs.pop(0)) for _ in range(3)] if mode == "grid" else [(None, None)] * 3
    low_ref = refs.pop(0)
    vf_ref = refs.pop(0) if has_vres else None
    mu_ref, w0_ref, a0_ref, kk_ref, ka_ref, w2_ref, a2_ref = [refs.pop(0) for _ in range(7)]
    if has_vres:
        v0_ref, v2_ref = refs.pop(0), refs.pop(0)
    r_o, v_o, kk_o, kdf_o, kdr_o, af_o, ar_o, lwf_o, lwr_o = refs
    c = pl.program_id(1)
    low = low_ref[0]

    def lerp(n):
        x = main[n][0].astype(F32)
        up, dn = ((halo[n][0][0].astype(F32), halo[n][1][0].astype(F32)) if mode == "grid"
                  else (None, None))
        sh = jnp.concatenate(_token_shift(x, up, dn, c > 0, c < nblk - 1, mode), axis=1)
        return x + (sh - x) * mu_ref[n:n + 1, :]

    def low_dot(lo, hi_, w, act=None):
        z = low[:, lo:hi_]
        if act is not None:
            z = act(z)
        return jnp.dot(_bf(z), w, preferred_element_type=F32)

    r = lerp(0)
    k = lerp(1)
    v = lerp(2)
    if has_vres:
        gate = _sigmoid(v0_ref[...] + low_dot(offs[6], offs[7], v2_ref[...]))
        v = v + (vf_ref[0].astype(F32) - v) * gate
    kq = k * kk_ref[...]
    kk = kq * lax.rsqrt(jnp.maximum(_head_sums(kq * kq, hd), 1e-24))
    r_o[0] = r.astype(r_o.dtype)
    v_o[0] = v.astype(v_o.dtype)
    kk_o[0] = kk.astype(kk_o.dtype)
    for di, (kd_o, a_o, lw_o) in enumerate(((kdf_o, af_o, lwf_o), (kdr_o, ar_o, lwr_o))):
        wl = w0_ref[di:di + 1, :] + low_dot(offs[di], offs[di + 1], w2_ref[di], jnp.tanh)
        lw_o[0] = _sigmoid(wl) * (-math.exp(-DECAY_OFFSET))
        a = _sigmoid(a0_ref[di:di + 1, :] + low_dot(offs[2 + di], offs[3 + di], a2_ref[di]))
        a_o[0] = a.astype(a_o.dtype)
        kd_o[0] = (k * (1.0 + (a - 1.0) * ka_ref[...])).astype(kd_o.dtype)


def wkv_prep(p, low, v_first, mu, w0, a0, k_k, k_a, w2, a2, v0, v2, offs, hd, mode):
    b, t, rw3 = p.shape
    rw = rw3 // 3
    q = rw // 4
    rows = _pick(t, WKV_PREP_ROWS, GRID_W) if mode == "grid" else t
    per, last = rows // GRID_W, t // GRID_W - 1
    nblk = t // rows
    has_vres = v_first is not None

    def tok(width, col):
        return pl.BlockSpec((1, rows, width), lambda bi, ci: (bi, ci, col))

    def full(arr):
        nd = arr.ndim
        return pl.BlockSpec(arr.shape, lambda bi, ci: (0,) * nd)

    in_specs = [tok(rw, n) for n in range(3)]
    args = [p, p, p]
    if mode == "grid":
        for n in range(3):
            in_specs += [pl.BlockSpec((1, GRID_W, q),
                                      lambda bi, ci, n=n: (bi, jnp.maximum(ci * per - 1, 0), 4 * n + 2)),
                         pl.BlockSpec((1, GRID_W, q),
                                      lambda bi, ci, n=n: (bi, jnp.minimum((ci + 1) * per, last), 4 * n + 3))]
            args += [p, p]
    in_specs.append(tok(low.shape[-1], 0))
    args.append(low)
    if has_vres:
        in_specs.append(tok(rw, 0))
        args.append(v_first)
    small = [mu, w0, a0, k_k[None, :], k_a[None, :], w2.astype(BF16), a2.astype(BF16)]
    if has_vres:
        small += [v0[None, :], v2.astype(BF16)]
    in_specs += [full(s) for s in small]
    args += small
    out_spec = tok(rw, 0)
    shapes = [jax.ShapeDtypeStruct((b, t, rw), BF16)] * 7 + [jax.ShapeDtypeStruct((b, t, rw), F32)] * 2
    return pl.pallas_call(
        functools.partial(_wkv_prep_kernel, mode=mode, nblk=nblk, hd=hd, offs=tuple(offs), has_vres=has_vres),
        grid=(b, nblk),
        in_specs=in_specs,
        out_specs=[out_spec] * 9,
        out_shape=shapes,
        compiler_params=_params(("parallel", "parallel")),
        name="wkv_prep_" + mode,
    )(*args)


def _wkv_post_kernel(yf_ref, yr_ref, r_ref, kdf_ref, kdr_ref, v_ref, lg_ref, rk_ref, gw_ref, gb_ref,
                     g2_ref, o_ref, *, hd):
    y = yf_ref[0] + yr_ref[0]
    mean = _head_sums(y, hd) * (1.0 / hd)
    yc = y - mean
    var = _head_sums(yc * yc, hd) * (1.0 / hd)
    y = yc * lax.rsqrt(var + GN_EPS) * gw_ref[...] + gb_ref[...]
    rk = r_ref[0].astype(F32) * (kdf_ref[0].astype(F32) + kdr_ref[0].astype(F32)) * rk_ref[...]
    y = y + _head_sums(rk, hd) * v_ref[0].astype(F32)
    g = jnp.dot(_bf(_sigmoid(lg_ref[0])), g2_ref[...], preferred_element_type=F32)
    o_ref[0] = (y * g).astype(o_ref.dtype)


def wkv_post(y_f, y_r, r, kd_f, kd_r, v, low_g, r_k, gn_w, gn_b, g2, hd):
    b, t, rw = y_f.shape
    rows = _pick(t, POST_ROWS, 8)

    def tok(width):
        return pl.BlockSpec((1, rows, width), lambda bi, ci: (bi, ci, 0))

    def full(arr):
        return pl.BlockSpec(arr.shape, lambda bi, ci: (0, 0))

    small = [r_k.reshape(1, rw), gn_w[None, :], gn_b[None, :], g2.astype(BF16)]
    return pl.pallas_call(
        functools.partial(_wkv_post_kernel, hd=hd),
        grid=(b, t // rows),
        in_specs=[tok(rw)] * 6 + [tok(low_g.shape[-1])] + [full(s) for s in small],
        out_specs=tok(rw),
        out_shape=jax.ShapeDtypeStruct((b, t, rw), BF16),
        compiler_params=_params(("parallel", "parallel")),
        name="wkv_post",
    )(y_f, y_r, r, kd_f, kd_r, v, low_g, *small)


def rmsnorm(x, w):
    y = x * lax.rsqrt(jnp.mean(x * x, axis=-1, keepdims=True) + NORM_EPS)
    return y * w


def modulate(x, w, shift, scale):
    return rmsnorm(x, w) * (1 + scale) + shift


def dft_tables(rows, cols, n, scale):
    j = lax.broadcasted_iota(jnp.int32, (rows, cols), 0)
    k = lax.broadcasted_iota(jnp.int32, (rows, cols), 1)
    ang = ((j * k) % n).astype(F32) * (2.0 * math.pi / n)
    return jnp.cos(ang) * scale, jnp.sin(ang) * scale


def _dft_stage_kernel(l_ref, zr_ref, zi_ref, *rest, twiddle):
    z = jnp.concatenate([zr_ref[...], zi_ref[...]], axis=0)
    y = jnp.dot(l_ref[...], z, preferred_element_type=F32)
    if not twiddle:
        rest[0][...] = y.astype(rest[0].dtype)
        return
    twr_ref, twi_ref, o_ref = rest
    half = y.shape[0] // 2
    reps = y.shape[1] // LANE
    tr, ti = jnp.tile(twr_ref[...], (1, reps)), jnp.tile(twi_ref[...], (1, reps))
    yr, yi = y[:half], y[half:]
    o_ref[0] = (yr * tr - yi * ti).astype(o_ref.dtype)
    o_ref[1] = (yr * ti + yi * tr).astype(o_ref.dtype)


class PosDft:
    def __init__(self, n):
        self.n = n
        self.n1 = DFT_N1 if n % DFT_N1 == 0 and n // DFT_N1 >= 8 and (n // DFT_N1) % 8 == 0 else 0
        if not self.n1:
            c, s = dft_tables(n, n, n, n ** -0.5)
            self.direct = jnp.concatenate([c, s], axis=1).astype(BF16)
            return
        n1, n2 = self.n1, n // self.n1
        self.n2 = n2
        c1, s1 = dft_tables(n1, n1, n1, n1 ** -0.5)
        self.l1 = jnp.concatenate([jnp.concatenate([c1, s1], axis=1),
                                   jnp.concatenate([-s1, c1], axis=1)], axis=0).astype(BF16)
        tc, ts = dft_tables(n2, n1, n, 1.0)
        self.twr = jnp.broadcast_to(tc[:, :, None], (n2, n1, LANE))
        self.twi = jnp.broadcast_to(-ts[:, :, None], (n2, n1, LANE))
        c2, s2 = dft_tables(n2, n2, n2, n2 ** -0.5)
        self.l3 = jnp.concatenate([c2, s2], axis=1).astype(BF16)


def fourier_mix(u, pos, chan_table):
    b, n, fw = u.shape
    z = matmul(u.reshape(b * n, fw).astype(BF16), chan_table, BF16, planes=2)
    if not pos.n1:
        z = z.reshape(2, b, n, fw).transpose(1, 0, 2, 3).reshape(b, 2 * n, fw)
        return jnp.stack([matmul(pos.direct, z[i], BF16) for i in range(b)], axis=0)
    n1, n2 = pos.n1, pos.n2
    z = z.reshape(2, b, n1, n2 * fw)
    y = pl.pallas_call(
        functools.partial(_dft_stage_kernel, twiddle=True),
        grid=(b, n2),
        in_specs=[pl.BlockSpec((2 * n1, 2 * n1), lambda bi, j: (0, 0)),
                  pl.BlockSpec((None, None, n1, fw), lambda bi, j: (0, bi, 0, j)),
                  pl.BlockSpec((None, None, n1, fw), lambda bi, j: (1, bi, 0, j)),
                  pl.BlockSpec((None, n1, LANE), lambda bi, j: (j, 0, 0)),
                  pl.BlockSpec((None, n1, LANE), lambda bi, j: (j, 0, 0))],
        out_specs=pl.BlockSpec((None, 2, None, n1, fw), lambda bi, j: (bi, 0, j, 0, 0)),
        out_shape=jax.ShapeDtypeStruct((b, 2, n2, n1, fw), BF16),
        compiler_params=_params(("parallel", "parallel")),
        name="dft_stage1",
    )(pos.l1, z, z, pos.twr, pos.twi)
    y = y.reshape(b, 2, n2, n1 * fw)
    tn = _pick(n1 * fw, 4096, LANE)
    out = pl.pallas_call(
        functools.partial(_dft_stage_kernel, twiddle=False),
        grid=(b, n1 * fw // tn),
        in_specs=[pl.BlockSpec((n2, 2 * n2), lambda bi, j: (0, 0)),
                  pl.BlockSpec((None, None, n2, tn), lambda bi, j: (bi, 0, 0, j)),
                  pl.BlockSpec((None, None, n2, tn), lambda bi, j: (bi, 1, 0, j))],
        out_specs=pl.BlockSpec((None, n2, tn), lambda bi, j: (bi, 0, j)),
        out_shape=jax.ShapeDtypeStruct((b, n2, n1 * fw), BF16),
        compiler_params=_params(("parallel", "parallel")),
        name="dft_stage2",
    )(pos.l3, y, y)
    return out.reshape(b, n, fw)


def depthwise_conv(u, w):
    n = u.shape[1]
    kk = w.shape[0]
    pad = kk // 2
    up = jnp.pad(u, ((0, 0), (pad, pad), (0, 0)))
    out = up[:, 0:n] * w[0]
    for i in range(1, kk):
        out = out + up[:, i:i + n] * w[i]
    return out


def kernel(x, c, ctx, c_ctx, ada_a, ada_b, ada_bias, norm1_w, norm2_w, w_in, mu_rkv, mu_lr, decay_w0, decay_w1, decay_w2, iclr_a0, iclr_a1, iclr_a2, ogate_g1, ogate_g2, k_k, k_a, r_k, gn_w, gn_b, vres_mu, vres_v0, vres_v1, vres_v2, conv_w, gate_w1, gate_w2, gate_b, proj_rwkv, proj_fourier, proj_conv, w_out, ffn_w1, ffn_w3, ffn_w2, router_w, router_b, moe_w1, moe_w3, moe_w2, final_norm_w):
    depth = w_in.shape[0]
    bsz, seq, d = x.shape
    ctx_len = ctx.shape[1]
    rw = mu_rkv.shape[-1]
    nh, hd = r_k.shape[1], r_k.shape[2]
    fw = proj_fourier.shape[1]
    cw = proj_conv.shape[1]
    four_off = 3 * rw
    conv_off = four_off + fw
    n_exp = router_w.shape[-1]
    d_exp = moe_w1.shape[-1]
    n_mod = ada_b.shape[-1] // d
    r_dec, r_icl, r_og, r_vr, r_gate = (decay_w1.shape[-1], iclr_a1.shape[-1], ogate_g1.shape[-1],
                                        vres_v1.shape[-1], gate_w1.shape[-1])

    gc = fw // FOURIER_GROUPS
    cc, cs = dft_tables(gc, gc, gc, gc ** -0.5)
    eye_g = jnp.eye(FOURIER_GROUPS, dtype=F32)
    chan_table = jnp.concatenate([jnp.kron(eye_g, cc), -jnp.kron(eye_g, cs)], axis=1).astype(BF16)
    pos_lat, pos_ctx = PosDft(seq), PosDft(ctx_len)

    w_in_b, w_out_b = w_in.astype(BF16), w_out.astype(BF16)
    proj_r_b, proj_f_b, proj_c_b = proj_rwkv.astype(BF16), proj_fourier.astype(BF16), proj_conv.astype(BF16)
    gate_w2_b, gate_b3 = gate_w2.astype(BF16), gate_b[:, None, :]
    ffn_w1_b, ffn_w3_b, ffn_w2_b = ffn_w1.astype(BF16), ffn_w3.astype(BF16), ffn_w2.astype(BF16)
    moe_w1_b = moe_w1.astype(BF16).reshape((-1,) + moe_w1.shape[2:])
    moe_w3_b = moe_w3.astype(BF16).reshape((-1,) + moe_w3.shape[2:])
    moe_w2_b = moe_w2.astype(BF16).reshape(moe_w2.shape[0], n_exp * d_exp, d)
    moe_w2_e = moe_w2_b.reshape(-1, d_exp, d)

    cond_lat = jax.nn.silu(c)
    cond_ctx = jax.nn.silu(c_ctx)[None, :]
    cond = jnp.concatenate([cond_lat, cond_ctx], axis=0)
    cond = jnp.pad(cond, ((0, 16 - cond.shape[0] % 16), (0, 0)))

    def tokens(xs, mod, l, mode, pos_tab, s0, v_first, last_ctx):
        b, t, _ = xs.shape
        has_vres = l > 0
        hd_cat = prep(xs, norm1_w[l], jnp.broadcast_to(mod[0], (b, 1, d)),
                      jnp.broadcast_to(mod[1], (b, 1, d)), mode)

        w_h = [decay_w1[l, 0], decay_w1[l, 1], iclr_a1[l, 0], iclr_a1[l, 1], ogate_g1[l], gate_w1[l]]
        mus = [mu_lr[l, 0], mu_lr[l, 0], mu_lr[l, 1], mu_lr[l, 1], mu_lr[l, 2], None]
        if has_vres:
            w_h.append(vres_v1[l - 1])
            mus.append(vres_mu[l - 1])
        w_dh = [jnp.zeros_like(w) if m is None else w * m[:, None] for w, m in zip(w_h, mus)]
        w_low = jnp.concatenate([jnp.concatenate(w_h, axis=1), jnp.concatenate(w_dh, axis=1)], axis=0)
        n_low = w_low.shape[1]
        w_low = jnp.pad(w_low, ((0, 0), (0, -n_low % LANE)))
        low = matmul(hd_cat, w_low.astype(BF16)).reshape(b, t, -1)
        offs = [0]
        for w in w_h:
            offs.append(offs[-1] + w.shape[1])
        low_g = low[..., offs[4]:offs[5]]
        low_gate = low[..., offs[5]:offs[6]]

        p = matmul(hd_cat, w_in_b, BF16, n_cols=four_off, layer=l).reshape(b, t, -1)
        p_fc = matmul(hd_cat, w_in_b, BF16, n_cols=w_in_b.shape[-1] - four_off,
                      col_off=four_off, layer=l).reshape(b, t, -1)

        r, v, kk, kd_f, kd_r, a_f, a_r, lw_f, lw_r = wkv_prep(
            p, low, v_first, mu_rkv[l], decay_w0[l], iclr_a0[l], k_k[l], k_a[l], decay_w2[l], iclr_a2[l],
            vres_v0[l - 1] if has_vres else None, vres_v2[l - 1] if has_vres else None, offs, hd, mode)

        if s0 is None:
            s0 = [jnp.zeros(wkv_state_shape(b, nh, hd), F32)] * 2
        ys, states = wkv(r, v, kk, [(lw_f, kd_f, a_f), (lw_r, kd_r, a_r)], s0, hd)
        if last_ctx:
            return None, states, v

        y_rwkv = wkv_post(ys[0], ys[1], r, kd_f, kd_r, v, low_g, r_k[l], gn_w[l], gn_b[l],
                          ogate_g2[l], hd)

        y_four = fourier_mix(p_fc[..., :fw], pos_tab, chan_table)
        gate_b_ = p_fc[..., fw:fw + cw].astype(F32)
        gate_c_ = p_fc[..., fw + cw:fw + 2 * cw].astype(F32)
        uu = p_fc[..., fw + 2 * cw:].astype(F32)
        y_conv = gate_b_ * depthwise_conv(gate_c_ * uu, conv_w[l])

        rows = b * t

        def flat(z):
            return z.reshape(rows, -1).astype(BF16)

        merged = gated_merge(flat(y_rwkv), flat(y_four), flat(y_conv), flat(low_gate),
                             proj_r_b, proj_f_b, proj_c_b, gate_w2_b, gate_b3, l)
        xs = resid_matmul(merged, w_out_b, xs.reshape(rows, d),
                          jnp.broadcast_to(mod[2], (b, 1, d)), t, l)

        j = l // 2
        dense = l % 2 == 0
        h2, logits = norm_mod(xs, norm2_w[l], jnp.broadcast_to(mod[3], (b, 1, d)),
                              jnp.broadcast_to(mod[4], (b, 1, d)), t,
                              None if dense else router_w[j], None if dense else router_b[j])
        if dense:
            hid = swiglu_up(h2, ffn_w1_b, ffn_w3_b, e0=j, n_e=1)
            w2 = ffn_w2_b
        else:
            top_val, top_idx = lax.top_k(logits, TOP_K)
            weights = jax.nn.softmax(top_val, axis=-1)
            if rows >= MOE_SPARSE_MIN_ROWS:
                src, row_gate, tile_e, dest = moe_dispatch(top_idx, weights, n_exp, MOE_TILE)
                y = grouped_swiglu(h2[src], tile_e + j * n_exp,
                                   jnp.broadcast_to(row_gate[:, None], (row_gate.shape[0], LANE)),
                                   moe_w1_b, moe_w3_b, moe_w2_e, MOE_TILE)
                f = sum(y[dest[:, s]].astype(F32) for s in range(TOP_K)).reshape(b, t, d)
                return xs.reshape(b, t, d) + mod[5] * f, states, v
            gate = jnp.sum(jax.nn.one_hot(top_idx, n_exp, dtype=F32) * weights[..., None], axis=-2)
            gate_rep = jnp.repeat(gate, LANE, axis=-1)
            hid = swiglu_up(h2, moe_w1_b, moe_w3_b, gate_rep, e0=j * n_exp, n_e=n_exp)
            w2 = moe_w2_b
        xs = resid_matmul(hid, w2, xs, jnp.broadcast_to(mod[5], (b, 1, d)), t, j)
        return xs.reshape(b, t, d), states, v

    xl, xc = x, ctx
    v_first_l = v_first_c = None
    for l in range(depth):
        last = l == depth - 1
        m = mm(mm(cond, ada_a[l]), ada_b[l]) + ada_bias[l]
        mod_l = [m[:bsz, None, i * d:(i + 1) * d] for i in range(n_mod)]
        mod_c = [m[bsz:bsz + 1, None, i * d:(i + 1) * d] for i in range(n_mod)]

        xc_new, ctx_states, vc = tokens(xc, mod_c, l, "seq", pos_ctx, None, v_first_c, last)
        xl, _, vl = tokens(xl, mod_l, l, "grid", pos_lat, ctx_states, v_first_l, False)
        if l == 0:
            v_first_c, v_first_l = vc, vl
        if not last:
            xc = xc_new
    return rmsnorm(xl, final_norm_w)
```
